```python
import math
import jax, jax.numpy as jnp
from jax import lax
import numpy as np

D_MODEL = 4096
BATCH = 4
SEQ = 4096
DEPTH = 1

HEAD_DIM = 128
N_HEADS = D_MODEL // HEAD_DIM
HEADS_A = N_HEADS // 2
HEADS_B = N_HEADS - HEADS_A
KV_GROUPS_B = 4
REP_B = HEADS_B // KV_GROUPS_B
WIDTH_A = HEADS_A * HEAD_DIM
WIDTH_B = HEADS_B * HEAD_DIM
KV_WIDTH_B = KV_GROUPS_B * HEAD_DIM
DILATED_CONFIGS = ((128, 1), (512, 4), (2048, 16))
BLK = 128
CMP_LEN = 32
CMP_STRIDE = 16
SLC_BLOCK = 64
N_SELECT = 16
WIN = 512
FORCE_SCORE = 1e6
ROPE_THETA = 500000.0
ROPE_DIM = HEAD_DIM // 4
D_FF = -(-8 * D_MODEL // (3 * 256)) * 256
EPS = 1e-5
IN_SIZES = (WIDTH_A, WIDTH_A, WIDTH_A,
            WIDTH_B,
            KV_WIDTH_B, KV_WIDTH_B,
            KV_WIDTH_B, KV_WIDTH_B,
            KV_WIDTH_B, KV_WIDTH_B,
            3 * HEADS_B)
D_IN = sum(IN_SIZES)
IN_SPLIT_POINTS = tuple(int(v) for v in np.cumsum(IN_SIZES)[:-1])

kernel_name = "hybrid_dilated_nsa_block"


def rmsnorm(x, g):
    xf = x.astype(jnp.float32)
    y = xf * lax.rsqrt(jnp.mean(xf * xf, axis=-1, keepdims=True) + EPS)
    return (y * g.astype(jnp.float32)).astype(x.dtype)


def rope_tables(s):
    inv = ROPE_THETA ** (-jnp.arange(0, ROPE_DIM, 2, dtype=jnp.float32) / ROPE_DIM)
    ang = jnp.arange(s, dtype=jnp.float32)[:, None] * inv[None, :]
    return jnp.cos(ang), jnp.sin(ang)


def apply_rope(x, cos, sin):
    half = ROPE_DIM // 2
    x1 = x[..., :half].astype(jnp.float32)
    x2 = x[..., half:ROPE_DIM].astype(jnp.float32)
    r = jnp.concatenate([x1 * cos - x2 * sin, x1 * sin + x2 * cos], axis=-1).astype(x.dtype)
    return jnp.concatenate([r, x[..., ROPE_DIM:]], axis=-1)


def heads(t, n):
    b, s, _ = t.shape
    return t.reshape(b, s, n, HEAD_DIM).transpose(0, 2, 1, 3)


def banded_attention(q, k, v, max_dist):
    b, g, r, l, dh = q.shape
    nb = l // BLK
    n_prev = -(-max_dist // BLK)
    nk = (n_prev + 1) * BLK
    starts = jnp.arange(nb) * BLK
    kidx = starts[:, None] + jnp.arange(nk)[None, :] - n_prev * BLK
    kcl = jnp.clip(kidx, 0, l - 1)
    kb = jnp.take(k, kcl, axis=2)
    vb = jnp.take(v, kcl, axis=2)
    qpos = starts[:, None] + jnp.arange(BLK)[None, :]
    dist = qpos[:, :, None] - kidx[:, None, :]
    mask = (kidx[:, None, :] >= 0) & (dist >= 0) & (dist <= max_dist)
    qb = q.reshape(b, g, r, nb, BLK, dh)
    s = jnp.einsum('bgrnqd,bgnkd->bgrnqk', qb, kb).astype(jnp.float32) * (HEAD_DIM ** -0.5)
    s = jnp.where(mask, s, -jnp.inf)
    lse = jax.nn.logsumexp(s, axis=-1)
    p = jnp.exp(s - lse[..., None])
    o = jnp.einsum('bgrnqk,bgnkd->bgrnqd', p.astype(v.dtype), vb)
    return o.reshape(b, g, r, l, dh), lse.reshape(b, g, r, l)


def dilated_attention(q, k, v):
    b, h, s, dh = q.shape
    outs, lses = [], []
    for window, dil in DILATED_CONFIGS:
        span = dil * BLK
        sp = -(-s // span) * span
        n_sub = sp // dil

        def strided(t):
            t = jnp.pad(t, ((0, 0), (0, 0), (0, sp - s), (0, 0)))
            return t.reshape(b, h, n_sub, dil, dh).transpose(0, 1, 3, 2, 4).reshape(b, h * dil, n_sub, dh)

        o, lse = banded_attention(strided(q)[:, :, None], strided(k), strided(v), window // dil)
        o = o[:, :, 0].reshape(b, h, dil, n_sub, dh).transpose(0, 1, 3, 2, 4).reshape(b, h, sp, dh)[:, :, :s]
        lse = lse[:, :, 0].reshape(b, h, dil, n_sub).transpose(0, 1, 3, 2).reshape(b, h, sp)[:, :, :s]
        outs.append(o)
        lses.append(lse)
    w = jax.nn.softmax(jnp.stack(lses), axis=0)
    return jnp.einsum('cbhs,cbhsd->bhsd', w.astype(q.dtype), jnp.stack(outs))


def compress(t, pe, w1, w2):
    s = t.shape[2]
    n_c = (s - CMP_LEN) // CMP_STRIDE + 1
    idx = jnp.arange(n_c)[:, None] * CMP_STRIDE + jnp.arange(CMP_LEN)[None, :]
    blocks = jnp.take(t, idx, axis=2) + pe.astype(t.dtype)
    flat = blocks.reshape(blocks.shape[:3] + (CMP_LEN * HEAD_DIM,))
    return jax.nn.gelu(flat @ w1) @ w2


def nsa_attention(q, k_cmp, v_cmp, k_slc, v_slc, k_win, v_win, gates,
                  ck_pe, ck_w1, ck_w2, cv_pe, cv_w1, cv_w2):
    b, g, r, s, dh = q.shape
    scale = HEAD_DIM ** -0.5
    pos = jnp.arange(s)

    kc = compress(k_cmp, ck_pe, ck_w1, ck_w2)
    vc = compress(v_cmp, cv_pe, cv_w1, cv_w2)
    n_c = kc.shape[2]
    c_start = jnp.arange(n_c) * CMP_STRIDE
    c_mask = (c_start + CMP_LEN - 1)[None, :] <= pos[:, None]
    sc = jnp.einsum('bgrsd,bgcd->bgrsc', q, kc).astype(jnp.float32) * scale
    sc = jnp.where(c_mask, sc, -jnp.inf)
    m = jnp.max(sc, axis=-1, keepdims=True)
    m = jnp.where(jnp.isfinite(m), m, 0.0)
    e = jnp.exp(sc - m)
    p_cmp = e / jnp.maximum(jnp.sum(e, axis=-1, keepdims=True), 1e-30)
    o_cmp = jnp.einsum('bgrsc,bgcd->bgrsd', p_cmp.astype(vc.dtype), vc)

    n_s = s // SLC_BLOCK
    s_start = jnp.arange(n_s) * SLC_BLOCK
    span_hits = ((c_start[:, None] < s_start[None, :] + SLC_BLOCK) &
                 (c_start[:, None] + CMP_LEN > s_start[None, :])).astype(jnp.float32)
    imp = jnp.einsum('bgrsc,cj->bgsj', p_cmp, span_hits)
    j = jnp.arange(n_s)[None, :]
    qblk = (pos // SLC_BLOCK)[:, None]
    valid = j * SLC_BLOCK <= pos[:, None]
    forced = (j == 0) | (j == qblk) | (j == qblk - 1)
    score = jnp.where(forced, FORCE_SCORE, jnp.where(valid, imp, -jnp.inf))
    n_top = min(N_SELECT, n_s)
    top_val, top_idx = lax.top_k(score, n_top)
    top_ok = jnp.isfinite(top_val)

    nb = s // BLK
    gather_bg = jax.vmap(jax.vmap(lambda table, idx: table[idx]))
    q_blocks = jnp.moveaxis(q.reshape(b, g, r, nb, BLK, dh), 3, 0)
    idx_blocks = jnp.moveaxis(top_idx.reshape(b, g, nb, BLK, n_top), 2, 0)
    ok_blocks = jnp.moveaxis(top_ok.reshape(b, g, nb, BLK, n_top), 2, 0)
    starts = jnp.arange(nb) * BLK

    def slc_block(args):
        qb, ib, okb, start = args
        tok = (ib[..., None] * SLC_BLOCK + jnp.arange(SLC_BLOCK)).reshape(b, g, BLK, n_top * SLC_BLOCK)
        tok_ok = jnp.broadcast_to(okb[..., None], ib.shape + (SLC_BLOCK,)).reshape(b, g, BLK, n_top * SLC_BLOCK)
        qpos = start + jnp.arange(BLK)
        mask = tok_ok & (tok <= qpos[:, None])
        kg = gather_bg(k_slc, tok)
        vg = gather_bg(v_slc, tok)
        ss = jnp.einsum('bgrqd,bgqtd->bgrqt', qb, kg).astype(jnp.float32) * scale
        ss = jnp.where(mask[:, :, None], ss, -jnp.inf)
        p = jax.nn.softmax(ss, axis=-1)
        return jnp.einsum('bgrqt,bgqtd->bgrqd', p.astype(vg.dtype), vg)

    o_slc = lax.map(slc_block, (q_blocks, idx_blocks, ok_blocks, starts))
    o_slc = jnp.moveaxis(o_slc, 0, 3).reshape(b, g, r, s, dh)

    o_win, _ = banded_attention(q, k_win, v_win, WIN - 1)

    return gates[..., 0:1] * o_cmp + gates[..., 1:2] * o_slc + gates[..., 2:3] * o_win


def setup_inputs(seed: int = 0) -> dict:
    key = jax.random.key(seed)
    ks = jax.random.split(key, 17)
    f32 = jnp.float32
    nrm = lambda k, shape, scale: jax.random.normal(k, shape, f32) * scale
    gain = lambda k, shape: 1.0 + 0.02 * jax.random.normal(k, shape, f32)
    return {
        "x": jax.random.normal(ks[0], (BATCH, SEQ, D_MODEL), f32),
        "norm_attn": gain(ks[1], (DEPTH, D_MODEL)),
        "w_in": nrm(ks[2], (DEPTH, D_MODEL, D_IN), D_MODEL ** -0.5),
        "ck_pe": nrm(ks[3], (DEPTH, CMP_LEN, HEAD_DIM), 0.1),
        "ck_w1": nrm(ks[4], (DEPTH, CMP_LEN * HEAD_DIM, HEAD_DIM), (CMP_LEN * HEAD_DIM) ** -0.5),
        "ck_w2": nrm(ks[5], (DEPTH, HEAD_DIM, HEAD_DIM), HEAD_DIM ** -0.5),
        "cv_pe": nrm(ks[6], (DEPTH, CMP_LEN, HEAD_DIM), 0.1),
        "cv_w1": nrm(ks[7], (DEPTH, CMP_LEN * HEAD_DIM, HEAD_DIM), (CMP_LEN * HEAD_DIM) ** -0.5),
        "cv_w2": nrm(ks[8], (DEPTH, HEAD_DIM, HEAD_DIM), HEAD_DIM ** -0.5),
        "out_norm_a": gain(ks[9], (DEPTH, WIDTH_A)),
        "out_norm_b": gain(ks[10], (DEPTH, WIDTH_B)),
        "w_out": nrm(ks[11], (DEPTH, D_MODEL, D_MODEL), D_MODEL ** -0.5),
        "norm_ffn": gain(ks[12], (DEPTH, D_MODEL)),
        "w_gate": nrm(ks[13], (DEPTH, D_MODEL, D_FF), D_MODEL ** -0.5),
        "w_up": nrm(ks[14], (DEPTH, D_MODEL, D_FF), D_MODEL ** -0.5),
        "w_down": nrm(ks[15], (DEPTH, D_FF, D_MODEL), D_FF ** -0.5),
        "norm_final": gain(ks[16], (D_MODEL,)),
    }


def reference(x, norm_attn, w_in, ck_pe, ck_w1, ck_w2, cv_pe, cv_w1, cv_w2,
              out_norm_a, out_norm_b, w_out, norm_ffn, w_gate, w_up, w_down, norm_final):
    b, s, _ = x.shape
    cos, sin = rope_tables(s)
    for l in range(DEPTH):
        h = rmsnorm(x, norm_attn[l])
        proj = h @ w_in[l]
        (qa, ka, va, qb, kbc, vbc, kbs, vbs, kbw, vbw, gl) = jnp.split(proj, IN_SPLIT_POINTS, axis=-1)

        qa = apply_rope(heads(qa, HEADS_A), cos, sin)
        ka = apply_rope(heads(ka, HEADS_A), cos, sin)
        o_a = dilated_attention(qa, ka, heads(va, HEADS_A))

        q_b = apply_rope(heads(qb, HEADS_B), cos, sin).reshape(b, KV_GROUPS_B, REP_B, s, HEAD_DIM)
        k_c = apply_rope(heads(kbc, KV_GROUPS_B), cos, sin)
        k_s = apply_rope(heads(kbs, KV_GROUPS_B), cos, sin)
        k_w = apply_rope(heads(kbw, KV_GROUPS_B), cos, sin)
        gates = jax.nn.sigmoid(gl.reshape(b, s, KV_GROUPS_B, REP_B, 3).transpose(0, 2, 3, 1, 4))
        o_b = nsa_attention(q_b, k_c, heads(vbc, KV_GROUPS_B), k_s, heads(vbs, KV_GROUPS_B),
                            k_w, heads(vbw, KV_GROUPS_B), gates,
                            ck_pe[l], ck_w1[l], ck_w2[l], cv_pe[l], cv_w1[l], cv_w2[l])

        o_a = o_a.transpose(0, 2, 1, 3).reshape(b, s, WIDTH_A)
        o_b = o_b.reshape(b, HEADS_B, s, HEAD_DIM).transpose(0, 2, 1, 3).reshape(b, s, WIDTH_B)
        mixed = jnp.concatenate([rmsnorm(o_a, out_norm_a[l]), rmsnorm(o_b, out_norm_b[l])], axis=-1)
        x = x + mixed @ w_out[l]

        h = rmsnorm(x, norm_ffn[l])
        x = x + (jax.nn.silu(h @ w_gate[l]) * (h @ w_up[l])) @ w_down[l]
    return rmsnorm(x, norm_final)
```

```python
import functools

import jax
import jax.numpy as jnp
from jax import lax
from jax.experimental import pallas as pl
from jax.experimental.pallas import tpu as pltpu

F32 = jnp.float32
BF16 = jnp.bfloat16

D_MODEL = 4096
HEAD_DIM = 128
HEADS_A = 16
HEADS_B = 16
KV_GROUPS_B = 4
REP_B = HEADS_B // KV_GROUPS_B
WIDTH_A = HEADS_A * HEAD_DIM
WIDTH_B = HEADS_B * HEAD_DIM
DILATED_CONFIGS = ((128, 1), (512, 4), (2048, 16))
BLK = 128
CMP_LEN = 32
CMP_STRIDE = 16
SLC_BLOCK = 64
N_SELECT = 16
WIN = 512
FORCE_SCORE = 1e6
ROPE_THETA = 500000.0
ROPE_DIM = HEAD_DIM // 4
EPS = 1e-5
SCALE = HEAD_DIM ** -0.5
N_GATES = 3 * HEADS_B
D_MAIN = 3 * WIDTH_A + WIDTH_B + 6 * KV_GROUPS_B * HEAD_DIM

VMEM_LIMIT_BYTES = 56 * 1024 * 1024
LANES = 128
NEG = -1e30
SEL_BIAS = -32768.0

PROJ_TILE = 4 * HEAD_DIM
F32_TILES = tuple(range(12)) + (16, 17)
BF16_TILES = (12, 13, 14, 15, 18, 19, 20, 21)
Q_TILES = (0, 1, 2, 3, 12, 13, 14, 15)
V_TILES = (8, 9, 10, 11, 17, 19, 21)
PF_QA, PF_KA, PF_VA, PF_KC, PF_VC = 0, 16, 32, 48, 52
PB_QB, PB_KS, PB_VS, PB_KW, PB_VW = 0, 16, 20, 24, 28


def _params(*sem):
    return pltpu.CompilerParams(dimension_semantics=sem, vmem_limit_bytes=VMEM_LIMIT_BYTES)


def _dot_nt(a, b):
    return lax.dot_general(a, b, (((1,), (1,)), ((), ())), preferred_element_type=F32)


def _rms(x, g):
    return x * lax.rsqrt(jnp.mean(x * x, axis=-1, keepdims=True) + EPS) * g


def _rms_gate_kernel(x_ref, g_ref, wgl_ref, h_ref, gl_ref):
    hb = _rms(x_ref[...], g_ref[...]).astype(BF16)
    h_ref[...] = hb
    gl_ref[...] = jnp.dot(hb, wgl_ref[...], preferred_element_type=F32)


def rms_gate(x, gain, w_gl, tm=256):
    m, d = x.shape
    ng = w_gl.shape[1]
    return pl.pallas_call(
        _rms_gate_kernel,
        grid=(m // tm,),
        in_specs=[pl.BlockSpec((tm, d), lambda i: (i, 0)),
                  pl.BlockSpec((1, d), lambda i: (0, 0)),
                  pl.BlockSpec((d, ng), lambda i: (0, 0))],
        out_specs=[pl.BlockSpec((tm, d), lambda i: (i, 0)),
                   pl.BlockSpec((tm, ng), lambda i: (i, 0))],
        out_shape=[jax.ShapeDtypeStruct((m, d), BF16), jax.ShapeDtypeStruct((m, ng), F32)],
        compiler_params=_params("parallel"),
        name="rms_gate",
    )(x, gain.reshape(1, d), w_gl)


def _rms_kernel(x_ref, g_ref, o_ref):
    o_ref[...] = _rms(x_ref[...], g_ref[...]).astype(o_ref.dtype)


def rms(x, gain, out_dtype, tm=256):
    m, d = x.shape
    return pl.pallas_call(
        _rms_kernel,
        grid=(m // tm,),
        in_specs=[pl.BlockSpec((tm, d), lambda i: (i, 0)),
                  pl.BlockSpec((1, d), lambda i: (0, 0))],
        out_specs=pl.BlockSpec((tm, d), lambda i: (i, 0)),
        out_shape=jax.ShapeDtypeStruct((m, d), out_dtype),
        compiler_params=_params("parallel"),
        name="rms",
    )(x, gain.reshape(1, d))


def _rms_pair_kernel(a_ref, b_ref, ga_ref, gb_ref, o_ref):
    wa = a_ref.shape[1]
    o_ref[:, :wa] = _rms(a_ref[...], ga_ref[...]).astype(o_ref.dtype)
    o_ref[:, wa:] = _rms(b_ref[...], gb_ref[...]).astype(o_ref.dtype)


def rms_pair(a, b, ga, gb, tm=256):
    m, wa = a.shape
    wb = b.shape[1]
    return pl.pallas_call(
        _rms_pair_kernel,
        grid=(m // tm,),
        in_specs=[pl.BlockSpec((tm, wa), lambda i: (i, 0)),
                  pl.BlockSpec((tm, wb), lambda i: (i, 0)),
                  pl.BlockSpec((1, wa), lambda i: (0, 0)),
                  pl.BlockSpec((1, wb), lambda i: (0, 0))],
        out_specs=pl.BlockSpec((tm, wa + wb), lambda i: (i, 0)),
        out_shape=jax.ShapeDtypeStruct((m, wa + wb), BF16),
        compiler_params=_params("parallel"),
        name="rms_pair",
    )(a, b, ga.reshape(1, wa), gb.reshape(1, wb))


def _rope_tables(seq):
    half = ROPE_DIM // 2
    inv = ROPE_THETA ** (-jnp.arange(0, ROPE_DIM, 2, dtype=F32) / ROPE_DIM)
    ang = jnp.arange(seq, dtype=F32)[:, None] * inv[None, :]
    cos, sin = jnp.cos(ang), jnp.sin(ang)
    ones = jnp.ones((seq, HEAD_DIM - ROPE_DIM), F32)
    cos_t = jnp.concatenate([cos, cos, ones], axis=1)
    sin_t = jnp.concatenate([-sin, sin, 0.0 * ones], axis=1)
    del half
    return cos_t, sin_t


def _proj_kernel(h_ref, w_ref, cos_ref, sin_ref, o_ref, *, src_tiles):
    y = jnp.dot(h_ref[...], w_ref[...], preferred_element_type=F32)
    jt = pl.program_id(1)
    is_v = functools.reduce(jnp.logical_or, [jt == n for n, t in enumerate(src_tiles) if t in V_TILES])
    is_q = functools.reduce(jnp.logical_or, [jt == n for n, t in enumerate(src_tiles) if t in Q_TILES])
    heads_per_tile = PROJ_TILE // HEAD_DIM

    @pl.when(is_v)
    def _():
        for hh in range(heads_per_tile):
            o_ref[hh] = y[:, hh * HEAD_DIM:(hh + 1) * HEAD_DIM].astype(o_ref.dtype)

    @pl.when(jnp.logical_not(is_v))
    def _():
        scale = jnp.where(is_q, SCALE, 1.0).astype(F32)
        cos_t = cos_ref[...] * scale
        sin_t = sin_ref[...] * scale
        lane = lax.broadcasted_iota(jnp.int32, (1, HEAD_DIM), 1)
        first_half = lane < ROPE_DIM // 2
        for hh in range(heads_per_tile):
            yh = y[:, hh * HEAD_DIM:(hh + 1) * HEAD_DIM]
            partner = jnp.where(first_half,
                                pltpu.roll(yh, HEAD_DIM - ROPE_DIM // 2, axis=1),
                                pltpu.roll(yh, ROPE_DIM // 2, axis=1))
            o_ref[hh] = (yh * cos_t + partner * sin_t).astype(o_ref.dtype)


def project(h, w, cos_t, sin_t, src_tiles, out_dtype, seq, tm=1024):
    m, d = h.shape
    n_tiles = len(src_tiles)
    heads_per_tile = PROJ_TILE // HEAD_DIM
    tm = min(tm, seq)
    pos_blocks = seq // tm

    def w_map(i, j):
        col = functools.reduce(lambda acc, nt: jnp.where(j == nt[0], nt[1], acc),
                               list(enumerate(src_tiles)), 0)
        return (0, col)

    return pl.pallas_call(
        functools.partial(_proj_kernel, src_tiles=src_tiles),
        grid=(m // tm, n_tiles),
        in_specs=[pl.BlockSpec((tm, d), lambda i, j: (i, 0)),
                  pl.BlockSpec((d, PROJ_TILE), w_map),
                  pl.BlockSpec((tm, HEAD_DIM), lambda i, j: (i % pos_blocks, 0)),
                  pl.BlockSpec((tm, HEAD_DIM), lambda i, j: (i % pos_blocks, 0))],
        out_specs=pl.BlockSpec((heads_per_tile, tm, HEAD_DIM), lambda i, j: (j, i, 0)),
        out_shape=jax.ShapeDtypeStruct((n_tiles * heads_per_tile, m, HEAD_DIM), out_dtype),
        compiler_params=_params("parallel", "arbitrary"),
        name="project",
    )(h, w, cos_t, sin_t)


def _dilated_kernel(q_ref, k_ref, v_ref, o_ref, acc_ref, m_ref, l_ref, *, seq):
    for ci, (window, dil) in enumerate(DILATED_CONFIGS):
        band = window // dil
        n_prev = -(-band // BLK)
        nk = (n_prev + 1) * BLK
        span = BLK * dil
        nb = seq // span

        def body(idx, carry, ci=ci, dil=dil, band=band, n_prev=n_prev, nk=nk, span=span, nb=nb):
            r = idx // nb
            i = idx - r * nb
            kb = jnp.maximum(i - n_prev, 0)
            q0 = r + i * span
            k0 = r + kb * span
            if dil == 1:
                q_rows = pl.ds(pl.multiple_of(q0, BLK), BLK)
                k_rows = pl.ds(pl.multiple_of(k0, BLK), nk)
            else:
                q_rows = pl.ds(q0, BLK, stride=dil)
                k_rows = pl.ds(k0, nk, stride=dil)
            q = q_ref[q_rows, :].astype(BF16)
            k = k_ref[k_rows, :].astype(BF16)
            v = v_ref[k_rows, :].astype(BF16)
            s = _dot_nt(q, k)
            qn = i * BLK + lax.broadcasted_iota(jnp.int32, (BLK, 1), 0)
            kn = kb * BLK + lax.broadcasted_iota(jnp.int32, (1, nk), 1)
            dist = qn - kn
            s = jnp.where((dist >= 0) & (dist <= band), s, NEG)
            m_tile = jnp.max(s, axis=1, keepdims=True)
            if ci == 0:
                m_new = jnp.broadcast_to(m_tile, (BLK, LANES))
            else:
                m_old = m_ref[q_rows, :]
                m_new = jnp.maximum(m_old, m_tile)
            p = jnp.exp(s - jnp.concatenate([m_new] * (nk // LANES), axis=1))
            l_tile = jnp.sum(p, axis=1, keepdims=True)
            pv = jnp.dot(p.astype(BF16), v, preferred_element_type=F32)
            if ci == 0:
                l_ref[q_rows, :] = jnp.broadcast_to(l_tile, (BLK, LANES))
                acc_ref[q_rows, :] = pv
            else:
                alpha = jnp.exp(m_old - m_new)
                l_ref[q_rows, :] = alpha * l_ref[q_rows, :] + l_tile
                acc_ref[q_rows, :] = alpha * acc_ref[q_rows, :] + pv
            m_ref[q_rows, :] = m_new
            return carry

        lax.fori_loop(0, seq // BLK, body, 0)
    o_ref[...] = acc_ref[...] / l_ref[...]


def dilated_attention(pf, batch, seq):
    assert seq % (BLK * DILATED_CONFIGS[-1][1]) == 0 and seq // DILATED_CONFIGS[-1][1] >= 2 * BLK
    pf4 = pf.reshape(pf.shape[0], batch, seq, HEAD_DIM)

    def spec(off):
        return pl.BlockSpec((None, None, seq, HEAD_DIM), lambda b, h: (off + h, b, 0, 0))

    return pl.pallas_call(
        functools.partial(_dilated_kernel, seq=seq),
        grid=(batch, HEADS_A),
        in_specs=[spec(PF_QA), spec(PF_KA), spec(PF_VA)],
        out_specs=pl.BlockSpec((seq, HEAD_DIM), lambda b, h: (b, h)),
        out_shape=jax.ShapeDtypeStruct((batch * seq, WIDTH_A), F32),
        scratch_shapes=[pltpu.VMEM((seq, HEAD_DIM), F32),
                        pltpu.VMEM((seq, LANES), F32),
                        pltpu.VMEM((seq, LANES), F32)],
        compiler_params=_params("parallel", "parallel"),
        name="dilated_attention",
    )(pf4, pf4, pf4)


def _compress_kernel(t_ref, pe_ref, w1_ref, w2_ref, o_ref, *, seq):
    n_chunks = seq // CMP_STRIDE
    first = jnp.zeros((n_chunks, HEAD_DIM), F32)
    second = jnp.zeros((n_chunks, HEAD_DIM), F32)
    for i in range(CMP_STRIDE):
        ti = t_ref[pl.ds(i, n_chunks, stride=CMP_STRIDE), :]
        first += jnp.dot((ti + pe_ref[pl.ds(i, 1), :]).astype(BF16), w1_ref[i],
                         preferred_element_type=F32)
        second += jnp.dot((ti + pe_ref[pl.ds(CMP_STRIDE + i, 1), :]).astype(BF16), w1_ref[CMP_STRIDE + i],
                          preferred_element_type=F32)
    pre = first + pltpu.roll(second, n_chunks - 1, axis=0)
    o_ref[...] = jnp.dot(jax.nn.gelu(pre).astype(BF16), w2_ref[...],
                         preferred_element_type=F32).astype(o_ref.dtype)


def compress(pf, pe, w1, w2, batch, seq):
    assert CMP_LEN == 2 * CMP_STRIDE
    n_chunks = seq // CMP_STRIDE
    pf4 = pf.reshape(pf.shape[0], batch, seq, HEAD_DIM)
    return pl.pallas_call(
        functools.partial(_compress_kernel, seq=seq),
        grid=(2, batch, KV_GROUPS_B),
        in_specs=[pl.BlockSpec((None, None, seq, HEAD_DIM), lambda kv, b, g: (PF_KC + kv * KV_GROUPS_B + g, b, 0, 0)),
                  pl.BlockSpec((None, CMP_LEN, HEAD_DIM), lambda kv, b, g: (kv, 0, 0)),
                  pl.BlockSpec((None, CMP_LEN, HEAD_DIM, HEAD_DIM), lambda kv, b, g: (kv, 0, 0, 0)),
                  pl.BlockSpec((None, HEAD_DIM, HEAD_DIM), lambda kv, b, g: (kv, 0, 0))],
        out_specs=pl.BlockSpec((None, None, n_chunks, HEAD_DIM), lambda kv, b, g: (kv, b * KV_GROUPS_B + g, 0, 0)),
        out_shape=jax.ShapeDtypeStruct((2, batch * KV_GROUPS_B, n_chunks, HEAD_DIM), BF16),
        compiler_params=_params("parallel", "parallel", "parallel"),
        name="compress",
    )(pf4, pe, w1, w2)


NSA_TQ = 128
NSA_TK = 256


def _select_blocks(score):
    n_s, tq = score.shape
    groups = n_s // 8
    rows8 = [score[8 * v:8 * v + 8, :] for v in range(groups)]
    rank8 = [jnp.zeros((8, tq), F32) for _ in range(groups)]
    sub = lax.broadcasted_iota(jnp.int32, (8, 1), 0)
    for jp in range(n_s):
        vp, sp = divmod(jp, 8)
        row = jnp.broadcast_to(rows8[vp][sp:sp + 1, :], (8, tq))
        for v in range(groups):
            ge = jnp.where(row >= rows8[v], 1.0, 0.0)
            gt = jnp.where(row > rows8[v], 1.0, 0.0)
            if v > vp:
                beats = ge
            elif v < vp:
                beats = gt
            else:
                beats = jnp.where(sub > sp, ge, gt)
            rank8[v] = rank8[v] + beats
    rank = jnp.concatenate(rank8, axis=0)
    return (rank < N_SELECT) & (score > -jnp.inf)


def _nsa_kernel(q_ref, kc_ref, vc_ref, ks_ref, vs_ref, kw_ref, vw_ref, gl_ref, o_ref, *, seq):
    tq, tk = NSA_TQ, NSA_TK
    rows = REP_B * tq
    n_cp = seq // CMP_STRIDE
    n_s = seq // SLC_BLOCK
    qi = pl.program_id(2)
    t0 = qi * tq
    q = q_ref[...].reshape(rows, HEAD_DIM)
    row_id = lax.broadcasted_iota(jnp.int32, (rows, 1), 0)
    qpos = t0 + (row_id & (tq - 1))

    sc = _dot_nt(q, kc_ref[...])
    cidx = lax.broadcasted_iota(jnp.int32, (1, n_cp), 1)
    c_ok = (cidx * CMP_STRIDE + (CMP_LEN - 1) <= qpos) & (cidx < n_cp - 1)
    sc = jnp.where(c_ok, sc, NEG)
    m_c = jnp.max(sc, axis=1, keepdims=True)
    e = jnp.where(c_ok, jnp.exp(sc - m_c), 0.0)
    p_cmp = e / jnp.maximum(jnp.sum(e, axis=1, keepdims=True), 1e-30)
    o_cmp = jnp.dot(p_cmp.astype(BF16), vc_ref[...], preferred_element_type=F32)

    p_sum = p_cmp[0:tq]
    for r in range(1, REP_B):
        p_sum = p_sum + p_cmp[r * tq:(r + 1) * tq]
    jj = lax.broadcasted_iota(jnp.int32, (n_s, 1), 0)
    ratio = SLC_BLOCK // CMP_STRIDE
    c_first = ratio * jj - (CMP_LEN // CMP_STRIDE - 1)
    hits = ((cidx >= c_first) & (cidx < ratio * (jj + 1))).astype(BF16)
    p_hi = p_sum.astype(BF16)
    p_lo = (p_sum - p_hi.astype(F32)).astype(BF16)
    imp = _dot_nt(hits, p_hi) + _dot_nt(hits, p_lo)
    tpos = t0 + lax.broadcasted_iota(jnp.int32, (1, tq), 1)
    qblk = tpos // SLC_BLOCK
    forced = (jj == 0) | (jj == qblk) | (jj == qblk - 1)
    valid = jj * SLC_BLOCK <= tpos
    score = jnp.where(forced, FORCE_SCORE, jnp.where(valid, imp, -jnp.inf))
    sel = _select_blocks(score)
    bias_t = jnp.where(sel, 0.0, SEL_BIAS)
    if n_s < LANES:
        bias_t = jnp.concatenate([bias_t, jnp.full((LANES - n_s, tq), SEL_BIAS, F32)], axis=0)
    bias = bias_t.T.astype(BF16)
    q_aug = jnp.concatenate([q, jnp.concatenate([bias] * REP_B, axis=0)], axis=1)

    lane_blk = lax.broadcasted_iota(jnp.int32, (1, LANES), 1)

    def slc_step(kt, carry, causal):
        m_i, l_i, acc = carry
        k0 = pl.multiple_of(kt * tk, tk)
        k = ks_ref[pl.ds(k0, tk), :]
        v = vs_ref[pl.ds(k0, tk), :]
        tok = k0 + lax.broadcasted_iota(jnp.int32, (tk, 1), 0)
        onehot = jnp.where(tok // SLC_BLOCK == lane_blk, 1.0, 0.0).astype(BF16)
        s = _dot_nt(q_aug, jnp.concatenate([k, onehot], axis=1))
        if causal:
            kpos = k0 + lax.broadcasted_iota(jnp.int32, (1, tk), 1)
            s = jnp.where(kpos <= qpos, s, NEG)
        m_new = jnp.maximum(m_i, jnp.max(s, axis=1, keepdims=True))
        alpha = jnp.exp(m_i - m_new)
        p = jnp.exp(s - m_new)
        l_new = alpha * l_i + jnp.sum(p, axis=1, keepdims=True)
        acc_new = alpha * acc + jnp.dot(p.astype(BF16), v, preferred_element_type=F32)
        return m_new, l_new, acc_new

    last = (t0 + tq - 1) // tk
    carry = (jnp.full((rows, 1), NEG, F32), jnp.zeros((rows, 1), F32), jnp.zeros((rows, HEAD_DIM), F32))
    carry = lax.fori_loop(0, last, functools.partial(slc_step, causal=False), carry)
    _, l_s, acc_s = slc_step(last, carry, causal=True)
    o_slc = acc_s / l_s

    ws = WIN + tq
    w0 = pl.multiple_of(jnp.maximum(t0 - WIN, 0), tq)
    kpos_w = w0 + lax.broadcasted_iota(jnp.int32, (1, ws), 1)
    dist = qpos - kpos_w
    s_w = jnp.where((dist >= 0) & (dist <= WIN - 1), _dot_nt(q, kw_ref[pl.ds(w0, ws), :]), NEG)
    p_w = jnp.exp(s_w - jnp.max(s_w, axis=1, keepdims=True))
    o_win = (jnp.dot(p_w.astype(BF16), vw_ref[pl.ds(w0, ws), :], preferred_element_type=F32)
             / jnp.sum(p_w, axis=1, keepdims=True))

    gates = jax.nn.sigmoid(gl_ref[...])
    for r in range(REP_B):
        sl = slice(r * tq, (r + 1) * tq)
        o_ref[:, r * HEAD_DIM:(r + 1) * HEAD_DIM] = (
            gates[:, 3 * r:3 * r + 1] * o_cmp[sl]
            + gates[:, 3 * r + 1:3 * r + 2] * o_slc[sl]
            + gates[:, 3 * r + 2:3 * r + 3] * o_win[sl])


def nsa_attention(pb, kvc, gl, batch, seq):
    tq = NSA_TQ
    assert seq % NSA_TK == 0 and seq >= WIN + tq and seq // SLC_BLOCK <= LANES and (seq // SLC_BLOCK) % 8 == 0
    n_cp = seq // CMP_STRIDE
    nq = seq // tq
    pb4 = pb.reshape(pb.shape[0], batch, seq, HEAD_DIM)

    def kv_spec(off):
        return pl.BlockSpec((None, None, seq, HEAD_DIM), lambda b, g, i: (off + g, b, 0, 0))

    def cmp_spec(kv):
        return pl.BlockSpec((None, None, n_cp, HEAD_DIM), lambda b, g, i: (kv, b * KV_GROUPS_B + g, 0, 0))

    return pl.pallas_call(
        functools.partial(_nsa_kernel, seq=seq),
        grid=(batch, KV_GROUPS_B, nq),
        in_specs=[pl.BlockSpec((REP_B, None, tq, HEAD_DIM), lambda b, g, i: (PB_QB // REP_B + g, b, i, 0)),
                  cmp_spec(0), cmp_spec(1),
                  kv_spec(PB_KS), kv_spec(PB_VS), kv_spec(PB_KW), kv_spec(PB_VW),
                  pl.BlockSpec((tq, LANES), lambda b, g, i: (b * nq + i, g))],
        out_specs=pl.BlockSpec((tq, REP_B * HEAD_DIM), lambda b, g, i: (b * nq + i, g)),
        out_shape=jax.ShapeDtypeStruct((batch * seq, WIDTH_B), F32),
        compiler_params=_params("parallel", "parallel", "arbitrary"),
        name="nsa_attention",
    )(pb4, kvc, kvc, pb4, pb4, pb4, pb4, gl)


def _mm_res_kernel(a_ref, w_ref, r_ref, o_ref):
    o_ref[...] = jnp.dot(a_ref[...], w_ref[...], preferred_element_type=F32) + r_ref[...]


def matmul_residual(a, w, res, tm, tn):
    m, k = a.shape
    n = w.shape[1]
    return pl.pallas_call(
        _mm_res_kernel,
        grid=(m // tm, n // tn),
        in_specs=[pl.BlockSpec((tm, k), lambda i, j: (i, 0)),
                  pl.BlockSpec((k, tn), lambda i, j: (0, j)),
                  pl.BlockSpec((tm, tn), lambda i, j: (i, j))],
        out_specs=pl.BlockSpec((tm, tn), lambda i, j: (i, j)),
        out_shape=jax.ShapeDtypeStruct((m, n), F32),
        compiler_params=_params("parallel", "arbitrary"),
        name="matmul_residual",
    )(a, w, res)


def _ffn_up_kernel(h_ref, wg_ref, wu_ref, o_ref):
    h = h_ref[...]
    g = jnp.dot(h, wg_ref[...], preferred_element_type=F32)
    u = jnp.dot(h, wu_ref[...], preferred_element_type=F32)
    o_ref[...] = (g * jax.nn.sigmoid(g) * u).astype(o_ref.dtype)


def ffn_up(h, wg, wu, tm, tn):
    m, k = h.shape
    n = wg.shape[1]
    return pl.pallas_call(
        _ffn_up_kernel,
        grid=(m // tm, n // tn),
        in_specs=[pl.BlockSpec((tm, k), lambda i, j: (i, 0)),
                  pl.BlockSpec((k, tn), lambda i, j: (0, j)),
                  pl.BlockSpec((k, tn), lambda i, j: (0, j))],
        out_specs=pl.BlockSpec((tm, tn), lambda i, j: (i, j)),
        out_shape=jax.ShapeDtypeStruct((m, n), BF16),
        compiler_params=_params("parallel", "arbitrary"),
        name="ffn_up",
    )(h, wg, wu)


def _layer(x, norm_attn, w_in, ck_pe, ck_w1, ck_w2, cv_pe, cv_w1, cv_w2,
           out_norm_a, out_norm_b, w_out, norm_ffn, w_gate, w_up, w_down, cos_t, sin_t, batch, seq):
    m = batch * seq
    w_main = w_in[:, :D_MAIN].astype(BF16)
    w_gl = w_in[:, D_MAIN:].reshape(D_MODEL, KV_GROUPS_B, REP_B * 3)
    w_gl = jnp.pad(w_gl, ((0, 0), (0, 0), (0, LANES - REP_B * 3))).reshape(D_MODEL, KV_GROUPS_B * LANES).astype(BF16)
    pe = jnp.stack([ck_pe, cv_pe])
    w1 = jnp.stack([ck_w1, cv_w1]).reshape(2, CMP_LEN, HEAD_DIM, HEAD_DIM).astype(BF16)
    w2 = jnp.stack([ck_w2, cv_w2]).astype(BF16)

    h, gl = rms_gate(x, norm_attn, w_gl)
    pf = project(h, w_main, cos_t, sin_t, F32_TILES, F32, seq)
    pb = project(h, w_main, cos_t, sin_t, BF16_TILES, BF16, seq)
    o_a = dilated_attention(pf, batch, seq)
    kvc = compress(pf, pe, w1, w2, batch, seq)
    o_b = nsa_attention(pb, kvc, gl, batch, seq)
    mixed = rms_pair(o_a, o_b, out_norm_a, out_norm_b)
    x1 = matmul_residual(mixed, w_out.astype(BF16), x, tm=1024, tn=512)
    h2 = rms(x1, norm_ffn, BF16)
    act = ffn_up(h2, w_gate.astype(BF16), w_up.astype(BF16), tm=1024, tn=256)
    x2 = matmul_residual(act, w_down.astype(BF16), x1, tm=512, tn=256)
    return x2


def kernel(x, norm_attn, w_in, ck_pe, ck_w1, ck_w2, cv_pe, cv_w1, cv_w2, out_norm_a, out_norm_b, w_out,
           norm_ffn, w_gate, w_up, w_down, norm_final):
    batch, seq, d = x.shape
    depth = w_in.shape[0]
    cos_t, sin_t = _rope_tables(seq)
    xf = x.reshape(batch * seq, d)
    for l in range(depth):
        xf = _layer(xf, norm_attn[l], w_in[l], ck_pe[l], ck_w1[l], ck_w2[l], cv_pe[l], cv_w1[l], cv_w2[l],
                    out_norm_a[l], out_norm_b[l], w_out[l], norm_ffn[l], w_gate[l], w_up[l], w_down[l],
                    cos_t, sin_t, batch, seq)
    return rms(xf, norm_final, F32).reshape(batch, seq, d)
```

```python
import functools
import math

import jax
import jax.numpy as jnp
from jax import lax
from jax.experimental import pallas as pl
from jax.experimental.pallas import tpu as pltpu

F32 = jnp.float32
BF16 = jnp.bfloat16

D_MODEL = 4096
HEAD_DIM = 128
HEADS_A = 16
HEADS_B = 16
KV_GROUPS_B = 4
REP_B = HEADS_B // KV_GROUPS_B
WIDTH_A = HEADS_A * HEAD_DIM
WIDTH_B = HEADS_B * HEAD_DIM
DILATED_CONFIGS = ((128, 1), (512, 4), (2048, 16))
BLK = 128
CMP_LEN = 32
CMP_STRIDE = 16
SLC_BLOCK = 64
N_SELECT = 16
WIN = 512
FORCE_SCORE = 1e6
ROPE_THETA = 500000.0
ROPE_DIM = HEAD_DIM // 4
EPS = 1e-5
Q_SCALE = HEAD_DIM ** -0.5 * math.log2(math.e)
N_GATES = 3 * HEADS_B
D_MAIN = 3 * WIDTH_A + WIDTH_B + 6 * KV_GROUPS_B * HEAD_DIM

VMEM_LIMIT_BYTES = 56 * 1024 * 1024
LANES = 128
NEG = -1e30
SEL_BIAS = -32768.0

PROJ_TILE = 4 * HEAD_DIM
F32_TILES = tuple(range(12)) + (16, 17)
ROW_TILES = (18, 20)
COL_TILES = (12, 13, 14, 15, 19, 21)
Q_TILES = (0, 1, 2, 3, 12, 13, 14, 15)
V_TILES = (8, 9, 10, 11, 17, 19, 21)
PF_QA, PF_KA, PF_VA, PF_KC, PF_VC = 0, 16, 32, 48, 52
PR_KS, PR_KW = 0, 4
PC_QB, PC_VS, PC_VW = 0, 16, 20


def _params(*sem):
    return pltpu.CompilerParams(dimension_semantics=sem, vmem_limit_bytes=VMEM_LIMIT_BYTES)


def _dot(a, b):
    return jnp.dot(a, b, preferred_element_type=F32)


def _dot_nt(a, b):
    return lax.dot_general(a, b, (((1,), (1,)), ((), ())), preferred_element_type=F32)


def _rms(x, g):
    return x * lax.rsqrt(jnp.mean(x * x, axis=-1, keepdims=True) + EPS) * g


def _rms_gate_kernel(x_ref, g_ref, wgl_ref, h_ref, gl_ref):
    hb = _rms(x_ref[...], g_ref[...]).astype(BF16)
    h_ref[...] = hb
    gl_ref[...] = _dot(hb, wgl_ref[...])


def rms_gate(x, gain, w_gl, tm=256):
    m, d = x.shape
    ng = w_gl.shape[1]
    return pl.pallas_call(
        _rms_gate_kernel,
        grid=(m // tm,),
        in_specs=[pl.BlockSpec((tm, d), lambda i: (i, 0)),
                  pl.BlockSpec((1, d), lambda i: (0, 0)),
                  pl.BlockSpec((d, ng), lambda i: (0, 0))],
        out_specs=[pl.BlockSpec((tm, d), lambda i: (i, 0)),
                   pl.BlockSpec((tm, ng), lambda i: (i, 0))],
        out_shape=[jax.ShapeDtypeStruct((m, d), BF16), jax.ShapeDtypeStruct((m, ng), F32)],
        compiler_params=_params("parallel"),
        name="rms_gate",
    )(x, gain.reshape(1, d), w_gl)


def _rms_kernel(x_ref, g_ref, o_ref):
    o_ref[...] = _rms(x_ref[...], g_ref[...]).astype(o_ref.dtype)


def rms(x, gain, out_dtype, tm=256):
    m, d = x.shape
    return pl.pallas_call(
        _rms_kernel,
        grid=(m // tm,),
        in_specs=[pl.BlockSpec((tm, d), lambda i: (i, 0)),
                  pl.BlockSpec((1, d), lambda i: (0, 0))],
        out_specs=pl.BlockSpec((tm, d), lambda i: (i, 0)),
        out_shape=jax.ShapeDtypeStruct((m, d), out_dtype),
        compiler_params=_params("parallel"),
        name="rms",
    )(x, gain.reshape(1, d))


def _rms_pair_kernel(a_ref, b_ref, ga_ref, gb_ref, o_ref):
    wa = a_ref.shape[1]
    o_ref[:, :wa] = _rms(a_ref[...], ga_ref[...]).astype(o_ref.dtype)
    o_ref[:, wa:] = _rms(b_ref[...], gb_ref[...]).astype(o_ref.dtype)


def rms_pair(a, b, ga, gb, tm=256):
    m, wa = a.shape
    wb = b.shape[1]
    return pl.pallas_call(
        _rms_pair_kernel,
        grid=(m // tm,),
        in_specs=[pl.BlockSpec((tm, wa), lambda i: (i, 0)),
                  pl.BlockSpec((tm, wb), lambda i: (i, 0)),
                  pl.BlockSpec((1, wa), lambda i: (0, 0)),
                  pl.BlockSpec((1, wb), lambda i: (0, 0))],
        out_specs=pl.BlockSpec((tm, wa + wb), lambda i: (i, 0)),
        out_shape=jax.ShapeDtypeStruct((m, wa + wb), BF16),
        compiler_params=_params("parallel"),
        name="rms_pair",
    )(a, b, ga.reshape(1, wa), gb.reshape(1, wb))


def _rope_tables(seq):
    inv = ROPE_THETA ** (-jnp.arange(0, ROPE_DIM, 2, dtype=F32) / ROPE_DIM)
    ang = jnp.arange(seq, dtype=F32)[:, None] * inv[None, :]
    cos, sin = jnp.cos(ang), jnp.sin(ang)
    ones = jnp.ones((seq, HEAD_DIM - ROPE_DIM), F32)
    cos_t = jnp.concatenate([cos, cos, ones], axis=1)
    sin_t = jnp.concatenate([-sin, sin, 0.0 * ones], axis=1)
    return cos_t, sin_t


PROJ_ROWS = 256


def _proj_kernel(h_ref, w_ref, cos_ref, sin_ref, o_ref, *, src_tiles, transposed):
    jt = pl.program_id(1)
    is_v = functools.reduce(jnp.logical_or, [jt == n for n, t in enumerate(src_tiles) if t in V_TILES], False)
    is_q = functools.reduce(jnp.logical_or, [jt == n for n, t in enumerate(src_tiles) if t in Q_TILES], False)
    heads_per_tile = PROJ_TILE // HEAD_DIM
    tm = h_ref.shape[0]
    scale = jnp.where(is_q, Q_SCALE, 1.0).astype(F32)
    lane = lax.broadcasted_iota(jnp.int32, (1, HEAD_DIM), 1)
    first_half = lane < ROPE_DIM // 2

    for c in range(tm // PROJ_ROWS):
        rows = slice(c * PROJ_ROWS, (c + 1) * PROJ_ROWS)
        y = _dot(h_ref[rows, :], w_ref[...])
        cos_t = jnp.where(is_v, 1.0, cos_ref[rows, :]) * scale
        sin_t = jnp.where(is_v, 0.0, sin_ref[rows, :]) * scale
        for hh in range(heads_per_tile):
            yh = y[:, hh * HEAD_DIM:(hh + 1) * HEAD_DIM]
            partner = jnp.where(first_half,
                                pltpu.roll(yh, HEAD_DIM - ROPE_DIM // 2, axis=1),
                                pltpu.roll(yh, ROPE_DIM // 2, axis=1))
            val = yh * cos_t + partner * sin_t
            if transposed:
                for cc in range(PROJ_ROWS // LANES):
                    col0 = c * PROJ_ROWS + cc * LANES
                    o_ref[hh, :, col0:col0 + LANES] = val[cc * LANES:(cc + 1) * LANES, :].T.astype(o_ref.dtype)
            else:
                o_ref[hh, rows, :] = val.astype(o_ref.dtype)


def project(h, w, cos_t, sin_t, src_tiles, out_dtype, batch, seq, transposed=False, tm=1024):
    m, d = h.shape
    n_tiles = len(src_tiles)
    heads_per_tile = PROJ_TILE // HEAD_DIM
    tm = min(tm, seq)
    pos_blocks = seq // tm

    def w_map(i, j):
        col = functools.reduce(lambda acc, nt: jnp.where(j == nt[0], nt[1], acc),
                               list(enumerate(src_tiles)), 0)
        return (0, col)

    if transposed:
        out_spec = pl.BlockSpec((heads_per_tile, None, HEAD_DIM, tm),
                                lambda i, j: (j, i // pos_blocks, 0, i % pos_blocks))
        out_shape = jax.ShapeDtypeStruct((n_tiles * heads_per_tile, batch, HEAD_DIM, seq), out_dtype)
    else:
        out_spec = pl.BlockSpec((heads_per_tile, tm, HEAD_DIM), lambda i, j: (j, i, 0))
        out_shape = jax.ShapeDtypeStruct((n_tiles * heads_per_tile, m, HEAD_DIM), out_dtype)

    return pl.pallas_call(
        functools.partial(_proj_kernel, src_tiles=src_tiles, transposed=transposed),
        grid=(m // tm, n_tiles),
        in_specs=[pl.BlockSpec((tm, d), lambda i, j: (i, 0)),
                  pl.BlockSpec((d, PROJ_TILE), w_map),
                  pl.BlockSpec((tm, HEAD_DIM), lambda i, j: (i % pos_blocks, 0)),
                  pl.BlockSpec((tm, HEAD_DIM), lambda i, j: (i % pos_blocks, 0))],
        out_specs=out_spec,
        out_shape=out_shape,
        compiler_params=_params("parallel", "arbitrary"),
        name="project_t" if transposed else "project",
    )(h, w, cos_t, sin_t)


DIL_UNROLL = 4


def _dilated_kernel(q_ref, k_ref, v_ref, o_ref, acc_ref, m_ref, l_ref, *, seq):
    for ci, (window, dil) in enumerate(DILATED_CONFIGS):
        band = window // dil
        n_prev = -(-band // BLK)
        nk = (n_prev + 1) * BLK
        span = BLK * dil
        nb = seq // span

        def body(step, carry, ci=ci, dil=dil, band=band, n_prev=n_prev, nk=nk, span=span, nb=nb):
            tiles = []
            for u in range(DIL_UNROLL):
                idx = step * DIL_UNROLL + u
                r = idx // nb
                i = idx - r * nb
                kb = jnp.maximum(i - n_prev, 0)
                q0 = r + i * span
                k0 = r + kb * span
                if dil == 1:
                    q_rows = pl.ds(pl.multiple_of(q0, BLK), BLK)
                    k_rows = pl.ds(pl.multiple_of(k0, BLK), nk)
                else:
                    q_rows = pl.ds(q0, BLK, stride=dil)
                    k_rows = pl.ds(k0, nk, stride=dil)
                q = q_ref[q_rows, :].astype(BF16)
                k = k_ref[k_rows, :].astype(BF16)
                v = v_ref[k_rows, :].astype(BF16)
                m_old = None if ci == 0 else m_ref[q_rows, :]
                tiles.append((i, kb, q_rows, q, k, v, m_old))

            results = []
            for (i, kb, q_rows, q, k, v, m_old) in tiles:
                s = _dot_nt(q, k)
                qn = i * BLK + lax.broadcasted_iota(jnp.int32, (BLK, 1), 0)
                kn = kb * BLK + lax.broadcasted_iota(jnp.int32, (1, nk), 1)
                dist = qn - kn
                s = jnp.where((dist >= 0) & (dist <= band), s, NEG)
                m_tile = jnp.max(s, axis=1, keepdims=True)
                if ci == 0:
                    m_new = jnp.broadcast_to(m_tile, (BLK, LANES))
                else:
                    m_new = jnp.maximum(m_old, m_tile)
                p = jnp.exp2(s - jnp.concatenate([m_new] * (nk // LANES), axis=1))
                l_tile = jnp.sum(p, axis=1, keepdims=True)
                pv = _dot(p.astype(BF16), v)
                results.append((q_rows, m_old, m_new, l_tile, pv))

            for (q_rows, m_old, m_new, l_tile, pv) in results:
                if ci == 0:
                    l_ref[q_rows, :] = jnp.broadcast_to(l_tile, (BLK, LANES))
                    acc_ref[q_rows, :] = pv
                else:
                    alpha = jnp.exp2(m_old - m_new)
                    l_ref[q_rows, :] = alpha * l_ref[q_rows, :] + l_tile
                    acc_ref[q_rows, :] = alpha * acc_ref[q_rows, :] + pv
                m_ref[q_rows, :] = m_new
            return carry

        lax.fori_loop(0, seq // (BLK * DIL_UNROLL), body, 0)
    o_ref[...] = acc_ref[...] / l_ref[...]


def dilated_attention(pf, batch, seq):
    assert seq % (BLK * DILATED_CONFIGS[-1][1]) == 0 and seq // DILATED_CONFIGS[-1][1] >= 2 * BLK
    assert (seq // BLK) % DIL_UNROLL == 0
    pf4 = pf.reshape(pf.shape[0], batch, seq, HEAD_DIM)

    def spec(off):
        return pl.BlockSpec((None, None, seq, HEAD_DIM), lambda b, h: (off + h, b, 0, 0))

    return pl.pallas_call(
        functools.partial(_dilated_kernel, seq=seq),
        grid=(batch, HEADS_A),
        in_specs=[spec(PF_QA), spec(PF_KA), spec(PF_VA)],
        out_specs=pl.BlockSpec((seq, HEAD_DIM), lambda b, h: (b, h)),
        out_shape=jax.ShapeDtypeStruct((batch * seq, WIDTH_A), F32),
        scratch_shapes=[pltpu.VMEM((seq, HEAD_DIM), F32),
                        pltpu.VMEM((seq, LANES), F32),
                        pltpu.VMEM((seq, LANES), F32)],
        compiler_params=_params("parallel", "parallel"),
        name="dilated_attention",
    )(pf4, pf4, pf4)


def _compress_kernel(t_ref, pe_ref, w1_ref, w2_ref, o_ref, ot_ref, *, seq):
    n_chunks = seq // CMP_STRIDE
    first = jnp.zeros((n_chunks, HEAD_DIM), F32)
    second = jnp.zeros((n_chunks, HEAD_DIM), F32)
    for i in range(CMP_STRIDE):
        ti = t_ref[pl.ds(i, n_chunks, stride=CMP_STRIDE), :]
        first += _dot((ti + pe_ref[pl.ds(i, 1), :]).astype(BF16), w1_ref[i])
        second += _dot((ti + pe_ref[pl.ds(CMP_STRIDE + i, 1), :]).astype(BF16), w1_ref[CMP_STRIDE + i])
    pre = first + pltpu.roll(second, n_chunks - 1, axis=0)
    out = _dot(jax.nn.gelu(pre).astype(BF16), w2_ref[...])
    o_ref[...] = out.astype(o_ref.dtype)
    ot_ref[...] = out.T.astype(ot_ref.dtype)


def compress(pf, pe, w1, w2, batch, seq):
    assert CMP_LEN == 2 * CMP_STRIDE
    n_chunks = seq // CMP_STRIDE
    pf4 = pf.reshape(pf.shape[0], batch, seq, HEAD_DIM)
    bg = batch * KV_GROUPS_B
    return pl.pallas_call(
        functools.partial(_compress_kernel, seq=seq),
        grid=(2, batch, KV_GROUPS_B),
        in_specs=[pl.BlockSpec((None, None, seq, HEAD_DIM), lambda kv, b, g: (PF_KC + kv * KV_GROUPS_B + g, b, 0, 0)),
                  pl.BlockSpec((None, CMP_LEN, HEAD_DIM), lambda kv, b, g: (kv, 0, 0)),
                  pl.BlockSpec((None, CMP_LEN, HEAD_DIM, HEAD_DIM), lambda kv, b, g: (kv, 0, 0, 0)),
                  pl.BlockSpec((None, HEAD_DIM, HEAD_DIM), lambda kv, b, g: (kv, 0, 0))],
        out_specs=[pl.BlockSpec((None, None, n_chunks, HEAD_DIM), lambda kv, b, g: (kv, b * KV_GROUPS_B + g, 0, 0)),
                   pl.BlockSpec((None, None, HEAD_DIM, n_chunks), lambda kv, b, g: (kv, b * KV_GROUPS_B + g, 0, 0))],
        out_shape=[jax.ShapeDtypeStruct((2, bg, n_chunks, HEAD_DIM), BF16),
                   jax.ShapeDtypeStruct((2, bg, HEAD_DIM, n_chunks), BF16)],
        compiler_params=_params("parallel", "parallel", "parallel"),
        name="compress",
    )(pf4, pe, w1, w2)


NSA_TQ = 128
NSA_TK = 256


def _select_blocks(score):
    n_s, tq = score.shape
    groups = n_s // 8
    rows8 = [score[8 * v:8 * v + 8, :] for v in range(groups)]
    rank8 = [jnp.zeros((8, tq), F32) for _ in range(groups)]
    sub = lax.broadcasted_iota(jnp.int32, (8, 1), 0)
    for jp in range(n_s):
        vp, sp = divmod(jp, 8)
        row = jnp.broadcast_to(rows8[vp][sp:sp + 1, :], (8, tq))
        for v in range(groups):
            if v > vp:
                beats = jnp.where(row >= rows8[v], 1.0, 0.0)
            elif v < vp:
                beats = jnp.where(row > rows8[v], 1.0, 0.0)
            else:
                beats = jnp.where(sub > sp, jnp.where(row >= rows8[v], 1.0, 0.0),
                                  jnp.where(row > rows8[v], 1.0, 0.0))
            rank8[v] = rank8[v] + beats
    rank = jnp.concatenate(rank8, axis=0)
    return (rank < N_SELECT) & (score > -jnp.inf)


def _nsa_kernel(q_ref, kc_ref, vct_ref, ks_ref, vst_ref, kw_ref, vwt_ref, gl_ref, o_ref, *, seq):
    tq, tk = NSA_TQ, NSA_TK
    cols = REP_B * tq
    n_cp = seq // CMP_STRIDE
    n_s = seq // SLC_BLOCK
    qi = pl.program_id(2)
    t0 = qi * tq
    q_t = jnp.concatenate([q_ref[r] for r in range(REP_B)], axis=1)
    lane_id = lax.broadcasted_iota(jnp.int32, (1, cols), 1)
    qpos = t0 + (lane_id & (tq - 1))

    sc = _dot(kc_ref[...], q_t)
    csub = lax.broadcasted_iota(jnp.int32, (n_cp, 1), 0)
    c_ok = (csub * CMP_STRIDE + (CMP_LEN - 1) <= qpos) & (csub < n_cp - 1)
    sc = jnp.where(c_ok, sc, NEG)
    m_c = jnp.max(sc, axis=0, keepdims=True)
    e = jnp.where(c_ok, jnp.exp2(sc - m_c), 0.0)
    p_cmp = e * (1.0 / jnp.maximum(jnp.sum(e, axis=0, keepdims=True), 1e-30))
    o_cmp = _dot(vct_ref[...], p_cmp.astype(BF16))

    p_sum = p_cmp[:, 0:tq]
    for r in range(1, REP_B):
        p_sum = p_sum + p_cmp[:, r * tq:(r + 1) * tq]
    jj = lax.broadcasted_iota(jnp.int32, (n_s, 1), 0)
    cidx = lax.broadcasted_iota(jnp.int32, (1, n_cp), 1)
    ratio = SLC_BLOCK // CMP_STRIDE
    c_first = ratio * jj - (CMP_LEN // CMP_STRIDE - 1)
    hits = jnp.where((cidx >= c_first) & (cidx < ratio * (jj + 1)), 1.0, 0.0).astype(BF16)
    p_hi = p_sum.astype(BF16)
    p_lo = (p_sum - p_hi.astype(F32)).astype(BF16)
    imp = _dot(hits, p_hi) + _dot(hits, p_lo)
    tpos = t0 + lax.broadcasted_iota(jnp.int32, (1, tq), 1)
    qblk = tpos // SLC_BLOCK
    forced = (jj == 0) | (jj == qblk) | (jj == qblk - 1)
    valid = jj * SLC_BLOCK <= tpos
    score = jnp.where(forced, FORCE_SCORE, jnp.where(valid, imp, -jnp.inf))
    sel = _select_blocks(score)
    bias = jnp.where(sel, 0.0, SEL_BIAS)
    if n_s < LANES:
        bias = jnp.concatenate([bias, jnp.full((LANES - n_s, tq), SEL_BIAS, F32)], axis=0)
    bias = bias.astype(BF16)
    q_aug = jnp.concatenate([q_t, jnp.concatenate([bias] * REP_B, axis=1)], axis=0)

    lane_blk = lax.broadcasted_iota(jnp.int32, (1, LANES), 1)

    def slc_step(kt, carry, causal):
        m_i, l_i, acc = carry
        k0 = pl.multiple_of(kt * tk, tk)
        tok = k0 + lax.broadcasted_iota(jnp.int32, (tk, 1), 0)
        onehot = jnp.where(tok // SLC_BLOCK == lane_blk, 1.0, 0.0).astype(BF16)
        s = _dot(jnp.concatenate([ks_ref[pl.ds(k0, tk), :], onehot], axis=1), q_aug)
        if causal:
            s = jnp.where(tok <= qpos, s, NEG)
        m_new = jnp.maximum(m_i, jnp.max(s, axis=0, keepdims=True))
        alpha = jnp.exp2(m_i - m_new)
        p = jnp.exp2(s - m_new)
        l_new = alpha * l_i + jnp.sum(p, axis=0, keepdims=True)
        acc_new = alpha * acc + _dot(vst_ref[:, pl.ds(k0, tk)], p.astype(BF16))
        return m_new, l_new, acc_new

    last = (t0 + tq - 1) // tk
    carry = (jnp.full((1, cols), NEG, F32), jnp.zeros((1, cols), F32), jnp.zeros((HEAD_DIM, cols), F32))
    carry = lax.fori_loop(0, last, functools.partial(slc_step, causal=False), carry)
    _, l_s, acc_s = slc_step(last, carry, causal=True)
    o_slc = acc_s * (1.0 / l_s)

    ws = WIN + tq
    w0 = pl.multiple_of(jnp.maximum(t0 - WIN, 0), tq)
    kpos_w = w0 + lax.broadcasted_iota(jnp.int32, (ws, 1), 0)
    dist = qpos - kpos_w
    s_w = jnp.where((dist >= 0) & (dist <= WIN - 1), _dot(kw_ref[pl.ds(w0, ws), :], q_t), NEG)
    p_w = jnp.exp2(s_w - jnp.max(s_w, axis=0, keepdims=True))
    o_win = _dot(vwt_ref[:, pl.ds(w0, ws)], p_w.astype(BF16)) * (1.0 / jnp.sum(p_w, axis=0, keepdims=True))

    gates = jax.nn.sigmoid(gl_ref[...].T)
    for r in range(REP_B):
        sl = slice(r * tq, (r + 1) * tq)
        merged = (gates[3 * r:3 * r + 1, :] * o_cmp[:, sl]
                  + gates[3 * r + 1:3 * r + 2, :] * o_slc[:, sl]
                  + gates[3 * r + 2:3 * r + 3, :] * o_win[:, sl])
        o_ref[:, r * HEAD_DIM:(r + 1) * HEAD_DIM] = merged.T


def nsa_attention(pr, pc, kvc, kvct, gl, batch, seq):
    tq = NSA_TQ
    assert seq % NSA_TK == 0 and seq >= WIN + tq and seq // SLC_BLOCK <= LANES and (seq // SLC_BLOCK) % 8 == 0
    n_cp = seq // CMP_STRIDE
    nq = seq // tq
    pr4 = pr.reshape(pr.shape[0], batch, seq, HEAD_DIM)

    def k_spec(off):
        return pl.BlockSpec((None, None, seq, HEAD_DIM), lambda b, g, i: (off + g, b, 0, 0))

    def vt_spec(off):
        return pl.BlockSpec((None, None, HEAD_DIM, seq), lambda b, g, i: (off + g, b, 0, 0))

    return pl.pallas_call(
        functools.partial(_nsa_kernel, seq=seq),
        grid=(batch, KV_GROUPS_B, nq),
        in_specs=[pl.BlockSpec((REP_B, None, HEAD_DIM, tq), lambda b, g, i: (PC_QB // REP_B + g, b, 0, i)),
                  pl.BlockSpec((None, None, n_cp, HEAD_DIM), lambda b, g, i: (0, b * KV_GROUPS_B + g, 0, 0)),
                  pl.BlockSpec((None, None, HEAD_DIM, n_cp), lambda b, g, i: (1, b * KV_GROUPS_B + g, 0, 0)),
                  k_spec(PR_KS), vt_spec(PC_VS), k_spec(PR_KW), vt_spec(PC_VW),
                  pl.BlockSpec((tq, LANES), lambda b, g, i: (b * nq + i, g))],
        out_specs=pl.BlockSpec((tq, REP_B * HEAD_DIM), lambda b, g, i: (b * nq + i, g)),
        out_shape=jax.ShapeDtypeStruct((batch * seq, WIDTH_B), F32),
        compiler_params=_params("parallel", "parallel", "arbitrary"),
        name="nsa_attention",
    )(pc, kvc, kvct, pr4, pc, pr4, pc, gl)


def _mm_res_kernel(a_ref, w_ref, r_ref, o_ref):
    o_ref[...] = _dot(a_ref[...], w_ref[...]) + r_ref[...]


def matmul_residual(a, w, res, tm, tn):
    m, k = a.shape
    n = w.shape[1]
    return pl.pallas_call(
        _mm_res_kernel,
        grid=(m // tm, n // tn),
        in_specs=[pl.BlockSpec((tm, k), lambda i, j: (i, 0)),
                  pl.BlockSpec((k, tn), lambda i, j: (0, j)),
                  pl.BlockSpec((tm, tn), lambda i, j: (i, j))],
        out_specs=pl.BlockSpec((tm, tn), lambda i, j: (i, j)),
        out_shape=jax.ShapeDtypeStruct((m, n), F32),
        compiler_params=_params("parallel", "arbitrary"),
        name="matmul_residual",
    )(a, w, res)


def _ffn_up_kernel(h_ref, wg_ref, wu_ref, o_ref):
    h = h_ref[...]
    g = _dot(h, wg_ref[...])
    u = _dot(h, wu_ref[...])
    o_ref[...] = (g * jax.nn.sigmoid(g) * u).astype(o_ref.dtype)


def ffn_up(h, wg, wu, tm, tn):
    m, k = h.shape
    n = wg.shape[1]
    return pl.pallas_call(
        _ffn_up_kernel,
        grid=(m // tm, n // tn),
        in_specs=[pl.BlockSpec((tm, k), lambda i, j: (i, 0)),
                  pl.BlockSpec((k, tn), lambda i, j: (0, j)),
                  pl.BlockSpec((k, tn), lambda i, j: (0, j))],
        out_specs=pl.BlockSpec((tm, tn), lambda i, j: (i, j)),
        out_shape=jax.ShapeDtypeStruct((m, n), BF16),
        compiler_params=_params("parallel", "arbitrary"),
        name="ffn_up",
    )(h, wg, wu)


def _layer(x, norm_attn, w_in, ck_pe, ck_w1, ck_w2, cv_pe, cv_w1, cv_w2,
           out_norm_a, out_norm_b, w_out, norm_ffn, w_gate, w_up, w_down, cos_t, sin_t, batch, seq):
    w_main = w_in[:, :D_MAIN].astype(BF16)
    w_gl = w_in[:, D_MAIN:].reshape(D_MODEL, KV_GROUPS_B, REP_B * 3)
    w_gl = jnp.pad(w_gl, ((0, 0), (0, 0), (0, LANES - REP_B * 3))).reshape(D_MODEL, KV_GROUPS_B * LANES).astype(BF16)
    pe = jnp.stack([ck_pe, cv_pe])
    w1 = jnp.stack([ck_w1, cv_w1]).reshape(2, CMP_LEN, HEAD_DIM, HEAD_DIM).astype(BF16)
    w2 = jnp.stack([ck_w2, cv_w2]).astype(BF16)

    h, gl = rms_gate(x, norm_attn, w_gl)
    pf = project(h, w_main, cos_t, sin_t, F32_TILES, F32, batch, seq)
    pr = project(h, w_main, cos_t, sin_t, ROW_TILES, BF16, batch, seq)
    pc = project(h, w_main, cos_t, sin_t, COL_TILES, BF16, batch, seq, transposed=True)
    o_a = dilated_attention(pf, batch, seq)
    kvc, kvct = compress(pf, pe, w1, w2, batch, seq)
    o_b = nsa_attention(pr, pc, kvc, kvct, gl, batch, seq)
    mixed = rms_pair(o_a, o_b, out_norm_a, out_norm_b)
    x1 = matmul_residual(mixed, w_out.astype(BF16), x, tm=1024, tn=512)
    h2 = rms(x1, norm_ffn, BF16)
    act = ffn_up(h2, w_gate.astype(BF16), w_up.astype(BF16), tm=1024, tn=256)
    x2 = matmul_residual(act, w_down.astype(BF16), x1, tm=512, tn=256)
    return x2


def kernel(x, norm_attn, w_in, ck_pe, ck_w1, ck_w2, cv_pe, cv_w1, cv_w2, out_norm_a, out_norm_b, w_out,
           norm_ffn, w_gate, w_up, w_down, norm_final):
    batch, seq, d = x.shape
    depth = w_in.shape[0]
    cos_t, sin_t = _rope_tables(seq)
    xf = x.reshape(batch * seq, d)
    for l in range(depth):
        xf = _layer(xf, norm_attn[l], w_in[l], ck_pe[l], ck_w1[l], ck_w2[l], cv_pe[l], cv_w1[l], cv_w2[l],
                    out_norm_a[l], out_norm_b[l], w_out[l], norm_ffn[l], w_gate[l], w_up[l], w_down[l],
                    cos_t, sin_t, batch, seq)
    return rms(xf, norm_final, F32).reshape(batch, seq, d)
```

```python
import functools
import math

import jax
import jax.numpy as jnp
from jax import lax
from jax.experimental import pallas as pl
from jax.experimental.pallas import tpu as pltpu

F32 = jnp.float32
BF16 = jnp.bfloat16

D_MODEL = 4096
HEAD_DIM = 128
HEADS_A = 16
HEADS_B = 16
KV_GROUPS_B = 4
REP_B = HEADS_B // KV_GROUPS_B
WIDTH_A = HEADS_A * HEAD_DIM
WIDTH_B = HEADS_B * HEAD_DIM
DILATED_CONFIGS = ((128, 1), (512, 4), (2048, 16))
BLK = 128
CMP_LEN = 32
CMP_STRIDE = 16
SLC_BLOCK = 64
N_SELECT = 16
WIN = 512
FORCE_SCORE = 1e6
ROPE_THETA = 500000.0
ROPE_DIM = HEAD_DIM // 4
EPS = 1e-5
Q_SCALE = HEAD_DIM ** -0.5 * math.log2(math.e)
N_GATES = 3 * HEADS_B
D_MAIN = 3 * WIDTH_A + WIDTH_B + 6 * KV_GROUPS_B * HEAD_DIM

VMEM_LIMIT_BYTES = 56 * 1024 * 1024
LANES = 128
NEG = -1e30
SEL_BIAS = -32768.0

PROJ_TILE = 4 * HEAD_DIM
F32_TILES = tuple(range(12)) + (16, 17)
ROW_TILES = (18, 20)
COL_TILES = (12, 13, 14, 15, 19, 21)
Q_TILES = (0, 1, 2, 3, 12, 13, 14, 15)
V_TILES = (8, 9, 10, 11, 17, 19, 21)
PF_QA, PF_KA, PF_VA, PF_KC, PF_VC = 0, 16, 32, 48, 52
PR_KS, PR_KW = 0, 4
PC_QB, PC_VS, PC_VW = 0, 16, 20


def _params(*sem):
    return pltpu.CompilerParams(dimension_semantics=sem, vmem_limit_bytes=VMEM_LIMIT_BYTES)


def _dot(a, b):
    return jnp.dot(a, b, preferred_element_type=F32)


def _dot_nt(a, b):
    return lax.dot_general(a, b, (((1,), (1,)), ((), ())), preferred_element_type=F32)


def _rms(x, g):
    return x * lax.rsqrt(jnp.mean(x * x, axis=-1, keepdims=True) + EPS) * g


def _rms_gate_kernel(x_ref, g_ref, wgl_ref, h_ref, gl_ref):
    hb = _rms(x_ref[...], g_ref[...]).astype(BF16)
    h_ref[...] = hb
    gl_ref[...] = _dot(hb, wgl_ref[...])


def rms_gate(x, gain, w_gl, tm=256):
    m, d = x.shape
    ng = w_gl.shape[1]
    return pl.pallas_call(
        _rms_gate_kernel,
        grid=(m // tm,),
        in_specs=[pl.BlockSpec((tm, d), lambda i: (i, 0)),
                  pl.BlockSpec((1, d), lambda i: (0, 0)),
                  pl.BlockSpec((d, ng), lambda i: (0, 0))],
        out_specs=[pl.BlockSpec((tm, d), lambda i: (i, 0)),
                   pl.BlockSpec((tm, ng), lambda i: (i, 0))],
        out_shape=[jax.ShapeDtypeStruct((m, d), BF16), jax.ShapeDtypeStruct((m, ng), F32)],
        compiler_params=_params("parallel"),
        name="rms_gate",
    )(x, gain.reshape(1, d), w_gl)


def _rms_kernel(x_ref, g_ref, o_ref):
    o_ref[...] = _rms(x_ref[...], g_ref[...]).astype(o_ref.dtype)


def rms(x, gain, out_dtype, tm=256):
    m, d = x.shape
    return pl.pallas_call(
        _rms_kernel,
        grid=(m // tm,),
        in_specs=[pl.BlockSpec((tm, d), lambda i: (i, 0)),
                  pl.BlockSpec((1, d), lambda i: (0, 0))],
        out_specs=pl.BlockSpec((tm, d), lambda i: (i, 0)),
        out_shape=jax.ShapeDtypeStruct((m, d), out_dtype),
        compiler_params=_params("parallel"),
        name="rms",
    )(x, gain.reshape(1, d))


def _rms_pair_kernel(a_ref, b_ref, ga_ref, gb_ref, o_ref):
    n_heads = a_ref.shape[0]
    wa = n_heads * HEAD_DIM
    ssq = jnp.sum(a_ref[0] * a_ref[0], axis=-1, keepdims=True)
    for hh in range(1, n_heads):
        ssq = ssq + jnp.sum(a_ref[hh] * a_ref[hh], axis=-1, keepdims=True)
    inv = lax.rsqrt(ssq * (1.0 / wa) + EPS)
    for hh in range(n_heads):
        cols = slice(hh * HEAD_DIM, (hh + 1) * HEAD_DIM)
        o_ref[:, cols] = (a_ref[hh] * inv * ga_ref[:, cols]).astype(o_ref.dtype)
    o_ref[:, wa:] = _rms(b_ref[...], gb_ref[...]).astype(o_ref.dtype)


def rms_pair(a, b, ga, gb, tm=256):
    n_heads, m, _ = a.shape
    wa = n_heads * HEAD_DIM
    wb = b.shape[1]
    return pl.pallas_call(
        _rms_pair_kernel,
        grid=(m // tm,),
        in_specs=[pl.BlockSpec((n_heads, tm, HEAD_DIM), lambda i: (0, i, 0)),
                  pl.BlockSpec((tm, wb), lambda i: (i, 0)),
                  pl.BlockSpec((1, wa), lambda i: (0, 0)),
                  pl.BlockSpec((1, wb), lambda i: (0, 0))],
        out_specs=pl.BlockSpec((tm, wa + wb), lambda i: (i, 0)),
        out_shape=jax.ShapeDtypeStruct((m, wa + wb), BF16),
        compiler_params=_params("parallel"),
        name="rms_pair",
    )(a, b, ga.reshape(1, wa), gb.reshape(1, wb))


def _rope_tables(seq):
    inv = ROPE_THETA ** (-jnp.arange(0, ROPE_DIM, 2, dtype=F32) / ROPE_DIM)
    ang = jnp.arange(seq, dtype=F32)[:, None] * inv[None, :]
    cos, sin = jnp.cos(ang), jnp.sin(ang)
    ones = jnp.ones((seq, HEAD_DIM - ROPE_DIM), F32)
    cos_t = jnp.concatenate([cos, cos, ones], axis=1)
    sin_t = jnp.concatenate([-sin, sin, 0.0 * ones], axis=1)
    return cos_t, sin_t


PROJ_ROWS = 256


def _proj_kernel(h_ref, w_ref, cos_ref, sin_ref, o_ref, *, src_tiles, transposed):
    jt = pl.program_id(1)
    is_v = functools.reduce(jnp.logical_or, [jt == n for n, t in enumerate(src_tiles) if t in V_TILES], False)
    is_q = functools.reduce(jnp.logical_or, [jt == n for n, t in enumerate(src_tiles) if t in Q_TILES], False)
    heads_per_tile = PROJ_TILE // HEAD_DIM
    tm = h_ref.shape[0]
    scale = jnp.where(is_q, Q_SCALE, 1.0).astype(F32)
    lane = lax.broadcasted_iota(jnp.int32, (1, HEAD_DIM), 1)
    first_half = lane < ROPE_DIM // 2

    for c in range(tm // PROJ_ROWS):
        rows = slice(c * PROJ_ROWS, (c + 1) * PROJ_ROWS)
        y = _dot(h_ref[rows, :], w_ref[...])
        cos_t = jnp.where(is_v, 1.0, cos_ref[rows, :]) * scale
        sin_t = jnp.where(is_v, 0.0, sin_ref[rows, :]) * scale
        for hh in range(heads_per_tile):
            yh = y[:, hh * HEAD_DIM:(hh + 1) * HEAD_DIM]
            partner = jnp.where(first_half,
                                pltpu.roll(yh, HEAD_DIM - ROPE_DIM // 2, axis=1),
                                pltpu.roll(yh, ROPE_DIM // 2, axis=1))
            val = yh * cos_t + partner * sin_t
            if transposed:
                for cc in range(PROJ_ROWS // LANES):
                    col0 = c * PROJ_ROWS + cc * LANES
                    o_ref[hh, :, col0:col0 + LANES] = val[cc * LANES:(cc + 1) * LANES, :].T.astype(o_ref.dtype)
            else:
                o_ref[hh, rows, :] = val.astype(o_ref.dtype)


def project(h, w, cos_t, sin_t, src_tiles, out_dtype, batch, seq, transposed=False, tm=1024):
    m, d = h.shape
    n_tiles = len(src_tiles)
    heads_per_tile = PROJ_TILE // HEAD_DIM
    tm = min(tm, seq)
    pos_blocks = seq // tm

    def w_map(i, j):
        col = functools.reduce(lambda acc, nt: jnp.where(j == nt[0], nt[1], acc),
                               list(enumerate(src_tiles)), 0)
        return (0, col)

    if transposed:
        out_spec = pl.BlockSpec((heads_per_tile, None, HEAD_DIM, tm),
                                lambda i, j: (j, i // pos_blocks, 0, i % pos_blocks))
        out_shape = jax.ShapeDtypeStruct((n_tiles * heads_per_tile, batch, HEAD_DIM, seq), out_dtype)
    else:
        out_spec = pl.BlockSpec((heads_per_tile, tm, HEAD_DIM), lambda i, j: (j, i, 0))
        out_shape = jax.ShapeDtypeStruct((n_tiles * heads_per_tile, m, HEAD_DIM), out_dtype)

    return pl.pallas_call(
        functools.partial(_proj_kernel, src_tiles=src_tiles, transposed=transposed),
        grid=(m // tm, n_tiles),
        in_specs=[pl.BlockSpec((tm, d), lambda i, j: (i, 0)),
                  pl.BlockSpec((d, PROJ_TILE), w_map),
                  pl.BlockSpec((tm, HEAD_DIM), lambda i, j: (i % pos_blocks, 0)),
                  pl.BlockSpec((tm, HEAD_DIM), lambda i, j: (i % pos_blocks, 0))],
        out_specs=out_spec,
        out_shape=out_shape,
        compiler_params=_params("parallel", "arbitrary"),
        name="project_t" if transposed else "project",
    )(h, w, cos_t, sin_t)


DIL_UNROLL = 8
DIL_RES = DILATED_CONFIGS[-1][1]


def _dilated_kernel(q_ref, k_ref, v_ref, o_ref, acc_ref, m_ref, l_ref, bias_ref, *, seq):
    for ci, (window, dil) in enumerate(DILATED_CONFIGS):
        band = window // dil
        n_pieces = DIL_RES // dil
        pr = BLK // n_pieces
        kr = 2 * pr
        tiles_per_res = seq // (BLK * dil)
        assert band <= BLK and pr % 8 == 0 and pr & (pr - 1) == 0

        q_row = lax.broadcasted_iota(jnp.int32, (BLK, 1), 0)
        k_row = lax.broadcasted_iota(jnp.int32, (1, 2 * BLK), 1)
        q_sub = n_pieces * (q_row & (pr - 1)) + (q_row >> (pr.bit_length() - 1))
        k_sub = n_pieces * (k_row & (kr - 1)) + (k_row >> (kr.bit_length() - 1))
        for lead in range(2):
            dist = (q_sub + lead * BLK) - k_sub
            bias_ref[2 * ci + lead] = jnp.where((dist >= 0) & (dist <= band), 0.0, NEG)

        def body(step, carry, ci=ci, dil=dil, band=band, n_pieces=n_pieces, pr=pr, kr=kr,
                 tiles_per_res=tiles_per_res):
            def pieces(ref, row0, nrows, r):
                return [ref[pl.ds(row0, nrows), pl.ds(pl.multiple_of((r + dil * a) * HEAD_DIM, HEAD_DIM), HEAD_DIM)]
                        for a in range(n_pieces)]

            tiles = []
            for u in range(DIL_UNROLL):
                idx = step * DIL_UNROLL + u
                r = idx // tiles_per_res
                i = idx - r * tiles_per_res
                kb = jnp.maximum(i - 1, 0)
                q_row0 = pl.multiple_of(i * pr, pr)
                k_row0 = pl.multiple_of(kb * pr, pr)
                m_old = None if ci == 0 else jnp.concatenate(pieces(m_ref, q_row0, pr, r), axis=0)
                tiles.append((r, i, kb, q_row0, k_row0, m_old))

            results = []
            for (r, i, kb, q_row0, k_row0, m_old) in tiles:
                q = jnp.concatenate(pieces(q_ref, q_row0, pr, r), axis=0).astype(BF16)
                k = jnp.concatenate(pieces(k_ref, k_row0, kr, r), axis=0).astype(BF16)
                v = jnp.concatenate(pieces(v_ref, k_row0, kr, r), axis=0).astype(BF16)
                s = _dot_nt(q, k)
                s = s + bias_ref[2 * ci + i - kb]
                m_tile = jnp.max(s, axis=1, keepdims=True)
                if ci == 0:
                    m_new = jnp.broadcast_to(m_tile, (BLK, LANES))
                else:
                    m_new = jnp.maximum(m_old, m_tile)
                p = jnp.exp2(s - jnp.concatenate([m_new, m_new], axis=1))
                l_tile = jnp.sum(p, axis=1, keepdims=True)
                pv = _dot(p.astype(BF16), v)
                results.append((r, q_row0, m_old, m_new, l_tile, pv))

            for (r, q_row0, m_old, m_new, l_tile, pv) in results:
                if ci == 0:
                    l_new = jnp.broadcast_to(l_tile, (BLK, LANES))
                    acc_new = pv
                else:
                    alpha = jnp.exp2(m_old - m_new)
                    l_new = alpha * jnp.concatenate(pieces(l_ref, q_row0, pr, r), axis=0) + l_tile
                    acc_new = alpha * jnp.concatenate(pieces(acc_ref, q_row0, pr, r), axis=0) + pv
                for a in range(n_pieces):
                    col = pl.ds(pl.multiple_of((r + dil * a) * HEAD_DIM, HEAD_DIM), HEAD_DIM)
                    rows = slice(a * pr, (a + 1) * pr)
                    m_ref[pl.ds(q_row0, pr), col] = m_new[rows]
                    l_ref[pl.ds(q_row0, pr), col] = l_new[rows]
                    acc_ref[pl.ds(q_row0, pr), col] = acc_new[rows]
            return carry

        lax.fori_loop(0, seq // (BLK * DIL_UNROLL), body, 0)
    o_ref[...] = acc_ref[...] / l_ref[...]


def dilated_attention(pf, batch, seq):
    assert all(DIL_RES % d == 0 for _, d in DILATED_CONFIGS)
    assert seq % (BLK * DIL_RES) == 0 and seq // DIL_RES >= 2 * BLK and (seq // BLK) % DIL_UNROLL == 0
    rows = seq // DIL_RES
    width = DIL_RES * HEAD_DIM
    view = pf.reshape(pf.shape[0], batch, rows, width)

    def spec(off):
        return pl.BlockSpec((None, None, rows, width), lambda b, h: (off + h, b, 0, 0))

    out = pl.pallas_call(
        functools.partial(_dilated_kernel, seq=seq),
        grid=(batch, HEADS_A),
        in_specs=[spec(PF_QA), spec(PF_KA), spec(PF_VA)],
        out_specs=spec(0),
        out_shape=jax.ShapeDtypeStruct((HEADS_A, batch, rows, width), F32),
        scratch_shapes=[pltpu.VMEM((rows, width), F32)] * 3
        + [pltpu.VMEM((2 * len(DILATED_CONFIGS), BLK, 2 * BLK), F32)],
        compiler_params=_params("parallel", "parallel"),
        name="dilated_attention",
    )(view, view, view)
    return out.reshape(HEADS_A, batch * seq, HEAD_DIM)


def _compress_kernel(t_ref, pe_ref, w1_ref, w2_ref, o_ref, ot_ref, *, seq):
    n_chunks = seq // CMP_STRIDE
    first = jnp.zeros((n_chunks, HEAD_DIM), F32)
    second = jnp.zeros((n_chunks, HEAD_DIM), F32)
    for i in range(CMP_STRIDE):
        ti = t_ref[pl.ds(i, n_chunks, stride=CMP_STRIDE), :]
        first += _dot((ti + pe_ref[pl.ds(i, 1), :]).astype(BF16), w1_ref[i])
        second += _dot((ti + pe_ref[pl.ds(CMP_STRIDE + i, 1), :]).astype(BF16), w1_ref[CMP_STRIDE + i])
    pre = first + pltpu.roll(second, n_chunks - 1, axis=0)
    out = _dot(jax.nn.gelu(pre).astype(BF16), w2_ref[...])
    o_ref[...] = out.astype(o_ref.dtype)
    ot_ref[...] = out.T.astype(ot_ref.dtype)


def compress(pf, pe, w1, w2, batch, seq):
    assert CMP_LEN == 2 * CMP_STRIDE
    n_chunks = seq // CMP_STRIDE
    pf4 = pf.reshape(pf.shape[0], batch, seq, HEAD_DIM)
    bg = batch * KV_GROUPS_B
    return pl.pallas_call(
        functools.partial(_compress_kernel, seq=seq),
        grid=(2, batch, KV_GROUPS_B),
        in_specs=[pl.BlockSpec((None, None, seq, HEAD_DIM), lambda kv, b, g: (PF_KC + kv * KV_GROUPS_B + g, b, 0, 0)),
                  pl.BlockSpec((None, CMP_LEN, HEAD_DIM), lambda kv, b, g: (kv, 0, 0)),
                  pl.BlockSpec((None, CMP_LEN, HEAD_DIM, HEAD_DIM), lambda kv, b, g: (kv, 0, 0, 0)),
                  pl.BlockSpec((None, HEAD_DIM, HEAD_DIM), lambda kv, b, g: (kv, 0, 0))],
        out_specs=[pl.BlockSpec((None, None, n_chunks, HEAD_DIM), lambda kv, b, g: (kv, b * KV_GROUPS_B + g, 0, 0)),
                   pl.BlockSpec((None, None, HEAD_DIM, n_chunks), lambda kv, b, g: (kv, b * KV_GROUPS_B + g, 0, 0))],
        out_shape=[jax.ShapeDtypeStruct((2, bg, n_chunks, HEAD_DIM), BF16),
                   jax.ShapeDtypeStruct((2, bg, HEAD_DIM, n_chunks), BF16)],
        compiler_params=_params("parallel", "parallel", "parallel"),
        name="compress",
    )(pf4, pe, w1, w2)


NSA_TQ = 128
NSA_TK = 512
NSA_SUB = 256


def _select_blocks(score):
    n_s, tq = score.shape
    groups = n_s // 8
    rows8 = [score[8 * v:8 * v + 8, :] for v in range(groups)]
    rank8 = [jnp.zeros((8, tq), F32) for _ in range(groups)]
    sub = lax.broadcasted_iota(jnp.int32, (8, 1), 0)
    for jp in range(n_s):
        vp, sp = divmod(jp, 8)
        row = jnp.broadcast_to(rows8[vp][sp:sp + 1, :], (8, tq))
        for v in range(groups):
            if v > vp:
                beats = jnp.where(row >= rows8[v], 1.0, 0.0)
            elif v < vp:
                beats = jnp.where(row > rows8[v], 1.0, 0.0)
            else:
                beats = jnp.where(sub > sp, jnp.where(row >= rows8[v], 1.0, 0.0),
                                  jnp.where(row > rows8[v], 1.0, 0.0))
            rank8[v] = rank8[v] + beats
    rank = jnp.concatenate(rank8, axis=0)
    return (rank < N_SELECT) & (score > -jnp.inf)


def _flash_block(s, vt, carry):
    m_i, l_i, acc = carry
    m_new = jnp.maximum(m_i, jnp.max(s, axis=0, keepdims=True))
    alpha = jnp.exp2(m_i - m_new)
    p = jnp.exp2(s - m_new)
    return m_new, alpha * l_i + jnp.sum(p, axis=0, keepdims=True), alpha * acc + _dot(vt, p.astype(BF16))


def _nsa_kernel(q_ref, kc_ref, vct_ref, ks_ref, vst_ref, kw_ref, vwt_ref, e_ref, gl_ref, o_ref, *, seq):
    tq, tk = NSA_TQ, NSA_TK
    cols = REP_B * tq
    n_cp = seq // CMP_STRIDE
    n_s = seq // SLC_BLOCK
    qi = pl.program_id(2)
    t0 = qi * tq
    q_t = jnp.concatenate([q_ref[r] for r in range(REP_B)], axis=1)
    lane = lax.broadcasted_iota(jnp.int32, (1, tq), 1)
    tpos = t0 + lane

    def per_head(x):
        return jnp.concatenate([x] * REP_B, axis=1)

    csub = lax.broadcasted_iota(jnp.int32, (n_cp, 1), 0)
    c_ok = (csub * CMP_STRIDE + (CMP_LEN - 1) <= tpos) & (csub < n_cp - 1)
    sc = _dot(kc_ref[...], q_t) + per_head(jnp.where(c_ok, 0.0, NEG))
    e = jnp.exp2(sc - jnp.max(sc, axis=0, keepdims=True))
    has_block = per_head(jnp.where(tpos >= CMP_LEN - 1, 1.0, 0.0))
    p_cmp = e * (has_block / jnp.maximum(jnp.sum(e, axis=0, keepdims=True), 1e-30))
    o_cmp = _dot(vct_ref[...], p_cmp.astype(BF16))

    ws = WIN + tq
    w0 = pl.multiple_of(jnp.maximum(t0 - WIN, 0), tq)
    dist = tpos - (w0 + lax.broadcasted_iota(jnp.int32, (ws, 1), 0))
    win_bias = per_head(jnp.where((dist >= 0) & (dist <= WIN - 1), 0.0, NEG))
    s_w = _dot(kw_ref[pl.ds(w0, ws), :], q_t) + win_bias
    p_w = jnp.exp2(s_w - jnp.max(s_w, axis=0, keepdims=True))
    o_win = _dot(vwt_ref[:, pl.ds(w0, ws)], p_w.astype(BF16)) * (1.0 / jnp.sum(p_w, axis=0, keepdims=True))

    p_sum = p_cmp[:, 0:tq]
    for r in range(1, REP_B):
        p_sum = p_sum + p_cmp[:, r * tq:(r + 1) * tq]
    jj = lax.broadcasted_iota(jnp.int32, (n_s, 1), 0)
    cidx = lax.broadcasted_iota(jnp.int32, (1, n_cp), 1)
    ratio = SLC_BLOCK // CMP_STRIDE
    c_first = ratio * jj - (CMP_LEN // CMP_STRIDE - 1)
    hits = jnp.where((cidx >= c_first) & (cidx < ratio * (jj + 1)), 1.0, 0.0).astype(BF16)
    p_hi = p_sum.astype(BF16)
    p_lo = (p_sum - p_hi.astype(F32)).astype(BF16)
    imp = _dot(hits, p_hi) + _dot(hits, p_lo)
    qblk = tpos >> (SLC_BLOCK.bit_length() - 1)
    forced = (jj == 0) | (jj == qblk) | (jj == qblk - 1)
    valid = jj * SLC_BLOCK <= tpos
    score = jnp.where(forced, FORCE_SCORE, jnp.where(valid, imp, -jnp.inf))
    sel = _select_blocks(score)
    bias = jnp.where(sel, 0.0, SEL_BIAS)
    if n_s < LANES:
        bias = jnp.concatenate([bias, jnp.full((LANES - n_s, tq), SEL_BIAS, F32)], axis=0)
    q_aug = jnp.concatenate([q_t, per_head(bias.astype(BF16))], axis=0)

    def slc_scores(kt):
        k0 = pl.multiple_of(kt * tk, tk)
        k_aug = jnp.concatenate([ks_ref[pl.ds(k0, tk), :], e_ref[pl.ds(k0, tk), :]], axis=1)
        return _dot(k_aug, q_aug)

    def slc_values(kt):
        return vst_ref[:, pl.ds(pl.multiple_of(kt * tk, tk), tk)]

    def slc_step(kt, stats):
        return _flash_block(slc_scores(kt), slc_values(kt), stats)

    last = (t0 + tq - 1) // tk
    stats = (jnp.full((1, cols), NEG, F32), jnp.zeros((1, cols), F32), jnp.zeros((HEAD_DIM, cols), F32))
    stats = lax.fori_loop(0, last, slc_step, stats)
    tok = last * tk + lax.broadcasted_iota(jnp.int32, (tk, 1), 0)
    s_last = slc_scores(last) + per_head(jnp.where(tok <= tpos, 0.0, NEG))
    _, l_s, acc_s = _flash_block(s_last, slc_values(last), stats)
    o_slc = acc_s * (1.0 / l_s)

    gates = jax.nn.sigmoid(gl_ref[...].T)
    for r in range(REP_B):
        sl = slice(r * tq, (r + 1) * tq)
        merged = (gates[3 * r:3 * r + 1, :] * o_cmp[:, sl]
                  + gates[3 * r + 1:3 * r + 2, :] * o_slc[:, sl]
                  + gates[3 * r + 2:3 * r + 3, :] * o_win[:, sl])
        o_ref[:, r * HEAD_DIM:(r + 1) * HEAD_DIM] = merged.T


def nsa_attention(pr, pc, kvc, kvct, gl, batch, seq):
    tq = NSA_TQ
    n_s = seq // SLC_BLOCK
    assert seq % NSA_TK == 0 and NSA_TK % tq == 0 and seq >= WIN + tq and n_s <= LANES and n_s % 8 == 0
    assert SLC_BLOCK & (SLC_BLOCK - 1) == 0
    n_cp = seq // CMP_STRIDE
    nq = seq // tq
    pr4 = pr.reshape(pr.shape[0], batch, seq, HEAD_DIM)
    block_onehot = (jnp.arange(seq)[:, None] // SLC_BLOCK == jnp.arange(LANES)[None, :]).astype(BF16)

    def k_spec(off):
        return pl.BlockSpec((None, None, seq, HEAD_DIM), lambda b, g, i: (off + g, b, 0, 0))

    def vt_spec(off):
        return pl.BlockSpec((None, None, HEAD_DIM, seq), lambda b, g, i: (off + g, b, 0, 0))

    return pl.pallas_call(
        functools.partial(_nsa_kernel, seq=seq),
        grid=(batch, KV_GROUPS_B, nq),
        in_specs=[pl.BlockSpec((REP_B, None, HEAD_DIM, tq), lambda b, g, i: (PC_QB // REP_B + g, b, 0, i)),
                  pl.BlockSpec((None, None, n_cp, HEAD_DIM), lambda b, g, i: (0, b * KV_GROUPS_B + g, 0, 0)),
                  pl.BlockSpec((None, None, HEAD_DIM, n_cp), lambda b, g, i: (1, b * KV_GROUPS_B + g, 0, 0)),
                  k_spec(PR_KS), vt_spec(PC_VS), k_spec(PR_KW), vt_spec(PC_VW),
                  pl.BlockSpec((seq, LANES), lambda b, g, i: (0, 0)),
                  pl.BlockSpec((tq, LANES), lambda b, g, i: (b * nq + i, g))],
        out_specs=pl.BlockSpec((tq, REP_B * HEAD_DIM), lambda b, g, i: (b * nq + i, g)),
        out_shape=jax.ShapeDtypeStruct((batch * seq, WIDTH_B), F32),
        compiler_params=_params("parallel", "parallel", "arbitrary"),
        name="nsa_attention",
    )(pc, kvc, kvct, pr4, pc, pr4, pc, block_onehot, gl)


def _mm_res_kernel(a_ref, w_ref, r_ref, o_ref):
    o_ref[...] = _dot(a_ref[...], w_ref[...]) + r_ref[...]


def matmul_residual(a, w, res, tm, tn):
    m, k = a.shape
    n = w.shape[1]
    return pl.pallas_call(
        _mm_res_kernel,
        grid=(m // tm, n // tn),
        in_specs=[pl.BlockSpec((tm, k), lambda i, j: (i, 0)),
                  pl.BlockSpec((k, tn), lambda i, j: (0, j)),
                  pl.BlockSpec((tm, tn), lambda i, j: (i, j))],
        out_specs=pl.BlockSpec((tm, tn), lambda i, j: (i, j)),
        out_shape=jax.ShapeDtypeStruct((m, n), F32),
        compiler_params=_params("parallel", "arbitrary"),
        name="matmul_residual",
    )(a, w, res)


def _ffn_up_kernel(h_ref, wg_ref, wu_ref, o_ref):
    h = h_ref[...]
    g = _dot(h, wg_ref[...])
    u = _dot(h, wu_ref[...])
    o_ref[...] = (g * jax.nn.sigmoid(g) * u).astype(o_ref.dtype)


def ffn_up(h, wg, wu, tm, tn):
    m, k = h.shape
    n = wg.shape[1]
    return pl.pallas_call(
        _ffn_up_kernel,
        grid=(m // tm, n // tn),
        in_specs=[pl.BlockSpec((tm, k), lambda i, j: (i, 0)),
                  pl.BlockSpec((k, tn), lambda i, j: (0, j)),
                  pl.BlockSpec((k, tn), lambda i, j: (0, j))],
        out_specs=pl.BlockSpec((tm, tn), lambda i, j: (i, j)),
        out_shape=jax.ShapeDtypeStruct((m, n), BF16),
        compiler_params=_params("parallel", "arbitrary"),
        name="ffn_up",
    )(h, wg, wu)


def _layer(x, norm_attn, w_in, ck_pe, ck_w1, ck_w2, cv_pe, cv_w1, cv_w2,
           out_norm_a, out_norm_b, w_out, norm_ffn, w_gate, w_up, w_down, cos_t, sin_t, batch, seq):
    w_main = w_in[:, :D_MAIN].astype(BF16)
    w_gl = w_in[:, D_MAIN:].reshape(D_MODEL, KV_GROUPS_B, REP_B * 3)
    w_gl = jnp.pad(w_gl, ((0, 0), (0, 0), (0, LANES - REP_B * 3))).reshape(D_MODEL, KV_GROUPS_B * LANES).astype(BF16)
    pe = jnp.stack([ck_pe, cv_pe])
    w1 = jnp.stack([ck_w1, cv_w1]).reshape(2, CMP_LEN, HEAD_DIM, HEAD_DIM).astype(BF16)
    w2 = jnp.stack([ck_w2, cv_w2]).astype(BF16)

    h, gl = rms_gate(x, norm_attn, w_gl)
    pf = project(h, w_main, cos_t, sin_t, F32_TILES, F32, batch, seq)
    pr = project(h, w_main, cos_t, sin_t, ROW_TILES, BF16, batch, seq)
    pc = project(h, w_main, cos_t, sin_t, COL_TILES, BF16, batch, seq, transposed=True)
    o_a = dilated_attention(pf, batch, seq)
    kvc, kvct = compress(pf, pe, w1, w2, batch, seq)
    o_b = nsa_attention(pr, pc, kvc, kvct, gl, batch, seq)
    mixed = rms_pair(o_a, o_b, out_norm_a, out_norm_b)
    x1 = matmul_residual(mixed, w_out.astype(BF16), x, tm=1024, tn=512)
    h2 = rms(x1, norm_ffn, BF16)
    act = ffn_up(h2, w_gate.astype(BF16), w_up.astype(BF16), tm=1024, tn=256)
    x2 = matmul_residual(act, w_down.astype(BF16), x1, tm=512, tn=256)
    return x2


def kernel(x, norm_attn, w_in, ck_pe, ck_w1, ck_w2, cv_pe, cv_w1, cv_w2, out_norm_a, out_norm_b, w_out,
           norm_ffn, w_gate, w_up, w_down, norm_final):
    batch, seq, d = x.shape
    depth = w_in.shape[0]
    cos_t, sin_t = _rope_tables(seq)
    xf = x.reshape(batch * seq, d)
    for l in range(depth):
        xf = _layer(xf, norm_attn[l], w_in[l], ck_pe[l], ck_w1[l], ck_w2[l], cv_pe[l], cv_w1[l], cv_w2[l],
                    out_norm_a[l], out_norm_b[l], w_out[l], norm_ffn[l], w_gate[l], w_up[l], w_down[l],
                    cos_t, sin_t, batch, seq)
    return rms(xf, norm_final, F32).reshape(batch, seq, d)
```

```python
import functools
import math

import jax
import jax.numpy as jnp
from jax import lax
from jax.experimental import pallas as pl
from jax.experimental.pallas import tpu as pltpu

F32 = jnp.float32
BF16 = jnp.bfloat16

D_MODEL = 4096
HEAD_DIM = 128
HEADS_A = 16
HEADS_B = 16
KV_GROUPS_B = 4
REP_B = HEADS_B // KV_GROUPS_B
WIDTH_A = HEADS_A * HEAD_DIM
WIDTH_B = HEADS_B * HEAD_DIM
DILATED_CONFIGS = ((128, 1), (512, 4), (2048, 16))
BLK = 128
CMP_LEN = 32
CMP_STRIDE = 16
SLC_BLOCK = 64
N_SELECT = 16
WIN = 512
FORCE_SCORE = 1e6
ROPE_THETA = 500000.0
ROPE_DIM = HEAD_DIM // 4
EPS = 1e-5
Q_SCALE = HEAD_DIM ** -0.5 * math.log2(math.e)
N_GATES = 3 * HEADS_B
D_MAIN = 3 * WIDTH_A + WIDTH_B + 6 * KV_GROUPS_B * HEAD_DIM

VMEM_LIMIT_BYTES = 56 * 1024 * 1024
LANES = 128
NEG = -1e30
SEL_BIAS = -32768.0

PROJ_TILE = 4 * HEAD_DIM
F32_TILES = tuple(range(12)) + (16, 17)
ROW_TILES = (18, 20)
COL_TILES = (12, 13, 14, 15, 19, 21)
Q_TILES = (0, 1, 2, 3, 12, 13, 14, 15)
V_TILES = (8, 9, 10, 11, 17, 19, 21)
PF_QA, PF_KA, PF_VA, PF_KC, PF_VC = 0, 16, 32, 48, 52
PR_KS, PR_KW = 0, 4
PC_QB, PC_VS, PC_VW = 0, 16, 20


def _params(*sem):
    return pltpu.CompilerParams(dimension_semantics=sem, vmem_limit_bytes=VMEM_LIMIT_BYTES)


def _dot(a, b):
    return jnp.dot(a, b, preferred_element_type=F32)


def _dot_nt(a, b):
    return lax.dot_general(a, b, (((1,), (1,)), ((), ())), preferred_element_type=F32)


def _rms(x, g):
    return x * lax.rsqrt(jnp.mean(x * x, axis=-1, keepdims=True) + EPS) * g


def _rms_gate_kernel(x_ref, g_ref, wgl_ref, h_ref, gl_ref):
    hb = _rms(x_ref[...], g_ref[...]).astype(BF16)
    h_ref[...] = hb
    gl_ref[...] = _dot(hb, wgl_ref[...])


def rms_gate(x, gain, w_gl, tm=256):
    m, d = x.shape
    ng = w_gl.shape[1]
    return pl.pallas_call(
        _rms_gate_kernel,
        grid=(m // tm,),
        in_specs=[pl.BlockSpec((tm, d), lambda i: (i, 0)),
                  pl.BlockSpec((1, d), lambda i: (0, 0)),
                  pl.BlockSpec((d, ng), lambda i: (0, 0))],
        out_specs=[pl.BlockSpec((tm, d), lambda i: (i, 0)),
                   pl.BlockSpec((tm, ng), lambda i: (i, 0))],
        out_shape=[jax.ShapeDtypeStruct((m, d), BF16), jax.ShapeDtypeStruct((m, ng), F32)],
        compiler_params=_params("parallel"),
        name="rms_gate",
    )(x, gain.reshape(1, d), w_gl)


def _rms_kernel(x_ref, g_ref, o_ref):
    o_ref[...] = _rms(x_ref[...], g_ref[...]).astype(o_ref.dtype)


def rms(x, gain, out_dtype, tm=256):
    m, d = x.shape
    return pl.pallas_call(
        _rms_kernel,
        grid=(m // tm,),
        in_specs=[pl.BlockSpec((tm, d), lambda i: (i, 0)),
                  pl.BlockSpec((1, d), lambda i: (0, 0))],
        out_specs=pl.BlockSpec((tm, d), lambda i: (i, 0)),
        out_shape=jax.ShapeDtypeStruct((m, d), out_dtype),
        compiler_params=_params("parallel"),
        name="rms",
    )(x, gain.reshape(1, d))


def _rms_pair_kernel(a_ref, b_ref, ga_ref, gb_ref, o_ref):
    n_heads = a_ref.shape[0]
    wa = n_heads * HEAD_DIM
    ssq = jnp.sum(a_ref[0] * a_ref[0], axis=-1, keepdims=True)
    for hh in range(1, n_heads):
        ssq = ssq + jnp.sum(a_ref[hh] * a_ref[hh], axis=-1, keepdims=True)
    inv = lax.rsqrt(ssq * (1.0 / wa) + EPS)
    for hh in range(n_heads):
        cols = slice(hh * HEAD_DIM, (hh + 1) * HEAD_DIM)
        o_ref[:, cols] = (a_ref[hh] * inv * ga_ref[:, cols]).astype(o_ref.dtype)
    o_ref[:, wa:] = _rms(b_ref[...], gb_ref[...]).astype(o_ref.dtype)


def rms_pair(a, b, ga, gb, tm=256):
    n_heads, m, _ = a.shape
    wa = n_heads * HEAD_DIM
    wb = b.shape[1]
    return pl.pallas_call(
        _rms_pair_kernel,
        grid=(m // tm,),
        in_specs=[pl.BlockSpec((n_heads, tm, HEAD_DIM), lambda i: (0, i, 0)),
                  pl.BlockSpec((tm, wb), lambda i: (i, 0)),
                  pl.BlockSpec((1, wa), lambda i: (0, 0)),
                  pl.BlockSpec((1, wb), lambda i: (0, 0))],
        out_specs=pl.BlockSpec((tm, wa + wb), lambda i: (i, 0)),
        out_shape=jax.ShapeDtypeStruct((m, wa + wb), BF16),
        compiler_params=_params("parallel"),
        name="rms_pair",
    )(a, b, ga.reshape(1, wa), gb.reshape(1, wb))


def _rope_tables(seq):
    inv = ROPE_THETA ** (-jnp.arange(0, ROPE_DIM, 2, dtype=F32) / ROPE_DIM)
    ang = jnp.arange(seq, dtype=F32)[:, None] * inv[None, :]
    cos, sin = jnp.cos(ang), jnp.sin(ang)
    ones = jnp.ones((seq, HEAD_DIM - ROPE_DIM), F32)
    cos_t = jnp.concatenate([cos, cos, ones], axis=1)
    sin_t = jnp.concatenate([-sin, sin, 0.0 * ones], axis=1)
    return cos_t, sin_t


PROJ_ROWS = 256
VIEW_RES = DILATED_CONFIGS[-1][1]


def _proj_kernel(h_ref, w_ref, cos_ref, sin_ref, o_ref, *scratch, src_tiles, layout):
    jt = pl.program_id(1)
    is_v = functools.reduce(jnp.logical_or, [jt == n for n, t in enumerate(src_tiles) if t in V_TILES], False)
    is_q = functools.reduce(jnp.logical_or, [jt == n for n, t in enumerate(src_tiles) if t in Q_TILES], False)
    heads_per_tile = PROJ_TILE // HEAD_DIM
    tm = h_ref.shape[0]
    scale = jnp.where(is_q, Q_SCALE, 1.0).astype(F32)
    lane = lax.broadcasted_iota(jnp.int32, (1, HEAD_DIM), 1)
    first_half = lane < ROPE_DIM // 2

    for c in range(tm // PROJ_ROWS):
        rows = slice(c * PROJ_ROWS, (c + 1) * PROJ_ROWS)
        y = _dot(h_ref[rows, :], w_ref[...])
        cos_t = jnp.where(is_v, 1.0, cos_ref[rows, :]) * scale
        sin_t = jnp.where(is_v, 0.0, sin_ref[rows, :]) * scale
        for hh in range(heads_per_tile):
            yh = y[:, hh * HEAD_DIM:(hh + 1) * HEAD_DIM]
            partner = jnp.where(first_half,
                                pltpu.roll(yh, HEAD_DIM - ROPE_DIM // 2, axis=1),
                                pltpu.roll(yh, ROPE_DIM // 2, axis=1))
            val = yh * cos_t + partner * sin_t
            if layout == "cols":
                for cc in range(PROJ_ROWS // LANES):
                    col0 = c * PROJ_ROWS + cc * LANES
                    o_ref[hh, :, col0:col0 + LANES] = val[cc * LANES:(cc + 1) * LANES, :].T.astype(o_ref.dtype)
            elif layout == "view":
                stage_ref, = scratch
                stage_ref[hh] = val
                nv = PROJ_ROWS // VIEW_RES
                for r in range(VIEW_RES):
                    o_ref[hh, c * nv:(c + 1) * nv, r * HEAD_DIM:(r + 1) * HEAD_DIM] = (
                        stage_ref[hh, pl.ds(r, nv, stride=VIEW_RES), :].astype(o_ref.dtype))
            else:
                o_ref[hh, rows, :] = val.astype(o_ref.dtype)


def project(h, w, cos_t, sin_t, src_tiles, out_dtype, batch, seq, layout="rows", tm=1024):
    m, d = h.shape
    n_tiles = len(src_tiles)
    heads_per_tile = PROJ_TILE // HEAD_DIM
    tm = min(tm, seq)
    pos_blocks = seq // tm

    def w_map(i, j):
        col = functools.reduce(lambda acc, nt: jnp.where(j == nt[0], nt[1], acc),
                               list(enumerate(src_tiles)), 0)
        return (0, col)

    scratch = []
    if layout == "cols":
        out_spec = pl.BlockSpec((heads_per_tile, None, HEAD_DIM, tm),
                                lambda i, j: (j, i // pos_blocks, 0, i % pos_blocks))
        out_shape = jax.ShapeDtypeStruct((n_tiles * heads_per_tile, batch, HEAD_DIM, seq), out_dtype)
    elif layout == "view":
        out_spec = pl.BlockSpec((heads_per_tile, None, tm // VIEW_RES, VIEW_RES * HEAD_DIM),
                                lambda i, j: (j, i // pos_blocks, i % pos_blocks, 0))
        out_shape = jax.ShapeDtypeStruct((n_tiles * heads_per_tile, batch, seq // VIEW_RES, VIEW_RES * HEAD_DIM),
                                         out_dtype)
        scratch = [pltpu.VMEM((heads_per_tile, PROJ_ROWS, HEAD_DIM), F32)]
    else:
        out_spec = pl.BlockSpec((heads_per_tile, tm, HEAD_DIM), lambda i, j: (j, i, 0))
        out_shape = jax.ShapeDtypeStruct((n_tiles * heads_per_tile, m, HEAD_DIM), out_dtype)

    return pl.pallas_call(
        functools.partial(_proj_kernel, src_tiles=src_tiles, layout=layout),
        grid=(m // tm, n_tiles),
        in_specs=[pl.BlockSpec((tm, d), lambda i, j: (i, 0)),
                  pl.BlockSpec((d, PROJ_TILE), w_map),
                  pl.BlockSpec((tm, HEAD_DIM), lambda i, j: (i % pos_blocks, 0)),
                  pl.BlockSpec((tm, HEAD_DIM), lambda i, j: (i % pos_blocks, 0))],
        out_specs=out_spec,
        out_shape=out_shape,
        scratch_shapes=scratch,
        compiler_params=_params("parallel", "arbitrary"),
        name="project_" + layout,
    )(h, w, cos_t, sin_t)


DIL_UNROLL = 8
DIL_RES = VIEW_RES


def _dilated_kernel(q_ref, k_ref, v_ref, o_ref, acc_ref, m_ref, l_ref, bias_ref, *, seq):
    for ci, (window, dil) in enumerate(DILATED_CONFIGS):
        band = window // dil
        n_pieces = DIL_RES // dil
        pr = BLK // n_pieces
        kr = 2 * pr
        tiles_per_res = seq // (BLK * dil)
        assert band <= BLK and pr % 8 == 0 and pr & (pr - 1) == 0

        q_row = lax.broadcasted_iota(jnp.int32, (BLK, 1), 0)
        k_row = lax.broadcasted_iota(jnp.int32, (1, 2 * BLK), 1)
        q_sub = n_pieces * (q_row & (pr - 1)) + (q_row >> (pr.bit_length() - 1))
        k_sub = n_pieces * (k_row & (kr - 1)) + (k_row >> (kr.bit_length() - 1))
        for lead in range(2):
            dist = (q_sub + lead * BLK) - k_sub
            bias_ref[2 * ci + lead] = jnp.where((dist >= 0) & (dist <= band), 0.0, NEG)

        def body(step, carry, ci=ci, dil=dil, band=band, n_pieces=n_pieces, pr=pr, kr=kr,
                 tiles_per_res=tiles_per_res):
            def pieces(ref, row0, nrows, r):
                return [ref[pl.ds(row0, nrows), pl.ds(pl.multiple_of((r + dil * a) * HEAD_DIM, HEAD_DIM), HEAD_DIM)]
                        for a in range(n_pieces)]

            tiles = []
            for u in range(DIL_UNROLL):
                idx = step * DIL_UNROLL + u
                r = idx // tiles_per_res
                i = idx - r * tiles_per_res
                kb = jnp.maximum(i - 1, 0)
                q_row0 = pl.multiple_of(i * pr, pr)
                k_row0 = pl.multiple_of(kb * pr, pr)
                m_old = None if ci == 0 else jnp.concatenate(pieces(m_ref, q_row0, pr, r), axis=0)
                tiles.append((r, i, kb, q_row0, k_row0, m_old))

            results = []
            for (r, i, kb, q_row0, k_row0, m_old) in tiles:
                q = jnp.concatenate(pieces(q_ref, q_row0, pr, r), axis=0).astype(BF16)
                k = jnp.concatenate(pieces(k_ref, k_row0, kr, r), axis=0).astype(BF16)
                v = jnp.concatenate(pieces(v_ref, k_row0, kr, r), axis=0).astype(BF16)
                s = _dot_nt(q, k)
                s = s + bias_ref[2 * ci + i - kb]
                m_tile = jnp.max(s, axis=1, keepdims=True)
                if ci == 0:
                    m_new = jnp.broadcast_to(m_tile, (BLK, LANES))
                else:
                    m_new = jnp.maximum(m_old, m_tile)
                p = jnp.exp2(s - jnp.concatenate([m_new, m_new], axis=1))
                l_tile = jnp.sum(p, axis=1, keepdims=True)
                pv = _dot(p.astype(BF16), v)
                results.append((r, q_row0, m_old, m_new, l_tile, pv))

            for (r, q_row0, m_old, m_new, l_tile, pv) in results:
                if ci == 0:
                    l_new = jnp.broadcast_to(l_tile, (BLK, LANES))
                    acc_new = pv
                else:
                    alpha = jnp.exp2(m_old - m_new)
                    l_new = alpha * jnp.concatenate(pieces(l_ref, q_row0, pr, r), axis=0) + l_tile
                    acc_new = alpha * jnp.concatenate(pieces(acc_ref, q_row0, pr, r), axis=0) + pv
                for a in range(n_pieces):
                    col = pl.ds(pl.multiple_of((r + dil * a) * HEAD_DIM, HEAD_DIM), HEAD_DIM)
                    rows = slice(a * pr, (a + 1) * pr)
                    m_ref[pl.ds(q_row0, pr), col] = m_new[rows]
                    l_ref[pl.ds(q_row0, pr), col] = l_new[rows]
                    acc_ref[pl.ds(q_row0, pr), col] = acc_new[rows]
            return carry

        lax.fori_loop(0, seq // (BLK * DIL_UNROLL), body, 0)
    o_ref[...] = acc_ref[...] / l_ref[...]


def dilated_attention(pf, batch, seq):
    assert all(DIL_RES % d == 0 for _, d in DILATED_CONFIGS)
    assert seq % (BLK * DIL_RES) == 0 and seq // DIL_RES >= 2 * BLK and (seq // BLK) % DIL_UNROLL == 0
    rows = seq // DIL_RES
    width = DIL_RES * HEAD_DIM
    view = pf

    def spec(off):
        return pl.BlockSpec((None, None, rows, width), lambda b, h: (off + h, b, 0, 0))

    out = pl.pallas_call(
        functools.partial(_dilated_kernel, seq=seq),
        grid=(batch, HEADS_A),
        in_specs=[spec(PF_QA), spec(PF_KA), spec(PF_VA)],
        out_specs=spec(0),
        out_shape=jax.ShapeDtypeStruct((HEADS_A, batch, rows, width), F32),
        scratch_shapes=[pltpu.VMEM((rows, width), F32)] * 3
        + [pltpu.VMEM((2 * len(DILATED_CONFIGS), BLK, 2 * BLK), F32)],
        compiler_params=_params("parallel", "parallel"),
        name="dilated_attention",
    )(view, view, view)
    return out.reshape(HEADS_A, batch * seq, HEAD_DIM)


def _compress_kernel(t_ref, pe_ref, w1_ref, w2_ref, o_ref, ot_ref, *, seq):
    n_chunks = seq // CMP_STRIDE
    first = jnp.zeros((n_chunks, HEAD_DIM), F32)
    second = jnp.zeros((n_chunks, HEAD_DIM), F32)
    for i in range(CMP_STRIDE):
        ti = t_ref[:, i * HEAD_DIM:(i + 1) * HEAD_DIM]
        first += _dot((ti + pe_ref[pl.ds(i, 1), :]).astype(BF16), w1_ref[i])
        second += _dot((ti + pe_ref[pl.ds(CMP_STRIDE + i, 1), :]).astype(BF16), w1_ref[CMP_STRIDE + i])
    pre = first + pltpu.roll(second, n_chunks - 1, axis=0)
    out = _dot(jax.nn.gelu(pre).astype(BF16), w2_ref[...])
    o_ref[...] = out.astype(o_ref.dtype)
    ot_ref[...] = out.T.astype(ot_ref.dtype)


def compress(pf, pe, w1, w2, batch, seq):
    assert CMP_LEN == 2 * CMP_STRIDE and CMP_STRIDE == VIEW_RES
    n_chunks = seq // CMP_STRIDE
    pf4 = pf
    bg = batch * KV_GROUPS_B
    return pl.pallas_call(
        functools.partial(_compress_kernel, seq=seq),
        grid=(2, batch, KV_GROUPS_B),
        in_specs=[pl.BlockSpec((None, None, n_chunks, VIEW_RES * HEAD_DIM),
                               lambda kv, b, g: (PF_KC + kv * KV_GROUPS_B + g, b, 0, 0)),
                  pl.BlockSpec((None, CMP_LEN, HEAD_DIM), lambda kv, b, g: (kv, 0, 0)),
                  pl.BlockSpec((None, CMP_LEN, HEAD_DIM, HEAD_DIM), lambda kv, b, g: (kv, 0, 0, 0)),
                  pl.BlockSpec((None, HEAD_DIM, HEAD_DIM), lambda kv, b, g: (kv, 0, 0))],
        out_specs=[pl.BlockSpec((None, None, n_chunks, HEAD_DIM), lambda kv, b, g: (kv, b * KV_GROUPS_B + g, 0, 0)),
                   pl.BlockSpec((None, None, HEAD_DIM, n_chunks), lambda kv, b, g: (kv, b * KV_GROUPS_B + g, 0, 0))],
        out_shape=[jax.ShapeDtypeStruct((2, bg, n_chunks, HEAD_DIM), BF16),
                   jax.ShapeDtypeStruct((2, bg, HEAD_DIM, n_chunks), BF16)],
        compiler_params=_params("parallel", "parallel", "parallel"),
        name="compress",
    )(pf4, pe, w1, w2)


NSA_TQ = 128
NSA_TK = 512
NSA_SUB = 256


def _select_blocks(score):
    n_s, tq = score.shape
    groups = n_s // 8
    rows8 = [score[8 * v:8 * v + 8, :] for v in range(groups)]
    rank8 = [jnp.zeros((8, tq), F32) for _ in range(groups)]
    sub = lax.broadcasted_iota(jnp.int32, (8, 1), 0)
    for jp in range(n_s):
        vp, sp = divmod(jp, 8)
        row = jnp.broadcast_to(rows8[vp][sp:sp + 1, :], (8, tq))
        for v in range(groups):
            if v > vp:
                beats = jnp.where(row >= rows8[v], 1.0, 0.0)
            elif v < vp:
                beats = jnp.where(row > rows8[v], 1.0, 0.0)
            else:
                beats = jnp.where(sub > sp, jnp.where(row >= rows8[v], 1.0, 0.0),
                                  jnp.where(row > rows8[v], 1.0, 0.0))
            rank8[v] = rank8[v] + beats
    rank = jnp.concatenate(rank8, axis=0)
    return (rank < N_SELECT) & (score > -jnp.inf)


def _flash_block(s, vt, carry):
    m_i, l_i, acc = carry
    m_new = jnp.maximum(m_i, jnp.max(s, axis=0, keepdims=True))
    alpha = jnp.exp2(m_i - m_new)
    p = jnp.exp2(s - m_new)
    return m_new, alpha * l_i + jnp.sum(p, axis=0, keepdims=True), alpha * acc + _dot(vt, p.astype(BF16))


def _nsa_kernel(q_ref, kc_ref, vct_ref, ks_ref, vst_ref, kw_ref, vwt_ref, e_ref, gl_ref, o_ref, *, seq):
    tq, tk = NSA_TQ, NSA_TK
    cols = REP_B * tq
    n_cp = seq // CMP_STRIDE
    n_s = seq // SLC_BLOCK
    qi = pl.program_id(2)
    t0 = qi * tq
    q_t = jnp.concatenate([q_ref[r] for r in range(REP_B)], axis=1)
    lane = lax.broadcasted_iota(jnp.int32, (1, tq), 1)
    tpos = t0 + lane

    def per_head(x):
        return jnp.concatenate([x] * REP_B, axis=1)

    csub = lax.broadcasted_iota(jnp.int32, (n_cp, 1), 0)
    c_ok = (csub * CMP_STRIDE + (CMP_LEN - 1) <= tpos) & (csub < n_cp - 1)
    sc = _dot(kc_ref[...], q_t) + per_head(jnp.where(c_ok, 0.0, NEG))
    e = jnp.exp2(sc - jnp.max(sc, axis=0, keepdims=True))
    has_block = per_head(jnp.where(tpos >= CMP_LEN - 1, 1.0, 0.0))
    p_cmp = e * (has_block / jnp.maximum(jnp.sum(e, axis=0, keepdims=True), 1e-30))
    o_cmp = _dot(vct_ref[...], p_cmp.astype(BF16))

    ws = WIN + tq
    w0 = pl.multiple_of(jnp.maximum(t0 - WIN, 0), tq)
    dist = tpos - (w0 + lax.broadcasted_iota(jnp.int32, (ws, 1), 0))
    win_bias = per_head(jnp.where((dist >= 0) & (dist <= WIN - 1), 0.0, NEG))
    s_w = _dot(kw_ref[pl.ds(w0, ws), :], q_t) + win_bias
    p_w = jnp.exp2(s_w - jnp.max(s_w, axis=0, keepdims=True))
    o_win = _dot(vwt_ref[:, pl.ds(w0, ws)], p_w.astype(BF16)) * (1.0 / jnp.sum(p_w, axis=0, keepdims=True))

    p_sum = p_cmp[:, 0:tq]
    for r in range(1, REP_B):
        p_sum = p_sum + p_cmp[:, r * tq:(r + 1) * tq]
    jj = lax.broadcasted_iota(jnp.int32, (n_s, 1), 0)
    cidx = lax.broadcasted_iota(jnp.int32, (1, n_cp), 1)
    ratio = SLC_BLOCK // CMP_STRIDE
    c_first = ratio * jj - (CMP_LEN // CMP_STRIDE - 1)
    hits = jnp.where((cidx >= c_first) & (cidx < ratio * (jj + 1)), 1.0, 0.0).astype(BF16)
    p_hi = p_sum.astype(BF16)
    p_lo = (p_sum - p_hi.astype(F32)).astype(BF16)
    imp = _dot(hits, p_hi) + _dot(hits, p_lo)
    qblk = tpos >> (SLC_BLOCK.bit_length() - 1)
    forced = (jj == 0) | (jj == qblk) | (jj == qblk - 1)
    valid = jj * SLC_BLOCK <= tpos
    score = jnp.where(forced, FORCE_SCORE, jnp.where(valid, imp, -jnp.inf))
    sel = _select_blocks(score)
    bias = jnp.where(sel, 0.0, SEL_BIAS)
    if n_s < LANES:
        bias = jnp.concatenate([bias, jnp.full((LANES - n_s, tq), SEL_BIAS, F32)], axis=0)
    q_aug = jnp.concatenate([q_t, per_head(bias.astype(BF16))], axis=0)

    def slc_scores(kt):
        k0 = pl.multiple_of(kt * tk, tk)
        k_aug = jnp.concatenate([ks_ref[pl.ds(k0, tk), :], e_ref[pl.ds(k0, tk), :]], axis=1)
        return _dot(k_aug, q_aug)

    def slc_values(kt):
        return vst_ref[:, pl.ds(pl.multiple_of(kt * tk, tk), tk)]

    def slc_step(kt, stats):
        return _flash_block(slc_scores(kt), slc_values(kt), stats)

    last = (t0 + tq - 1) // tk
    stats = (jnp.full((1, cols), NEG, F32), jnp.zeros((1, cols), F32), jnp.zeros((HEAD_DIM, cols), F32))
    stats = lax.fori_loop(0, last, slc_step, stats)
    tok = last * tk + lax.broadcasted_iota(jnp.int32, (tk, 1), 0)
    s_last = slc_scores(last) + per_head(jnp.where(tok <= tpos, 0.0, NEG))
    _, l_s, acc_s = _flash_block(s_last, slc_values(last), stats)
    o_slc = acc_s * (1.0 / l_s)

    gates = jax.nn.sigmoid(gl_ref[...].T)
    for r in range(REP_B):
        sl = slice(r * tq, (r + 1) * tq)
        merged = (gates[3 * r:3 * r + 1, :] * o_cmp[:, sl]
                  + gates[3 * r + 1:3 * r + 2, :] * o_slc[:, sl]
                  + gates[3 * r + 2:3 * r + 3, :] * o_win[:, sl])
        o_ref[:, r * HEAD_DIM:(r + 1) * HEAD_DIM] = merged.T


def nsa_attention(pr, pc, kvc, kvct, gl, batch, seq):
    tq = NSA_TQ
    n_s = seq // SLC_BLOCK
    assert seq % NSA_TK == 0 and NSA_TK % tq == 0 and seq >= WIN + tq and n_s <= LANES and n_s % 8 == 0
    assert SLC_BLOCK & (SLC_BLOCK - 1) == 0
    n_cp = seq // CMP_STRIDE
    nq = seq // tq
    pr4 = pr.reshape(pr.shape[0], batch, seq, HEAD_DIM)
    block_onehot = (jnp.arange(seq)[:, None] // SLC_BLOCK == jnp.arange(LANES)[None, :]).astype(BF16)

    def k_spec(off):
        return pl.BlockSpec((None, None, seq, HEAD_DIM), lambda b, g, i: (off + g, b, 0, 0))

    def vt_spec(off):
        return pl.BlockSpec((None, None, HEAD_DIM, seq), lambda b, g, i: (off + g, b, 0, 0))

    return pl.pallas_call(
        functools.partial(_nsa_kernel, seq=seq),
        grid=(batch, KV_GROUPS_B, nq),
        in_specs=[pl.BlockSpec((REP_B, None, HEAD_DIM, tq), lambda b, g, i: (PC_QB // REP_B + g, b, 0, i)),
                  pl.BlockSpec((None, None, n_cp, HEAD_DIM), lambda b, g, i: (0, b * KV_GROUPS_B + g, 0, 0)),
                  pl.BlockSpec((None, None, HEAD_DIM, n_cp), lambda b, g, i: (1, b * KV_GROUPS_B + g, 0, 0)),
                  k_spec(PR_KS), vt_spec(PC_VS), k_spec(PR_KW), vt_spec(PC_VW),
                  pl.BlockSpec((seq, LANES), lambda b, g, i: (0, 0)),
                  pl.BlockSpec((tq, LANES), lambda b, g, i: (b * nq + i, g))],
        out_specs=pl.BlockSpec((tq, REP_B * HEAD_DIM), lambda b, g, i: (b * nq + i, g)),
        out_shape=jax.ShapeDtypeStruct((batch * seq, WIDTH_B), F32),
        compiler_params=_params("parallel", "parallel", "arbitrary"),
        name="nsa_attention",
    )(pc, kvc, kvct, pr4, pc, pr4, pc, block_onehot, gl)


def _mm_res_kernel(a_ref, w_ref, r_ref, o_ref):
    o_ref[...] = _dot(a_ref[...], w_ref[...]) + r_ref[...]


def matmul_residual(a, w, res, tm, tn):
    m, k = a.shape
    n = w.shape[1]
    return pl.pallas_call(
        _mm_res_kernel,
        grid=(m // tm, n // tn),
        in_specs=[pl.BlockSpec((tm, k), lambda i, j: (i, 0)),
                  pl.BlockSpec((k, tn), lambda i, j: (0, j)),
                  pl.BlockSpec((tm, tn), lambda i, j: (i, j))],
        out_specs=pl.BlockSpec((tm, tn), lambda i, j: (i, j)),
        out_shape=jax.ShapeDtypeStruct((m, n), F32),
        compiler_params=_params("parallel", "arbitrary"),
        name="matmul_residual",
    )(a, w, res)


def _ffn_up_kernel(h_ref, wg_ref, wu_ref, o_ref):
    h = h_ref[...]
    g = _dot(h, wg_ref[...])
    u = _dot(h, wu_ref[...])
    o_ref[...] = (g * jax.nn.sigmoid(g) * u).astype(o_ref.dtype)


def ffn_up(h, wg, wu, tm, tn):
    m, k = h.shape
    n = wg.shape[1]
    return pl.pallas_call(
        _ffn_up_kernel,
        grid=(m // tm, n // tn),
        in_specs=[pl.BlockSpec((tm, k), lambda i, j: (i, 0)),
                  pl.BlockSpec((k, tn), lambda i, j: (0, j)),
                  pl.BlockSpec((k, tn), lambda i, j: (0, j))],
        out_specs=pl.BlockSpec((tm, tn), lambda i, j: (i, j)),
        out_shape=jax.ShapeDtypeStruct((m, n), BF16),
        compiler_params=_params("parallel", "arbitrary"),
        name="ffn_up",
    )(h, wg, wu)


def _cast_kernel(w_ref, o_ref):
    o_ref[...] = w_ref[...].astype(o_ref.dtype)


def cast_columns(w, layer, n_cols, tk=512, tn=2816):
    _, k, _ = w.shape
    return pl.pallas_call(
        _cast_kernel,
        grid=(k // tk, n_cols // tn),
        in_specs=[pl.BlockSpec((None, tk, tn), lambda i, j: (layer, i, j))],
        out_specs=pl.BlockSpec((tk, tn), lambda i, j: (i, j)),
        out_shape=jax.ShapeDtypeStruct((k, n_cols), BF16),
        compiler_params=_params("parallel", "parallel"),
        name="cast_columns",
    )(w)


def _layer(x, norm_attn, w_in, layer, ck_pe, ck_w1, ck_w2, cv_pe, cv_w1, cv_w2,
           out_norm_a, out_norm_b, w_out, norm_ffn, w_gate, w_up, w_down, cos_t, sin_t, batch, seq):
    w_main = lax.slice(w_in, (layer, 0, 0), (layer + 1, D_MODEL, D_MAIN)).astype(BF16).reshape(D_MODEL, D_MAIN)
    w_gl = lax.slice(w_in, (layer, 0, D_MAIN), (layer + 1, D_MODEL, D_MAIN + N_GATES))
    w_gl = w_gl.reshape(D_MODEL, KV_GROUPS_B, REP_B * 3)
    w_gl = jnp.pad(w_gl, ((0, 0), (0, 0), (0, LANES - REP_B * 3))).reshape(D_MODEL, KV_GROUPS_B * LANES).astype(BF16)
    pe = jnp.stack([ck_pe, cv_pe])
    w1 = jnp.stack([ck_w1, cv_w1]).reshape(2, CMP_LEN, HEAD_DIM, HEAD_DIM).astype(BF16)
    w2 = jnp.stack([ck_w2, cv_w2]).astype(BF16)

    h, gl = rms_gate(x, norm_attn, w_gl)
    pf = project(h, w_main, cos_t, sin_t, F32_TILES, F32, batch, seq, layout="view")
    pr = project(h, w_main, cos_t, sin_t, ROW_TILES, BF16, batch, seq)
    pc = project(h, w_main, cos_t, sin_t, COL_TILES, BF16, batch, seq, layout="cols")
    o_a = dilated_attention(pf, batch, seq)
    kvc, kvct = compress(pf, pe, w1, w2, batch, seq)
    o_b = nsa_attention(pr, pc, kvc, kvct, gl, batch, seq)
    mixed = rms_pair(o_a, o_b, out_norm_a, out_norm_b)
    x1 = matmul_residual(mixed, w_out.astype(BF16), x, tm=1024, tn=512)
    h2 = rms(x1, norm_ffn, BF16)
    act = ffn_up(h2, w_gate.astype(BF16), w_up.astype(BF16), tm=1024, tn=256)
    x2 = matmul_residual(act, w_down.astype(BF16), x1, tm=512, tn=512)
    return x2


def kernel(x, norm_attn, w_in, ck_pe, ck_w1, ck_w2, cv_pe, cv_w1, cv_w2, out_norm_a, out_norm_b, w_out,
           norm_ffn, w_gate, w_up, w_down, norm_final):
    batch, seq, d = x.shape
    depth = w_in.shape[0]
    cos_t, sin_t = _rope_tables(seq)
    xf = x.reshape(batch * seq, d)
    for l in range(depth):
        xf = _layer(xf, norm_attn[l], w_in, l, ck_pe[l], ck_w1[l], ck_w2[l], cv_pe[l], cv_w1[l], cv_w2[l],
                    out_norm_a[l], out_norm_b[l], w_out[l], norm_ffn[l], w_gate[l], w_up[l], w_down[l],
                    cos_t, sin_t, batch, seq)
    return rms(xf, norm_final, F32).reshape(batch, seq, d)
```

```python
import functools
import math

import jax
import jax.numpy as jnp
from jax import lax
from jax.experimental import pallas as pl
from jax.experimental.pallas import tpu as pltpu

F32 = jnp.float32
BF16 = jnp.bfloat16

D_MODEL = 4096
HEAD_DIM = 128
HEADS_A = 16
HEADS_B = 16
KV_GROUPS_B = 4
REP_B = HEADS_B // KV_GROUPS_B
WIDTH_A = HEADS_A * HEAD_DIM
WIDTH_B = HEADS_B * HEAD_DIM
DILATED_CONFIGS = ((128, 1), (512, 4), (2048, 16))
BLK = 128
CMP_LEN = 32
CMP_STRIDE = 16
SLC_BLOCK = 64
N_SELECT = 16
WIN = 512
FORCE_SCORE = 1e6
ROPE_THETA = 500000.0
ROPE_DIM = HEAD_DIM // 4
EPS = 1e-5
Q_SCALE = HEAD_DIM ** -0.5 * math.log2(math.e)
N_GATES = 3 * HEADS_B
D_MAIN = 3 * WIDTH_A + WIDTH_B + 6 * KV_GROUPS_B * HEAD_DIM

VMEM_LIMIT_BYTES = 56 * 1024 * 1024
LANES = 128
NEG = -1e30
SEL_BIAS = -32768.0

PROJ_TILE = 4 * HEAD_DIM
F32_TILES = tuple(range(12)) + (16, 17)
ROW_TILES = (18, 20)
COL_TILES = (12, 13, 14, 15, 19, 21)
Q_TILES = (0, 1, 2, 3, 12, 13, 14, 15)
V_TILES = (8, 9, 10, 11, 17, 19, 21)
PF_QA, PF_KA, PF_VA, PF_KC, PF_VC = 0, 16, 32, 48, 52
PR_KS, PR_KW = 0, 4
PC_QB, PC_VS, PC_VW = 0, 16, 20


def _params(*sem):
    return pltpu.CompilerParams(dimension_semantics=sem, vmem_limit_bytes=VMEM_LIMIT_BYTES)


def _dot(a, b):
    return jnp.dot(a, b, preferred_element_type=F32)


def _dot_nt(a, b):
    return lax.dot_general(a, b, (((1,), (1,)), ((), ())), preferred_element_type=F32)


def _rms(x, g):
    return x * lax.rsqrt(jnp.mean(x * x, axis=-1, keepdims=True) + EPS) * g


def _rms_gate_kernel(x_ref, g_ref, wgl_ref, h_ref, gl_ref):
    hb = _rms(x_ref[...], g_ref[...]).astype(BF16)
    h_ref[...] = hb
    gl_ref[...] = _dot(hb, wgl_ref[...])


def rms_gate(x, gain, w_gl, tm=256):
    m, d = x.shape
    ng = w_gl.shape[1]
    return pl.pallas_call(
        _rms_gate_kernel,
        grid=(m // tm,),
        in_specs=[pl.BlockSpec((tm, d), lambda i: (i, 0)),
                  pl.BlockSpec((1, d), lambda i: (0, 0)),
                  pl.BlockSpec((d, ng), lambda i: (0, 0))],
        out_specs=[pl.BlockSpec((tm, d), lambda i: (i, 0)),
                   pl.BlockSpec((tm, ng), lambda i: (i, 0))],
        out_shape=[jax.ShapeDtypeStruct((m, d), BF16), jax.ShapeDtypeStruct((m, ng), F32)],
        compiler_params=_params("parallel"),
        name="rms_gate",
    )(x, gain.reshape(1, d), w_gl)


def _rms_kernel(x_ref, g_ref, o_ref):
    o_ref[...] = _rms(x_ref[...], g_ref[...]).astype(o_ref.dtype)


def rms(x, gain, out_dtype, tm=256):
    m, d = x.shape
    return pl.pallas_call(
        _rms_kernel,
        grid=(m // tm,),
        in_specs=[pl.BlockSpec((tm, d), lambda i: (i, 0)),
                  pl.BlockSpec((1, d), lambda i: (0, 0))],
        out_specs=pl.BlockSpec((tm, d), lambda i: (i, 0)),
        out_shape=jax.ShapeDtypeStruct((m, d), out_dtype),
        compiler_params=_params("parallel"),
        name="rms",
    )(x, gain.reshape(1, d))


def _rms_pair_kernel(a_ref, b_ref, ga_ref, gb_ref, o_ref):
    n_heads = a_ref.shape[0]
    wa = n_heads * HEAD_DIM
    ssq = jnp.sum(a_ref[0] * a_ref[0], axis=-1, keepdims=True)
    for hh in range(1, n_heads):
        ssq = ssq + jnp.sum(a_ref[hh] * a_ref[hh], axis=-1, keepdims=True)
    inv = lax.rsqrt(ssq * (1.0 / wa) + EPS)
    for hh in range(n_heads):
        cols = slice(hh * HEAD_DIM, (hh + 1) * HEAD_DIM)
        o_ref[:, cols] = (a_ref[hh] * inv * ga_ref[:, cols]).astype(o_ref.dtype)
    o_ref[:, wa:] = _rms(b_ref[...], gb_ref[...]).astype(o_ref.dtype)


def rms_pair(a, b, ga, gb, tm=256):
    n_heads, m, _ = a.shape
    wa = n_heads * HEAD_DIM
    wb = b.shape[1]
    return pl.pallas_call(
        _rms_pair_kernel,
        grid=(m // tm,),
        in_specs=[pl.BlockSpec((n_heads, tm, HEAD_DIM), lambda i: (0, i, 0)),
                  pl.BlockSpec((tm, wb), lambda i: (i, 0)),
                  pl.BlockSpec((1, wa), lambda i: (0, 0)),
                  pl.BlockSpec((1, wb), lambda i: (0, 0))],
        out_specs=pl.BlockSpec((tm, wa + wb), lambda i: (i, 0)),
        out_shape=jax.ShapeDtypeStruct((m, wa + wb), BF16),
        compiler_params=_params("parallel"),
        name="rms_pair",
    )(a, b, ga.reshape(1, wa), gb.reshape(1, wb))


def _rope_tables(seq):
    inv = ROPE_THETA ** (-jnp.arange(0, ROPE_DIM, 2, dtype=F32) / ROPE_DIM)
    ang = jnp.arange(seq, dtype=F32)[:, None] * inv[None, :]
    cos, sin = jnp.cos(ang), jnp.sin(ang)
    ones = jnp.ones((seq, HEAD_DIM - ROPE_DIM), F32)
    cos_t = jnp.concatenate([cos, cos, ones], axis=1)
    sin_t = jnp.concatenate([-sin, sin, 0.0 * ones], axis=1)
    return cos_t, sin_t


PROJ_ROWS = 256
VIEW_RES = DILATED_CONFIGS[-1][1]


def _proj_kernel(h_ref, w_ref, cos_ref, sin_ref, o_ref, *scratch, src_tiles, layout):
    jt = pl.program_id(1)
    is_v = functools.reduce(jnp.logical_or, [jt == n for n, t in enumerate(src_tiles) if t in V_TILES], False)
    is_q = functools.reduce(jnp.logical_or, [jt == n for n, t in enumerate(src_tiles) if t in Q_TILES], False)
    heads_per_tile = PROJ_TILE // HEAD_DIM
    tm = h_ref.shape[0]
    scale = jnp.where(is_q, Q_SCALE, 1.0).astype(F32)
    lane = lax.broadcasted_iota(jnp.int32, (1, HEAD_DIM), 1)
    first_half = lane < ROPE_DIM // 2

    for c in range(tm // PROJ_ROWS):
        rows = slice(c * PROJ_ROWS, (c + 1) * PROJ_ROWS)
        y = _dot(h_ref[rows, :], w_ref[...])
        cos_t = jnp.where(is_v, 1.0, cos_ref[rows, :]) * scale
        sin_t = jnp.where(is_v, 0.0, sin_ref[rows, :]) * scale
        for hh in range(heads_per_tile):
            yh = y[:, hh * HEAD_DIM:(hh + 1) * HEAD_DIM]
            partner = jnp.where(first_half,
                                pltpu.roll(yh, HEAD_DIM - ROPE_DIM // 2, axis=1),
                                pltpu.roll(yh, ROPE_DIM // 2, axis=1))
            val = yh * cos_t + partner * sin_t
            if layout == "cols":
                for cc in range(PROJ_ROWS // LANES):
                    col0 = c * PROJ_ROWS + cc * LANES
                    o_ref[hh, :, col0:col0 + LANES] = val[cc * LANES:(cc + 1) * LANES, :].T.astype(o_ref.dtype)
            elif layout == "view":
                stage_ref, = scratch
                stage_ref[hh] = val
                nv = PROJ_ROWS // VIEW_RES
                for r in range(VIEW_RES):
                    o_ref[hh, c * nv:(c + 1) * nv, r * HEAD_DIM:(r + 1) * HEAD_DIM] = (
                        stage_ref[hh, pl.ds(r, nv, stride=VIEW_RES), :].astype(o_ref.dtype))
            else:
                o_ref[hh, rows, :] = val.astype(o_ref.dtype)


def project(h, w, cos_t, sin_t, src_tiles, out_dtype, batch, seq, layout="rows", tm=1024):
    m, d = h.shape
    n_tiles = len(src_tiles)
    heads_per_tile = PROJ_TILE // HEAD_DIM
    tm = min(tm, seq)
    pos_blocks = seq // tm

    def w_map(i, j):
        col = functools.reduce(lambda acc, nt: jnp.where(j == nt[0], nt[1], acc),
                               list(enumerate(src_tiles)), 0)
        return (0, col)

    scratch = []
    if layout == "cols":
        out_spec = pl.BlockSpec((heads_per_tile, None, HEAD_DIM, tm),
                                lambda i, j: (j, i // pos_blocks, 0, i % pos_blocks))
        out_shape = jax.ShapeDtypeStruct((n_tiles * heads_per_tile, batch, HEAD_DIM, seq), out_dtype)
    elif layout == "view":
        out_spec = pl.BlockSpec((heads_per_tile, None, tm // VIEW_RES, VIEW_RES * HEAD_DIM),
                                lambda i, j: (j, i // pos_blocks, i % pos_blocks, 0))
        out_shape = jax.ShapeDtypeStruct((n_tiles * heads_per_tile, batch, seq // VIEW_RES, VIEW_RES * HEAD_DIM),
                                         out_dtype)
        scratch = [pltpu.VMEM((heads_per_tile, PROJ_ROWS, HEAD_DIM), F32)]
    else:
        out_spec = pl.BlockSpec((heads_per_tile, tm, HEAD_DIM), lambda i, j: (j, i, 0))
        out_shape = jax.ShapeDtypeStruct((n_tiles * heads_per_tile, m, HEAD_DIM), out_dtype)

    return pl.pallas_call(
        functools.partial(_proj_kernel, src_tiles=src_tiles, layout=layout),
        grid=(m // tm, n_tiles),
        in_specs=[pl.BlockSpec((tm, d), lambda i, j: (i, 0)),
                  pl.BlockSpec((d, PROJ_TILE), w_map),
                  pl.BlockSpec((tm, HEAD_DIM), lambda i, j: (i % pos_blocks, 0)),
                  pl.BlockSpec((tm, HEAD_DIM), lambda i, j: (i % pos_blocks, 0))],
        out_specs=out_spec,
        out_shape=out_shape,
        scratch_shapes=scratch,
        compiler_params=_params("parallel", "arbitrary"),
        name="project_" + layout,
    )(h, w, cos_t, sin_t)


DIL_UNROLL = 8
DIL_RES = VIEW_RES


def _dilated_kernel(q_ref, k_ref, v_ref, o_ref, acc_ref, m_ref, l_ref, bias_ref, *, seq):
    for ci, (window, dil) in enumerate(DILATED_CONFIGS):
        band = window // dil
        n_pieces = DIL_RES // dil
        pr = BLK // n_pieces
        kr = 2 * pr
        tiles_per_res = seq // (BLK * dil)
        assert band <= BLK and pr % 8 == 0 and pr & (pr - 1) == 0

        q_row = lax.broadcasted_iota(jnp.int32, (BLK, 1), 0)
        k_row = lax.broadcasted_iota(jnp.int32, (1, 2 * BLK), 1)
        q_sub = n_pieces * (q_row & (pr - 1)) + (q_row >> (pr.bit_length() - 1))
        k_sub = n_pieces * (k_row & (kr - 1)) + (k_row >> (kr.bit_length() - 1))
        for lead in range(2):
            dist = (q_sub + lead * BLK) - k_sub
            bias_ref[2 * ci + lead] = jnp.where((dist >= 0) & (dist <= band), 0.0, NEG)

        def body(step, carry, ci=ci, dil=dil, band=band, n_pieces=n_pieces, pr=pr, kr=kr,
                 tiles_per_res=tiles_per_res):
            def pieces(ref, row0, nrows, r):
                return [ref[pl.ds(row0, nrows), pl.ds(pl.multiple_of((r + dil * a) * HEAD_DIM, HEAD_DIM), HEAD_DIM)]
                        for a in range(n_pieces)]

            tiles = []
            for u in range(DIL_UNROLL):
                idx = step * DIL_UNROLL + u
                r = idx // tiles_per_res
                i = idx - r * tiles_per_res
                kb = jnp.maximum(i - 1, 0)
                q_row0 = pl.multiple_of(i * pr, pr)
                k_row0 = pl.multiple_of(kb * pr, pr)
                m_old = None if ci == 0 else jnp.concatenate(pieces(m_ref, q_row0, pr, r), axis=0)
                tiles.append((r, i, kb, q_row0, k_row0, m_old))

            results = []
            for (r, i, kb, q_row0, k_row0, m_old) in tiles:
                q = jnp.concatenate(pieces(q_ref, q_row0, pr, r), axis=0).astype(BF16)
                k = jnp.concatenate(pieces(k_ref, k_row0, kr, r), axis=0).astype(BF16)
                v = jnp.concatenate(pieces(v_ref, k_row0, kr, r), axis=0).astype(BF16)
                s = _dot_nt(q, k)
                s = s + bias_ref[2 * ci + i - kb]
                m_tile = jnp.max(s, axis=1, keepdims=True)
                if ci == 0:
                    m_new = jnp.broadcast_to(m_tile, (BLK, LANES))
                else:
                    m_new = jnp.maximum(m_old, m_tile)
                p = jnp.exp2(s - jnp.concatenate([m_new, m_new], axis=1))
                l_tile = jnp.sum(p, axis=1, keepdims=True)
                pv = _dot(p.astype(BF16), v)
                results.append((r, q_row0, m_old, m_new, l_tile, pv))

            for (r, q_row0, m_old, m_new, l_tile, pv) in results:
                if ci == 0:
                    l_new = jnp.broadcast_to(l_tile, (BLK, LANES))
                    acc_new = pv
                else:
                    alpha = jnp.exp2(m_old - m_new)
                    l_new = alpha * jnp.concatenate(pieces(l_ref, q_row0, pr, r), axis=0) + l_tile
                    acc_new = alpha * jnp.concatenate(pieces(acc_ref, q_row0, pr, r), axis=0) + pv
                for a in range(n_pieces):
                    col = pl.ds(pl.multiple_of((r + dil * a) * HEAD_DIM, HEAD_DIM), HEAD_DIM)
                    rows = slice(a * pr, (a + 1) * pr)
                    m_ref[pl.ds(q_row0, pr), col] = m_new[rows]
                    l_ref[pl.ds(q_row0, pr), col] = l_new[rows]
                    acc_ref[pl.ds(q_row0, pr), col] = acc_new[rows]
            return carry

        lax.fori_loop(0, seq // (BLK * DIL_UNROLL), body, 0)
    o_ref[...] = acc_ref[...] / l_ref[...]


def dilated_attention(pf, batch, seq):
    assert all(DIL_RES % d == 0 for _, d in DILATED_CONFIGS)
    assert seq % (BLK * DIL_RES) == 0 and seq // DIL_RES >= 2 * BLK and (seq // BLK) % DIL_UNROLL == 0
    rows = seq // DIL_RES
    width = DIL_RES * HEAD_DIM
    view = pf

    def spec(off):
        return pl.BlockSpec((None, None, rows, width), lambda b, h: (off + h, b, 0, 0))

    out = pl.pallas_call(
        functools.partial(_dilated_kernel, seq=seq),
        grid=(batch, HEADS_A),
        in_specs=[spec(PF_QA), spec(PF_KA), spec(PF_VA)],
        out_specs=spec(0),
        out_shape=jax.ShapeDtypeStruct((HEADS_A, batch, rows, width), F32),
        scratch_shapes=[pltpu.VMEM((rows, width), F32)] * 3
        + [pltpu.VMEM((2 * len(DILATED_CONFIGS), BLK, 2 * BLK), F32)],
        compiler_params=_params("parallel", "parallel"),
        name="dilated_attention",
    )(view, view, view)
    return out.reshape(HEADS_A, batch * seq, HEAD_DIM)


def _compress_kernel(t_ref, pe_ref, w1_ref, w2_ref, o_ref, ot_ref, *, seq):
    n_chunks = seq // CMP_STRIDE
    first = jnp.zeros((n_chunks, HEAD_DIM), F32)
    second = jnp.zeros((n_chunks, HEAD_DIM), F32)
    for i in range(CMP_STRIDE):
        ti = t_ref[:, i * HEAD_DIM:(i + 1) * HEAD_DIM]
        first += _dot((ti + pe_ref[pl.ds(i, 1), :]).astype(BF16), w1_ref[i])
        second += _dot((ti + pe_ref[pl.ds(CMP_STRIDE + i, 1), :]).astype(BF16), w1_ref[CMP_STRIDE + i])
    pre = first + pltpu.roll(second, n_chunks - 1, axis=0)
    out = _dot(jax.nn.gelu(pre).astype(BF16), w2_ref[...])
    o_ref[...] = out.astype(o_ref.dtype)
    ot_ref[...] = out.T.astype(ot_ref.dtype)


def compress(pf, pe, w1, w2, batch, seq):
    assert CMP_LEN == 2 * CMP_STRIDE and CMP_STRIDE == VIEW_RES
    n_chunks = seq // CMP_STRIDE
    pf4 = pf
    bg = batch * KV_GROUPS_B
    return pl.pallas_call(
        functools.partial(_compress_kernel, seq=seq),
        grid=(2, batch, KV_GROUPS_B),
        in_specs=[pl.BlockSpec((None, None, n_chunks, VIEW_RES * HEAD_DIM),
                               lambda kv, b, g: (PF_KC + kv * KV_GROUPS_B + g, b, 0, 0)),
                  pl.BlockSpec((None, CMP_LEN, HEAD_DIM), lambda kv, b, g: (kv, 0, 0)),
                  pl.BlockSpec((None, CMP_LEN, HEAD_DIM, HEAD_DIM), lambda kv, b, g: (kv, 0, 0, 0)),
                  pl.BlockSpec((None, HEAD_DIM, HEAD_DIM), lambda kv, b, g: (kv, 0, 0))],
        out_specs=[pl.BlockSpec((None, None, n_chunks, HEAD_DIM), lambda kv, b, g: (kv, b * KV_GROUPS_B + g, 0, 0)),
                   pl.BlockSpec((None, None, HEAD_DIM, n_chunks), lambda kv, b, g: (kv, b * KV_GROUPS_B + g, 0, 0))],
        out_shape=[jax.ShapeDtypeStruct((2, bg, n_chunks, HEAD_DIM), BF16),
                   jax.ShapeDtypeStruct((2, bg, HEAD_DIM, n_chunks), BF16)],
        compiler_params=_params("parallel", "parallel", "parallel"),
        name="compress",
    )(pf4, pe, w1, w2)


NSA_TQ = 256
NSA_TK = 512


def _select_blocks(score):
    n_s, tq = score.shape
    groups = n_s // 8
    rows8 = [score[8 * v:8 * v + 8, :] for v in range(groups)]
    rank8 = [jnp.zeros((8, tq), F32) for _ in range(groups)]
    sub = lax.broadcasted_iota(jnp.int32, (8, 1), 0)
    for jp in range(n_s):
        vp, sp = divmod(jp, 8)
        row = jnp.broadcast_to(rows8[vp][sp:sp + 1, :], (8, tq))
        for v in range(groups):
            if v > vp:
                beats = jnp.where(row >= rows8[v], 1.0, 0.0)
            elif v < vp:
                beats = jnp.where(row > rows8[v], 1.0, 0.0)
            else:
                beats = jnp.where(sub > sp, jnp.where(row >= rows8[v], 1.0, 0.0),
                                  jnp.where(row > rows8[v], 1.0, 0.0))
            rank8[v] = rank8[v] + beats
    rank = jnp.concatenate(rank8, axis=0)
    return (rank < N_SELECT) & (score > -jnp.inf)


def _flash_block(s, vt, carry):
    m_i, l_i, acc = carry
    m_new = jnp.maximum(m_i, jnp.max(s, axis=0, keepdims=True))
    alpha = jnp.exp2(m_i - m_new)
    p = jnp.exp2(s - m_new)
    return m_new, alpha * l_i + jnp.sum(p, axis=0, keepdims=True), alpha * acc + _dot(vt, p.astype(BF16))


def _nsa_kernel(q_ref, kc_ref, vct_ref, ks_ref, vst_ref, kw_ref, vwt_ref, e_ref, gl_ref, o_ref, *, seq):
    tq, tk = NSA_TQ, NSA_TK
    cols = REP_B * tq
    n_cp = seq // CMP_STRIDE
    n_s = seq // SLC_BLOCK
    qi = pl.program_id(2)
    t0 = qi * tq
    q_t = jnp.concatenate([q_ref[r] for r in range(REP_B)], axis=1)
    lane = lax.broadcasted_iota(jnp.int32, (1, tq), 1)
    tpos = t0 + lane

    def per_head(x):
        return jnp.concatenate([x] * REP_B, axis=1)

    csub = lax.broadcasted_iota(jnp.int32, (n_cp, 1), 0)
    c_ok = (csub * CMP_STRIDE + (CMP_LEN - 1) <= tpos) & (csub < n_cp - 1)
    sc = _dot(kc_ref[...], q_t) + per_head(jnp.where(c_ok, 0.0, NEG))
    e = jnp.exp2(sc - jnp.max(sc, axis=0, keepdims=True))
    has_block = per_head(jnp.where(tpos >= CMP_LEN - 1, 1.0, 0.0))
    p_cmp = e * (has_block / jnp.maximum(jnp.sum(e, axis=0, keepdims=True), 1e-30))
    o_cmp = _dot(vct_ref[...], p_cmp.astype(BF16))

    ws = WIN + tq
    w0 = pl.multiple_of(jnp.maximum(t0 - WIN, 0), tq)
    dist = tpos - (w0 + lax.broadcasted_iota(jnp.int32, (ws, 1), 0))
    win_bias = per_head(jnp.where((dist >= 0) & (dist <= WIN - 1), 0.0, NEG))
    s_w = _dot(kw_ref[pl.ds(w0, ws), :], q_t) + win_bias
    p_w = jnp.exp2(s_w - jnp.max(s_w, axis=0, keepdims=True))
    o_win = _dot(vwt_ref[:, pl.ds(w0, ws)], p_w.astype(BF16)) * (1.0 / jnp.sum(p_w, axis=0, keepdims=True))

    p_sum = p_cmp[:, 0:tq]
    for r in range(1, REP_B):
        p_sum = p_sum + p_cmp[:, r * tq:(r + 1) * tq]
    jj = lax.broadcasted_iota(jnp.int32, (n_s, 1), 0)
    cidx = lax.broadcasted_iota(jnp.int32, (1, n_cp), 1)
    ratio = SLC_BLOCK // CMP_STRIDE
    c_first = ratio * jj - (CMP_LEN // CMP_STRIDE - 1)
    hits = jnp.where((cidx >= c_first) & (cidx < ratio * (jj + 1)), 1.0, 0.0).astype(BF16)
    p_hi = p_sum.astype(BF16)
    p_lo = (p_sum - p_hi.astype(F32)).astype(BF16)
    imp = _dot(hits, p_hi) + _dot(hits, p_lo)
    qblk = tpos >> (SLC_BLOCK.bit_length() - 1)
    forced = (jj == 0) | (jj == qblk) | (jj == qblk - 1)
    valid = jj * SLC_BLOCK <= tpos
    score = jnp.where(forced, FORCE_SCORE, jnp.where(valid, imp, -jnp.inf))
    sel = _select_blocks(score)
    bias = jnp.where(sel, 0.0, SEL_BIAS)
    if n_s < LANES:
        bias = jnp.concatenate([bias, jnp.full((LANES - n_s, tq), SEL_BIAS, F32)], axis=0)
    q_aug = jnp.concatenate([q_t, per_head(bias.astype(BF16))], axis=0)

    def slc_scores(kt):
        k0 = pl.multiple_of(kt * tk, tk)
        k_aug = jnp.concatenate([ks_ref[pl.ds(k0, tk), :], e_ref[pl.ds(k0, tk), :]], axis=1)
        return _dot(k_aug, q_aug)

    def slc_values(kt):
        return vst_ref[:, pl.ds(pl.multiple_of(kt * tk, tk), tk)]

    def slc_step(kt, stats):
        return _flash_block(slc_scores(kt), slc_values(kt), stats)

    last = (t0 + tq - 1) // tk
    stats = (jnp.full((1, cols), NEG, F32), jnp.zeros((1, cols), F32), jnp.zeros((HEAD_DIM, cols), F32))
    stats = lax.fori_loop(0, last, slc_step, stats)
    tok = last * tk + lax.broadcasted_iota(jnp.int32, (tk, 1), 0)
    s_last = slc_scores(last) + per_head(jnp.where(tok <= tpos, 0.0, NEG))
    _, l_s, acc_s = _flash_block(s_last, slc_values(last), stats)
    o_slc = acc_s * (1.0 / l_s)

    gates = jax.nn.sigmoid(gl_ref[...].T)
    for r in range(REP_B):
        sl = slice(r * tq, (r + 1) * tq)
        merged = (gates[3 * r:3 * r + 1, :] * o_cmp[:, sl]
                  + gates[3 * r + 1:3 * r + 2, :] * o_slc[:, sl]
                  + gates[3 * r + 2:3 * r + 3, :] * o_win[:, sl])
        o_ref[:, r * HEAD_DIM:(r + 1) * HEAD_DIM] = merged.T


def nsa_attention(pr, pc, kvc, kvct, gl, batch, seq):
    tq = NSA_TQ
    n_s = seq // SLC_BLOCK
    assert seq % NSA_TK == 0 and NSA_TK % tq == 0 and seq >= WIN + tq and n_s <= LANES and n_s % 8 == 0
    assert SLC_BLOCK & (SLC_BLOCK - 1) == 0
    n_cp = seq // CMP_STRIDE
    nq = seq // tq
    pr4 = pr.reshape(pr.shape[0], batch, seq, HEAD_DIM)
    block_onehot = (jnp.arange(seq)[:, None] // SLC_BLOCK == jnp.arange(LANES)[None, :]).astype(BF16)

    def k_spec(off):
        return pl.BlockSpec((None, None, seq, HEAD_DIM), lambda b, g, i: (off + g, b, 0, 0))

    def vt_spec(off):
        return pl.BlockSpec((None, None, HEAD_DIM, seq), lambda b, g, i: (off + g, b, 0, 0))

    return pl.pallas_call(
        functools.partial(_nsa_kernel, seq=seq),
        grid=(batch, KV_GROUPS_B, nq),
        in_specs=[pl.BlockSpec((REP_B, None, HEAD_DIM, tq), lambda b, g, i: (PC_QB // REP_B + g, b, 0, i)),
                  pl.BlockSpec((None, None, n_cp, HEAD_DIM), lambda b, g, i: (0, b * KV_GROUPS_B + g, 0, 0)),
                  pl.BlockSpec((None, None, HEAD_DIM, n_cp), lambda b, g, i: (1, b * KV_GROUPS_B + g, 0, 0)),
                  k_spec(PR_KS), vt_spec(PC_VS), k_spec(PR_KW), vt_spec(PC_VW),
                  pl.BlockSpec((seq, LANES), lambda b, g, i: (0, 0)),
                  pl.BlockSpec((tq, LANES), lambda b, g, i: (b * nq + i, g))],
        out_specs=pl.BlockSpec((tq, REP_B * HEAD_DIM), lambda b, g, i: (b * nq + i, g)),
        out_shape=jax.ShapeDtypeStruct((batch * seq, WIDTH_B), F32),
        compiler_params=_params("parallel", "parallel", "arbitrary"),
        name="nsa_attention",
    )(pc, kvc, kvct, pr4, pc, pr4, pc, block_onehot, gl)


def _mm_res_kernel(a_ref, w_ref, r_ref, o_ref):
    o_ref[...] = _dot(a_ref[...], w_ref[...]) + r_ref[...]


def matmul_residual(a, w, res, tm, tn):
    m, k = a.shape
    n = w.shape[1]
    return pl.pallas_call(
        _mm_res_kernel,
        grid=(m // tm, n // tn),
        in_specs=[pl.BlockSpec((tm, k), lambda i, j: (i, 0)),
                  pl.BlockSpec((k, tn), lambda i, j: (0, j)),
                  pl.BlockSpec((tm, tn), lambda i, j: (i, j))],
        out_specs=pl.BlockSpec((tm, tn), lambda i, j: (i, j)),
        out_shape=jax.ShapeDtypeStruct((m, n), F32),
        compiler_params=_params("parallel", "arbitrary"),
        name="matmul_residual",
    )(a, w, res)


def _ffn_up_kernel(h_ref, wg_ref, wu_ref, o_ref):
    h = h_ref[...]
    g = _dot(h, wg_ref[...])
    u = _dot(h, wu_ref[...])
    o_ref[...] = (g * jax.nn.sigmoid(g) * u).astype(o_ref.dtype)


def ffn_up(h, wg, wu, tm, tn):
    m, k = h.shape
    n = wg.shape[1]
    return pl.pallas_call(
        _ffn_up_kernel,
        grid=(m // tm, n // tn),
        in_specs=[pl.BlockSpec((tm, k), lambda i, j: (i, 0)),
                  pl.BlockSpec((k, tn), lambda i, j: (0, j)),
                  pl.BlockSpec((k, tn), lambda i, j: (0, j))],
        out_specs=pl.BlockSpec((tm, tn), lambda i, j: (i, j)),
        out_shape=jax.ShapeDtypeStruct((m, n), BF16),
        compiler_params=_params("parallel", "arbitrary"),
        name="ffn_up",
    )(h, wg, wu)


def _cast_kernel(w_ref, o_ref):
    o_ref[...] = w_ref[...].astype(o_ref.dtype)


def cast_columns(w, layer, n_cols, tk=512, tn=2816):
    _, k, _ = w.shape
    return pl.pallas_call(
        _cast_kernel,
        grid=(k // tk, n_cols // tn),
        in_specs=[pl.BlockSpec((None, tk, tn), lambda i, j: (layer, i, j))],
        out_specs=pl.BlockSpec((tk, tn), lambda i, j: (i, j)),
        out_shape=jax.ShapeDtypeStruct((k, n_cols), BF16),
        compiler_params=_params("parallel", "parallel"),
        name="cast_columns",
    )(w)


def _layer(x, norm_attn, w_in, layer, ck_pe, ck_w1, ck_w2, cv_pe, cv_w1, cv_w2,
           out_norm_a, out_norm_b, w_out, norm_ffn, w_gate, w_up, w_down, cos_t, sin_t, batch, seq):
    w_main = lax.slice(w_in, (layer, 0, 0), (layer + 1, D_MODEL, D_MAIN)).astype(BF16).reshape(D_MODEL, D_MAIN)
    w_gl = lax.slice(w_in, (layer, 0, D_MAIN), (layer + 1, D_MODEL, D_MAIN + N_GATES))
    w_gl = w_gl.reshape(D_MODEL, KV_GROUPS_B, REP_B * 3)
    w_gl = jnp.pad(w_gl, ((0, 0), (0, 0), (0, LANES - REP_B * 3))).reshape(D_MODEL, KV_GROUPS_B * LANES).astype(BF16)
    pe = jnp.stack([ck_pe, cv_pe])
    w1 = jnp.stack([ck_w1, cv_w1]).reshape(2, CMP_LEN, HEAD_DIM, HEAD_DIM).astype(BF16)
    w2 = jnp.stack([ck_w2, cv_w2]).astype(BF16)

    h, gl = rms_gate(x, norm_attn, w_gl)
    pf = project(h, w_main, cos_t, sin_t, F32_TILES, F32, batch, seq, layout="view")
    pr = project(h, w_main, cos_t, sin_t, ROW_TILES, BF16, batch, seq)
    pc = project(h, w_main, cos_t, sin_t, COL_TILES, BF16, batch, seq, layout="cols")
    o_a = dilated_attention(pf, batch, seq)
    kvc, kvct = compress(pf, pe, w1, w2, batch, seq)
    o_b = nsa_attention(pr, pc, kvc, kvct, gl, batch, seq)
    mixed = rms_pair(o_a, o_b, out_norm_a, out_norm_b)
    x1 = matmul_residual(mixed, w_out.astype(BF16), x, tm=1024, tn=512)
    h2 = rms(x1, norm_ffn, BF16)
    act = ffn_up(h2, w_gate.astype(BF16), w_up.astype(BF16), tm=1024, tn=256)
    x2 = matmul_residual(act, w_down.astype(BF16), x1, tm=512, tn=512)
    return x2


def kernel(x, norm_attn, w_in, ck_pe, ck_w1, ck_w2, cv_pe, cv_w1, cv_w2, out_norm_a, out_norm_b, w_out,
           norm_ffn, w_gate, w_up, w_down, norm_final):
    batch, seq, d = x.shape
    depth = w_in.shape[0]
    cos_t, sin_t = _rope_tables(seq)
    xf = x.reshape(batch * seq, d)
    for l in range(depth):
        xf = _layer(xf, norm_attn[l], w_in, l, ck_pe[l], ck_w1[l], ck_w2[l], cv_pe[l], cv_w1[l], cv_w2[l],
                    out_norm_a[l], out_norm_b[l], w_out[l], norm_ffn[l], w_gate[l], w_up[l], w_down[l],
                    cos_t, sin_t, batch, seq)
    return rms(xf, norm_final, F32).reshape(batch, seq, d)
```

```python
import functools
import math

import jax
import jax.numpy as jnp
from jax import lax
from jax.experimental import pallas as pl
from jax.experimental.pallas import tpu as pltpu

F32 = jnp.float32
BF16 = jnp.bfloat16

D_MODEL = 4096
HEAD_DIM = 128
HEADS_A = 16
HEADS_B = 16
KV_GROUPS_B = 4
REP_B = HEADS_B // KV_GROUPS_B
WIDTH_A = HEADS_A * HEAD_DIM
WIDTH_B = HEADS_B * HEAD_DIM
DILATED_CONFIGS = ((128, 1), (512, 4), (2048, 16))
BLK = 128
CMP_LEN = 32
CMP_STRIDE = 16
SLC_BLOCK = 64
N_SELECT = 16
WIN = 512
FORCE_SCORE = 1e6
ROPE_THETA = 500000.0
ROPE_DIM = HEAD_DIM // 4
EPS = 1e-5
Q_SCALE = HEAD_DIM ** -0.5 * math.log2(math.e)
N_GATES = 3 * HEADS_B
D_MAIN = 3 * WIDTH_A + WIDTH_B + 6 * KV_GROUPS_B * HEAD_DIM

VMEM_LIMIT_BYTES = 56 * 1024 * 1024
LANES = 128
NEG = -1e30
SEL_BIAS = -32768.0

PROJ_TILE = 4 * HEAD_DIM
F32_TILES = tuple(range(12)) + (16, 17)
ROW_TILES = (18, 20)
COL_TILES = (12, 13, 14, 15, 19, 21)
Q_TILES = (0, 1, 2, 3, 12, 13, 14, 15)
V_TILES = (8, 9, 10, 11, 17, 19, 21)
PF_QA, PF_KA, PF_VA, PF_KC, PF_VC = 0, 16, 32, 48, 52
PR_KS, PR_KW = 0, 4
PC_QB, PC_VS, PC_VW = 0, 16, 20


def _params(*sem):
    return pltpu.CompilerParams(dimension_semantics=sem, vmem_limit_bytes=VMEM_LIMIT_BYTES)


def _dot(a, b):
    return jnp.dot(a, b, preferred_element_type=F32)


def _dot_nt(a, b):
    return lax.dot_general(a, b, (((1,), (1,)), ((), ())), preferred_element_type=F32)


def _rms(x, g):
    return x * lax.rsqrt(jnp.mean(x * x, axis=-1, keepdims=True) + EPS) * g


def _rms_gate_kernel(x_ref, g_ref, wgl_ref, h_ref, gl_ref):
    hb = _rms(x_ref[...], g_ref[...]).astype(BF16)
    h_ref[...] = hb
    gl_ref[...] = _dot(hb, wgl_ref[...])


def rms_gate(x, gain, w_gl, tm=256):
    m, d = x.shape
    ng = w_gl.shape[1]
    return pl.pallas_call(
        _rms_gate_kernel,
        grid=(m // tm,),
        in_specs=[pl.BlockSpec((tm, d), lambda i: (i, 0)),
                  pl.BlockSpec((1, d), lambda i: (0, 0)),
                  pl.BlockSpec((d, ng), lambda i: (0, 0))],
        out_specs=[pl.BlockSpec((tm, d), lambda i: (i, 0)),
                   pl.BlockSpec((tm, ng), lambda i: (i, 0))],
        out_shape=[jax.ShapeDtypeStruct((m, d), BF16), jax.ShapeDtypeStruct((m, ng), F32)],
        compiler_params=_params("parallel"),
        name="rms_gate",
    )(x, gain.reshape(1, d), w_gl)


def _rms_kernel(x_ref, g_ref, o_ref):
    o_ref[...] = _rms(x_ref[...], g_ref[...]).astype(o_ref.dtype)


def rms(x, gain, out_dtype, tm=256):
    m, d = x.shape
    return pl.pallas_call(
        _rms_kernel,
        grid=(m // tm,),
        in_specs=[pl.BlockSpec((tm, d), lambda i: (i, 0)),
                  pl.BlockSpec((1, d), lambda i: (0, 0))],
        out_specs=pl.BlockSpec((tm, d), lambda i: (i, 0)),
        out_shape=jax.ShapeDtypeStruct((m, d), out_dtype),
        compiler_params=_params("parallel"),
        name="rms",
    )(x, gain.reshape(1, d))


def _rms_pair_kernel(a_ref, b_ref, ga_ref, gb_ref, o_ref):
    n_heads = a_ref.shape[0]
    wa = n_heads * HEAD_DIM
    ssq = jnp.sum(a_ref[0] * a_ref[0], axis=-1, keepdims=True)
    for hh in range(1, n_heads):
        ssq = ssq + jnp.sum(a_ref[hh] * a_ref[hh], axis=-1, keepdims=True)
    inv = lax.rsqrt(ssq * (1.0 / wa) + EPS)
    for hh in range(n_heads):
        cols = slice(hh * HEAD_DIM, (hh + 1) * HEAD_DIM)
        o_ref[:, cols] = (a_ref[hh] * inv * ga_ref[:, cols]).astype(o_ref.dtype)
    o_ref[:, wa:] = _rms(b_ref[...], gb_ref[...]).astype(o_ref.dtype)


def rms_pair(a, b, ga, gb, tm=256):
    n_heads, m, _ = a.shape
    wa = n_heads * HEAD_DIM
    wb = b.shape[1]
    return pl.pallas_call(
        _rms_pair_kernel,
        grid=(m // tm,),
        in_specs=[pl.BlockSpec((n_heads, tm, HEAD_DIM), lambda i: (0, i, 0)),
                  pl.BlockSpec((tm, wb), lambda i: (i, 0)),
                  pl.BlockSpec((1, wa), lambda i: (0, 0)),
                  pl.BlockSpec((1, wb), lambda i: (0, 0))],
        out_specs=pl.BlockSpec((tm, wa + wb), lambda i: (i, 0)),
        out_shape=jax.ShapeDtypeStruct((m, wa + wb), BF16),
        compiler_params=_params("parallel"),
        name="rms_pair",
    )(a, b, ga.reshape(1, wa), gb.reshape(1, wb))


def _rope_tables(seq):
    inv = ROPE_THETA ** (-jnp.arange(0, ROPE_DIM, 2, dtype=F32) / ROPE_DIM)
    ang = jnp.arange(seq, dtype=F32)[:, None] * inv[None, :]
    cos, sin = jnp.cos(ang), jnp.sin(ang)
    ones = jnp.ones((seq, HEAD_DIM - ROPE_DIM), F32)
    cos_t = jnp.concatenate([cos, cos, ones], axis=1)
    sin_t = jnp.concatenate([-sin, sin, 0.0 * ones], axis=1)
    return cos_t, sin_t


PROJ_ROWS = 256
VIEW_RES = DILATED_CONFIGS[-1][1]


def _proj_kernel(h_ref, w_ref, cos_ref, sin_ref, o_ref, *scratch, src_tiles, layout):
    jt = pl.program_id(1)
    is_v = functools.reduce(jnp.logical_or, [jt == n for n, t in enumerate(src_tiles) if t in V_TILES], False)
    is_q = functools.reduce(jnp.logical_or, [jt == n for n, t in enumerate(src_tiles) if t in Q_TILES], False)
    heads_per_tile = PROJ_TILE // HEAD_DIM
    tm = h_ref.shape[0]
    scale = jnp.where(is_q, Q_SCALE, 1.0).astype(F32)
    lane = lax.broadcasted_iota(jnp.int32, (1, HEAD_DIM), 1)
    first_half = lane < ROPE_DIM // 2

    for c in range(tm // PROJ_ROWS):
        rows = slice(c * PROJ_ROWS, (c + 1) * PROJ_ROWS)
        y = _dot(h_ref[rows, :], w_ref[...])
        cos_t = jnp.where(is_v, 1.0, cos_ref[rows, :]) * scale
        sin_t = jnp.where(is_v, 0.0, sin_ref[rows, :]) * scale
        for hh in range(heads_per_tile):
            yh = y[:, hh * HEAD_DIM:(hh + 1) * HEAD_DIM]
            partner = jnp.where(first_half,
                                pltpu.roll(yh, HEAD_DIM - ROPE_DIM // 2, axis=1),
                                pltpu.roll(yh, ROPE_DIM // 2, axis=1))
            val = yh * cos_t + partner * sin_t
            if layout == "cols":
                for cc in range(PROJ_ROWS // LANES):
                    col0 = c * PROJ_ROWS + cc * LANES
                    o_ref[hh, :, col0:col0 + LANES] = val[cc * LANES:(cc + 1) * LANES, :].T.astype(o_ref.dtype)
            elif layout == "view":
                stage_ref, = scratch
                stage_ref[hh] = val
                nv = PROJ_ROWS // VIEW_RES
                for r in range(VIEW_RES):
                    o_ref[hh, c * nv:(c + 1) * nv, r * HEAD_DIM:(r + 1) * HEAD_DIM] = (
                        stage_ref[hh, pl.ds(r, nv, stride=VIEW_RES), :].astype(o_ref.dtype))
            else:
                o_ref[hh, rows, :] = val.astype(o_ref.dtype)


def project(h, w, cos_t, sin_t, src_tiles, out_dtype, batch, seq, layout="rows", tm=1024):
    m, d = h.shape
    n_tiles = len(src_tiles)
    heads_per_tile = PROJ_TILE // HEAD_DIM
    tm = min(tm, seq)
    pos_blocks = seq // tm

    def w_map(i, j):
        col = functools.reduce(lambda acc, nt: jnp.where(j == nt[0], nt[1], acc),
                               list(enumerate(src_tiles)), 0)
        return (0, col)

    scratch = []
    if layout == "cols":
        out_spec = pl.BlockSpec((heads_per_tile, None, HEAD_DIM, tm),
                                lambda i, j: (j, i // pos_blocks, 0, i % pos_blocks))
        out_shape = jax.ShapeDtypeStruct((n_tiles * heads_per_tile, batch, HEAD_DIM, seq), out_dtype)
    elif layout == "view":
        out_spec = pl.BlockSpec((heads_per_tile, None, tm // VIEW_RES, VIEW_RES * HEAD_DIM),
                                lambda i, j: (j, i // pos_blocks, i % pos_blocks, 0))
        out_shape = jax.ShapeDtypeStruct((n_tiles * heads_per_tile, batch, seq // VIEW_RES, VIEW_RES * HEAD_DIM),
                                         out_dtype)
        scratch = [pltpu.VMEM((heads_per_tile, PROJ_ROWS, HEAD_DIM), F32)]
    else:
        out_spec = pl.BlockSpec((heads_per_tile, tm, HEAD_DIM), lambda i, j: (j, i, 0))
        out_shape = jax.ShapeDtypeStruct((n_tiles * heads_per_tile, m, HEAD_DIM), out_dtype)

    return pl.pallas_call(
        functools.partial(_proj_kernel, src_tiles=src_tiles, layout=layout),
        grid=(m // tm, n_tiles),
        in_specs=[pl.BlockSpec((tm, d), lambda i, j: (i, 0)),
                  pl.BlockSpec((d, PROJ_TILE), w_map),
                  pl.BlockSpec((tm, HEAD_DIM), lambda i, j: (i % pos_blocks, 0)),
                  pl.BlockSpec((tm, HEAD_DIM), lambda i, j: (i % pos_blocks, 0))],
        out_specs=out_spec,
        out_shape=out_shape,
        scratch_shapes=scratch,
        compiler_params=_params("parallel", "arbitrary"),
        name="project_" + layout,
    )(h, w, cos_t, sin_t)


DIL_UNROLL = 8
DIL_RES = VIEW_RES


def _dilated_kernel(q_ref, k_ref, v_ref, o_ref, acc_ref, m_ref, l_ref, bias_ref, *, seq):
    for ci, (window, dil) in enumerate(DILATED_CONFIGS):
        band = window // dil
        n_pieces = DIL_RES // dil
        pr = BLK // n_pieces
        kr = 2 * pr
        tiles_per_res = seq // (BLK * dil)
        assert band <= BLK and pr % 8 == 0 and pr & (pr - 1) == 0

        q_row = lax.broadcasted_iota(jnp.int32, (BLK, 1), 0)
        k_row = lax.broadcasted_iota(jnp.int32, (1, 2 * BLK), 1)
        q_sub = n_pieces * (q_row & (pr - 1)) + (q_row >> (pr.bit_length() - 1))
        k_sub = n_pieces * (k_row & (kr - 1)) + (k_row >> (kr.bit_length() - 1))
        for lead in range(2):
            dist = (q_sub + lead * BLK) - k_sub
            bias_ref[2 * ci + lead] = jnp.where((dist >= 0) & (dist <= band), 0.0, NEG)

        def body(step, carry, ci=ci, dil=dil, band=band, n_pieces=n_pieces, pr=pr, kr=kr,
                 tiles_per_res=tiles_per_res):
            def pieces(ref, row0, nrows, r):
                return [ref[pl.ds(row0, nrows), pl.ds(pl.multiple_of((r + dil * a) * HEAD_DIM, HEAD_DIM), HEAD_DIM)]
                        for a in range(n_pieces)]

            tiles = []
            for u in range(DIL_UNROLL):
                idx = step * DIL_UNROLL + u
                r = idx // tiles_per_res
                i = idx - r * tiles_per_res
                kb = jnp.maximum(i - 1, 0)
                q_row0 = pl.multiple_of(i * pr, pr)
                k_row0 = pl.multiple_of(kb * pr, pr)
                m_old = None if ci == 0 else jnp.concatenate(pieces(m_ref, q_row0, pr, r), axis=0)
                tiles.append((r, i, kb, q_row0, k_row0, m_old))

            results = []
            for (r, i, kb, q_row0, k_row0, m_old) in tiles:
                q = jnp.concatenate(pieces(q_ref, q_row0, pr, r), axis=0).astype(BF16)
                k = jnp.concatenate(pieces(k_ref, k_row0, kr, r), axis=0).astype(BF16)
                v = jnp.concatenate(pieces(v_ref, k_row0, kr, r), axis=0).astype(BF16)
                s = _dot_nt(q, k)
                s = s + bias_ref[2 * ci + i - kb]
                m_tile = jnp.max(s, axis=1, keepdims=True)
                if ci == 0:
                    m_new = jnp.broadcast_to(m_tile, (BLK, LANES))
                else:
                    m_new = jnp.maximum(m_old, m_tile)
                p = jnp.exp2(s - jnp.concatenate([m_new, m_new], axis=1))
                l_tile = jnp.sum(p, axis=1, keepdims=True)
                pv = _dot(p.astype(BF16), v)
                results.append((r, q_row0, m_old, m_new, l_tile, pv))

            for (r, q_row0, m_old, m_new, l_tile, pv) in results:
                if ci == 0:
                    l_new = jnp.broadcast_to(l_tile, (BLK, LANES))
                    acc_new = pv
                else:
                    alpha = jnp.exp2(m_old - m_new)
                    l_new = alpha * jnp.concatenate(pieces(l_ref, q_row0, pr, r), axis=0) + l_tile
                    acc_new = alpha * jnp.concatenate(pieces(acc_ref, q_row0, pr, r), axis=0) + pv
                for a in range(n_pieces):
                    col = pl.ds(pl.multiple_of((r + dil * a) * HEAD_DIM, HEAD_DIM), HEAD_DIM)
                    rows = slice(a * pr, (a + 1) * pr)
                    m_ref[pl.ds(q_row0, pr), col] = m_new[rows]
                    l_ref[pl.ds(q_row0, pr), col] = l_new[rows]
                    acc_ref[pl.ds(q_row0, pr), col] = acc_new[rows]
            return carry

        lax.fori_loop(0, seq // (BLK * DIL_UNROLL), body, 0)
    o_ref[...] = acc_ref[...] / l_ref[...]


def dilated_attention(pf, batch, seq):
    assert all(DIL_RES % d == 0 for _, d in DILATED_CONFIGS)
    assert seq % (BLK * DIL_RES) == 0 and seq // DIL_RES >= 2 * BLK and (seq // BLK) % DIL_UNROLL == 0
    rows = seq // DIL_RES
    width = DIL_RES * HEAD_DIM
    view = pf

    def spec(off):
        return pl.BlockSpec((None, None, rows, width), lambda b, h: (off + h, b, 0, 0))

    out = pl.pallas_call(
        functools.partial(_dilated_kernel, seq=seq),
        grid=(batch, HEADS_A),
        in_specs=[spec(PF_QA), spec(PF_KA), spec(PF_VA)],
        out_specs=spec(0),
        out_shape=jax.ShapeDtypeStruct((HEADS_A, batch, rows, width), F32),
        scratch_shapes=[pltpu.VMEM((rows, width), F32)] * 3
        + [pltpu.VMEM((2 * len(DILATED_CONFIGS), BLK, 2 * BLK), F32)],
        compiler_params=_params("parallel", "parallel"),
        name="dilated_attention",
    )(view, view, view)
    return out.reshape(HEADS_A, batch * seq, HEAD_DIM)


def _compress_kernel(t_ref, pe_ref, w1_ref, w2_ref, o_ref, ot_ref, *, seq):
    n_chunks = seq // CMP_STRIDE
    first = jnp.zeros((n_chunks, HEAD_DIM), F32)
    second = jnp.zeros((n_chunks, HEAD_DIM), F32)
    for i in range(CMP_STRIDE):
        ti = t_ref[:, i * HEAD_DIM:(i + 1) * HEAD_DIM]
        first += _dot((ti + pe_ref[pl.ds(i, 1), :]).astype(BF16), w1_ref[i])
        second += _dot((ti + pe_ref[pl.ds(CMP_STRIDE + i, 1), :]).astype(BF16), w1_ref[CMP_STRIDE + i])
    pre = first + pltpu.roll(second, n_chunks - 1, axis=0)
    out = _dot(jax.nn.gelu(pre).astype(BF16), w2_ref[...])
    o_ref[...] = out.astype(o_ref.dtype)
    ot_ref[...] = out.T.astype(ot_ref.dtype)


def compress(pf, pe, w1, w2, batch, seq):
    assert CMP_LEN == 2 * CMP_STRIDE and CMP_STRIDE == VIEW_RES
    n_chunks = seq // CMP_STRIDE
    pf4 = pf
    bg = batch * KV_GROUPS_B
    return pl.pallas_call(
        functools.partial(_compress_kernel, seq=seq),
        grid=(2, batch, KV_GROUPS_B),
        in_specs=[pl.BlockSpec((None, None, n_chunks, VIEW_RES * HEAD_DIM),
                               lambda kv, b, g: (PF_KC + kv * KV_GROUPS_B + g, b, 0, 0)),
                  pl.BlockSpec((None, CMP_LEN, HEAD_DIM), lambda kv, b, g: (kv, 0, 0)),
                  pl.BlockSpec((None, CMP_LEN, HEAD_DIM, HEAD_DIM), lambda kv, b, g: (kv, 0, 0, 0)),
                  pl.BlockSpec((None, HEAD_DIM, HEAD_DIM), lambda kv, b, g: (kv, 0, 0))],
        out_specs=[pl.BlockSpec((None, None, n_chunks, HEAD_DIM), lambda kv, b, g: (kv, b * KV_GROUPS_B + g, 0, 0)),
                   pl.BlockSpec((None, None, HEAD_DIM, n_chunks), lambda kv, b, g: (kv, b * KV_GROUPS_B + g, 0, 0))],
        out_shape=[jax.ShapeDtypeStruct((2, bg, n_chunks, HEAD_DIM), BF16),
                   jax.ShapeDtypeStruct((2, bg, HEAD_DIM, n_chunks), BF16)],
        compiler_params=_params("parallel", "parallel", "parallel"),
        name="compress",
    )(pf4, pe, w1, w2)


NSA_TQ = 512
NSA_TK = 512


def _select_blocks(score):
    n_s, tq = score.shape
    groups = n_s // 8
    rows8 = [score[8 * v:8 * v + 8, :] for v in range(groups)]
    rank8 = [jnp.zeros((8, tq), F32) for _ in range(groups)]
    sub = lax.broadcasted_iota(jnp.int32, (8, 1), 0)
    for jp in range(n_s):
        vp, sp = divmod(jp, 8)
        row = jnp.broadcast_to(rows8[vp][sp:sp + 1, :], (8, tq))
        for v in range(groups):
            if v > vp:
                beats = jnp.where(row >= rows8[v], 1.0, 0.0)
            elif v < vp:
                beats = jnp.where(row > rows8[v], 1.0, 0.0)
            else:
                beats = jnp.where(sub > sp, jnp.where(row >= rows8[v], 1.0, 0.0),
                                  jnp.where(row > rows8[v], 1.0, 0.0))
            rank8[v] = rank8[v] + beats
    rank = jnp.concatenate(rank8, axis=0)
    return (rank < N_SELECT) & (score > -jnp.inf)


def _flash_block(s, vt, carry):
    m_i, l_i, acc = carry
    m_new = jnp.maximum(m_i, jnp.max(s, axis=0, keepdims=True))
    alpha = jnp.exp2(m_i - m_new)
    p = jnp.exp2(s - m_new)
    return m_new, alpha * l_i + jnp.sum(p, axis=0, keepdims=True), alpha * acc + _dot(vt, p.astype(BF16))


def _nsa_kernel(q_ref, kc_ref, vct_ref, ks_ref, vst_ref, kw_ref, vwt_ref, e_ref, gl_ref, o_ref, *, seq):
    tq, tk = NSA_TQ, NSA_TK
    cols = REP_B * tq
    n_cp = seq // CMP_STRIDE
    n_s = seq // SLC_BLOCK
    qi = pl.program_id(2)
    t0 = qi * tq
    q_t = jnp.concatenate([q_ref[r] for r in range(REP_B)], axis=1)
    lane = lax.broadcasted_iota(jnp.int32, (1, tq), 1)
    tpos = t0 + lane

    def per_head(x):
        return jnp.concatenate([x] * REP_B, axis=1)

    csub = lax.broadcasted_iota(jnp.int32, (n_cp, 1), 0)
    c_ok = (csub * CMP_STRIDE + (CMP_LEN - 1) <= tpos) & (csub < n_cp - 1)
    sc = _dot(kc_ref[...], q_t) + per_head(jnp.where(c_ok, 0.0, NEG))
    e = jnp.exp2(sc - jnp.max(sc, axis=0, keepdims=True))
    has_block = per_head(jnp.where(tpos >= CMP_LEN - 1, 1.0, 0.0))
    p_cmp = e * (has_block / jnp.maximum(jnp.sum(e, axis=0, keepdims=True), 1e-30))
    o_cmp = _dot(vct_ref[...], p_cmp.astype(BF16))

    ws = WIN + tq
    w0 = pl.multiple_of(jnp.maximum(t0 - WIN, 0), tq)
    dist = tpos - (w0 + lax.broadcasted_iota(jnp.int32, (ws, 1), 0))
    win_bias = per_head(jnp.where((dist >= 0) & (dist <= WIN - 1), 0.0, NEG))
    s_w = _dot(kw_ref[pl.ds(w0, ws), :], q_t) + win_bias
    p_w = jnp.exp2(s_w - jnp.max(s_w, axis=0, keepdims=True))
    o_win = _dot(vwt_ref[:, pl.ds(w0, ws)], p_w.astype(BF16)) * (1.0 / jnp.sum(p_w, axis=0, keepdims=True))

    p_sum = p_cmp[:, 0:tq]
    for r in range(1, REP_B):
        p_sum = p_sum + p_cmp[:, r * tq:(r + 1) * tq]
    jj = lax.broadcasted_iota(jnp.int32, (n_s, 1), 0)
    cidx = lax.broadcasted_iota(jnp.int32, (1, n_cp), 1)
    ratio = SLC_BLOCK // CMP_STRIDE
    c_first = ratio * jj - (CMP_LEN // CMP_STRIDE - 1)
    hits = jnp.where((cidx >= c_first) & (cidx < ratio * (jj + 1)), 1.0, 0.0).astype(BF16)
    p_hi = p_sum.astype(BF16)
    p_lo = (p_sum - p_hi.astype(F32)).astype(BF16)
    imp = _dot(hits, p_hi) + _dot(hits, p_lo)
    qblk = tpos >> (SLC_BLOCK.bit_length() - 1)
    forced = (jj == 0) | (jj == qblk) | (jj == qblk - 1)
    valid = jj * SLC_BLOCK <= tpos
    score = jnp.where(forced, FORCE_SCORE, jnp.where(valid, imp, -jnp.inf))
    sel = _select_blocks(score)
    bias = jnp.where(sel, 0.0, SEL_BIAS)
    if n_s < LANES:
        bias = jnp.concatenate([bias, jnp.full((LANES - n_s, tq), SEL_BIAS, F32)], axis=0)
    q_aug = jnp.concatenate([q_t, per_head(bias.astype(BF16))], axis=0)

    def slc_scores(kt):
        k0 = pl.multiple_of(kt * tk, tk)
        k_aug = jnp.concatenate([ks_ref[pl.ds(k0, tk), :], e_ref[pl.ds(k0, tk), :]], axis=1)
        return _dot(k_aug, q_aug)

    def slc_values(kt):
        return vst_ref[:, pl.ds(pl.multiple_of(kt * tk, tk), tk)]

    def slc_step(kt, stats):
        return _flash_block(slc_scores(kt), slc_values(kt), stats)

    last = (t0 + tq - 1) // tk
    stats = (jnp.full((1, cols), NEG, F32), jnp.zeros((1, cols), F32), jnp.zeros((HEAD_DIM, cols), F32))
    stats = lax.fori_loop(0, last, slc_step, stats)
    tok = last * tk + lax.broadcasted_iota(jnp.int32, (tk, 1), 0)
    s_last = slc_scores(last) + per_head(jnp.where(tok <= tpos, 0.0, NEG))
    _, l_s, acc_s = _flash_block(s_last, slc_values(last), stats)
    o_slc = acc_s * (1.0 / l_s)

    gates = jax.nn.sigmoid(gl_ref[...].T)
    for r in range(REP_B):
        sl = slice(r * tq, (r + 1) * tq)
        merged = (gates[3 * r:3 * r + 1, :] * o_cmp[:, sl]
                  + gates[3 * r + 1:3 * r + 2, :] * o_slc[:, sl]
                  + gates[3 * r + 2:3 * r + 3, :] * o_win[:, sl])
        o_ref[:, r * HEAD_DIM:(r + 1) * HEAD_DIM] = merged.T


def nsa_attention(pr, pc, kvc, kvct, gl, batch, seq):
    tq = NSA_TQ
    n_s = seq // SLC_BLOCK
    assert seq % NSA_TK == 0 and NSA_TK % tq == 0 and seq >= WIN + tq and n_s <= LANES and n_s % 8 == 0
    assert SLC_BLOCK & (SLC_BLOCK - 1) == 0
    n_cp = seq // CMP_STRIDE
    nq = seq // tq
    pr4 = pr.reshape(pr.shape[0], batch, seq, HEAD_DIM)
    block_onehot = (jnp.arange(seq)[:, None] // SLC_BLOCK == jnp.arange(LANES)[None, :]).astype(BF16)

    def k_spec(off):
        return pl.BlockSpec((None, None, seq, HEAD_DIM), lambda b, g, i: (off + g, b, 0, 0))

    def vt_spec(off):
        return pl.BlockSpec((None, None, HEAD_DIM, seq), lambda b, g, i: (off + g, b, 0, 0))

    return pl.pallas_call(
        functools.partial(_nsa_kernel, seq=seq),
        grid=(batch, KV_GROUPS_B, nq),
        in_specs=[pl.BlockSpec((REP_B, None, HEAD_DIM, tq), lambda b, g, i: (PC_QB // REP_B + g, b, 0, i)),
                  pl.BlockSpec((None, None, n_cp, HEAD_DIM), lambda b, g, i: (0, b * KV_GROUPS_B + g, 0, 0)),
                  pl.BlockSpec((None, None, HEAD_DIM, n_cp), lambda b, g, i: (1, b * KV_GROUPS_B + g, 0, 0)),
                  k_spec(PR_KS), vt_spec(PC_VS), k_spec(PR_KW), vt_spec(PC_VW),
                  pl.BlockSpec((seq, LANES), lambda b, g, i: (0, 0)),
                  pl.BlockSpec((tq, LANES), lambda b, g, i: (b * nq + i, g))],
        out_specs=pl.BlockSpec((tq, REP_B * HEAD_DIM), lambda b, g, i: (b * nq + i, g)),
        out_shape=jax.ShapeDtypeStruct((batch * seq, WIDTH_B), F32),
        compiler_params=_params("parallel", "parallel", "arbitrary"),
        name="nsa_attention",
    )(pc, kvc, kvct, pr4, pc, pr4, pc, block_onehot, gl)


def _mm_res_kernel(a_ref, w_ref, r_ref, o_ref):
    o_ref[...] = _dot(a_ref[...], w_ref[...]) + r_ref[...]


def matmul_residual(a, w, res, tm, tn):
    m, k = a.shape
    n = w.shape[1]
    return pl.pallas_call(
        _mm_res_kernel,
        grid=(m // tm, n // tn),
        in_specs=[pl.BlockSpec((tm, k), lambda i, j: (i, 0)),
                  pl.BlockSpec((k, tn), lambda i, j: (0, j)),
                  pl.BlockSpec((tm, tn), lambda i, j: (i, j))],
        out_specs=pl.BlockSpec((tm, tn), lambda i, j: (i, j)),
        out_shape=jax.ShapeDtypeStruct((m, n), F32),
        compiler_params=_params("parallel", "arbitrary"),
        name="matmul_residual",
    )(a, w, res)


def _ffn_up_kernel(h_ref, wg_ref, wu_ref, o_ref):
    h = h_ref[...]
    g = _dot(h, wg_ref[...])
    u = _dot(h, wu_ref[...])
    o_ref[...] = (g * jax.nn.sigmoid(g) * u).astype(o_ref.dtype)


def ffn_up(h, wg, wu, tm, tn):
    m, k = h.shape
    n = wg.shape[1]
    return pl.pallas_call(
        _ffn_up_kernel,
        grid=(m // tm, n // tn),
        in_specs=[pl.BlockSpec((tm, k), lambda i, j: (i, 0)),
                  pl.BlockSpec((k, tn), lambda i, j: (0, j)),
                  pl.BlockSpec((k, tn), lambda i, j: (0, j))],
        out_specs=pl.BlockSpec((tm, tn), lambda i, j: (i, j)),
        out_shape=jax.ShapeDtypeStruct((m, n), BF16),
        compiler_params=_params("parallel", "arbitrary"),
        name="ffn_up",
    )(h, wg, wu)


def _cast_kernel(w_ref, o_ref):
    o_ref[...] = w_ref[...].astype(o_ref.dtype)


def cast_columns(w, layer, n_cols, tk=512, tn=2816):
    _, k, _ = w.shape
    return pl.pallas_call(
        _cast_kernel,
        grid=(k // tk, n_cols // tn),
        in_specs=[pl.BlockSpec((None, tk, tn), lambda i, j: (layer, i, j))],
        out_specs=pl.BlockSpec((tk, tn), lambda i, j: (i, j)),
        out_shape=jax.ShapeDtypeStruct((k, n_cols), BF16),
        compiler_params=_params("parallel", "parallel"),
        name="cast_columns",
    )(w)


def _layer(x, norm_attn, w_in, layer, ck_pe, ck_w1, ck_w2, cv_pe, cv_w1, cv_w2,
           out_norm_a, out_norm_b, w_out, norm_ffn, w_gate, w_up, w_down, cos_t, sin_t, batch, seq):
    w_main = lax.slice(w_in, (layer, 0, 0), (layer + 1, D_MODEL, D_MAIN)).astype(BF16).reshape(D_MODEL, D_MAIN)
    w_gl = lax.slice(w_in, (layer, 0, D_MAIN), (layer + 1, D_MODEL, D_MAIN + N_GATES))
    w_gl = w_gl.reshape(D_MODEL, KV_GROUPS_B, REP_B * 3)
    w_gl = jnp.pad(w_gl, ((0, 0), (0, 0), (0, LANES - REP_B * 3))).reshape(D_MODEL, KV_GROUPS_B * LANES).astype(BF16)
    pe = jnp.stack([ck_pe, cv_pe])
    w1 = jnp.stack([ck_w1, cv_w1]).reshape(2, CMP_LEN, HEAD_DIM, HEAD_DIM).astype(BF16)
    w2 = jnp.stack([ck_w2, cv_w2]).astype(BF16)

    h, gl = rms_gate(x, norm_attn, w_gl)
    pf = project(h, w_main, cos_t, sin_t, F32_TILES, F32, batch, seq, layout="view")
    pr = project(h, w_main, cos_t, sin_t, ROW_TILES, BF16, batch, seq)
    pc = project(h, w_main, cos_t, sin_t, COL_TILES, BF16, batch, seq, layout="cols")
    o_a = dilated_attention(pf, batch, seq)
    kvc, kvct = compress(pf, pe, w1, w2, batch, seq)
    o_b = nsa_attention(pr, pc, kvc, kvct, gl, batch, seq)
    mixed = rms_pair(o_a, o_b, out_norm_a, out_norm_b)
    x1 = matmul_residual(mixed, w_out.astype(BF16), x, tm=1024, tn=1024)
    h2 = rms(x1, norm_ffn, BF16)
    act = ffn_up(h2, w_gate.astype(BF16), w_up.astype(BF16), tm=2048, tn=256)
    x2 = matmul_residual(act, w_down.astype(BF16), x1, tm=512, tn=512)
    return x2


def kernel(x, norm_attn, w_in, ck_pe, ck_w1, ck_w2, cv_pe, cv_w1, cv_w2, out_norm_a, out_norm_b, w_out,
           norm_ffn, w_gate, w_up, w_down, norm_final):
    batch, seq, d = x.shape
    depth = w_in.shape[0]
    cos_t, sin_t = _rope_tables(seq)
    xf = x.reshape(batch * seq, d)
    for l in range(depth):
        xf = _layer(xf, norm_attn[l], w_in, l, ck_pe[l], ck_w1[l], ck_w2[l], cv_pe[l], cv_w1[l], cv_w2[l],
                    out_norm_a[l], out_norm_b[l], w_out[l], norm_ffn[l], w_gate[l], w_up[l], w_down[l],
                    cos_t, sin_t, batch, seq)
    return rms(xf, norm_final, F32).reshape(batch, seq, d)
```

```python
import functools
import math

import jax
import jax.numpy as jnp
from jax import lax
from jax.experimental import pallas as pl
from jax.experimental.pallas import tpu as pltpu

F32 = jnp.float32
BF16 = jnp.bfloat16

D_MODEL = 4096
HEAD_DIM = 128
HEADS_A = 16
HEADS_B = 16
KV_GROUPS_B = 4
REP_B = HEADS_B // KV_GROUPS_B
WIDTH_A = HEADS_A * HEAD_DIM
WIDTH_B = HEADS_B * HEAD_DIM
DILATED_CONFIGS = ((128, 1), (512, 4), (2048, 16))
BLK = 128
CMP_LEN = 32
CMP_STRIDE = 16
SLC_BLOCK = 64
N_SELECT = 16
WIN = 512
FORCE_SCORE = 1e6
ROPE_THETA = 500000.0
ROPE_DIM = HEAD_DIM // 4
EPS = 1e-5
Q_SCALE = HEAD_DIM ** -0.5 * math.log2(math.e)
N_GATES = 3 * HEADS_B
D_MAIN = 3 * WIDTH_A + WIDTH_B + 6 * KV_GROUPS_B * HEAD_DIM

VMEM_LIMIT_BYTES = 56 * 1024 * 1024
LANES = 128
NEG = -1e30
SEL_BIAS = -32768.0

PROJ_TILE = 4 * HEAD_DIM
F32_TILES = tuple(range(12)) + (16, 17)
ROW_TILES = (18, 20)
COL_TILES = (12, 13, 14, 15, 19, 21)
Q_TILES = (0, 1, 2, 3, 12, 13, 14, 15)
V_TILES = (8, 9, 10, 11, 17, 19, 21)
PF_QA, PF_KA, PF_VA, PF_KC, PF_VC = 0, 16, 32, 48, 52
PR_KS, PR_KW = 0, 4
PC_QB, PC_VS, PC_VW = 0, 16, 20


def _params(*sem):
    return pltpu.CompilerParams(dimension_semantics=sem, vmem_limit_bytes=VMEM_LIMIT_BYTES)


def _dot(a, b):
    return jnp.dot(a, b, preferred_element_type=F32)


def _dot_nt(a, b):
    return lax.dot_general(a, b, (((1,), (1,)), ((), ())), preferred_element_type=F32)


def _rms(x, g):
    return x * lax.rsqrt(jnp.mean(x * x, axis=-1, keepdims=True) + EPS) * g


def _rms_gate_kernel(x_ref, g_ref, wgl_ref, h_ref, gl_ref):
    hb = _rms(x_ref[...], g_ref[...]).astype(BF16)
    h_ref[...] = hb
    gl_ref[...] = _dot(hb, wgl_ref[...])


def rms_gate(x, gain, w_gl, tm=256):
    m, d = x.shape
    ng = w_gl.shape[1]
    return pl.pallas_call(
        _rms_gate_kernel,
        grid=(m // tm,),
        in_specs=[pl.BlockSpec((tm, d), lambda i: (i, 0)),
                  pl.BlockSpec((1, d), lambda i: (0, 0)),
                  pl.BlockSpec((d, ng), lambda i: (0, 0))],
        out_specs=[pl.BlockSpec((tm, d), lambda i: (i, 0)),
                   pl.BlockSpec((tm, ng), lambda i: (i, 0))],
        out_shape=[jax.ShapeDtypeStruct((m, d), BF16), jax.ShapeDtypeStruct((m, ng), F32)],
        compiler_params=_params("parallel"),
        name="rms_gate",
    )(x, gain.reshape(1, d), w_gl)


def _rms_kernel(x_ref, g_ref, o_ref):
    o_ref[...] = _rms(x_ref[...], g_ref[...]).astype(o_ref.dtype)


def rms(x, gain, out_dtype, tm=256):
    m, d = x.shape
    return pl.pallas_call(
        _rms_kernel,
        grid=(m // tm,),
        in_specs=[pl.BlockSpec((tm, d), lambda i: (i, 0)),
                  pl.BlockSpec((1, d), lambda i: (0, 0))],
        out_specs=pl.BlockSpec((tm, d), lambda i: (i, 0)),
        out_shape=jax.ShapeDtypeStruct((m, d), out_dtype),
        compiler_params=_params("parallel"),
        name="rms",
    )(x, gain.reshape(1, d))


def _rms_pair_kernel(a_ref, b_ref, ga_ref, gb_ref, o_ref, nat_ref):
    n_heads, nv, _ = a_ref.shape
    wa = n_heads * HEAD_DIM
    for r in range(VIEW_RES):
        lanes = slice(r * HEAD_DIM, (r + 1) * HEAD_DIM)
        sq = a_ref[0, :, lanes] * a_ref[0, :, lanes]
        for hh in range(1, n_heads):
            sq = sq + a_ref[hh, :, lanes] * a_ref[hh, :, lanes]
        inv = lax.rsqrt(jnp.sum(sq, axis=-1, keepdims=True) * (1.0 / wa) + EPS)
        for hh in range(n_heads):
            gain = ga_ref[:, hh * HEAD_DIM:(hh + 1) * HEAD_DIM]
            nat_ref[hh, pl.ds(r, nv, stride=VIEW_RES), :] = a_ref[hh, :, lanes] * inv * gain
    for hh in range(n_heads):
        o_ref[:, hh * HEAD_DIM:(hh + 1) * HEAD_DIM] = nat_ref[hh].astype(o_ref.dtype)
    o_ref[:, wa:] = _rms(b_ref[...], gb_ref[...]).astype(o_ref.dtype)


def rms_pair(a, b, ga, gb, tm=256):
    n_heads, batch, rows, width = a.shape
    assert width == VIEW_RES * HEAD_DIM and (rows * VIEW_RES) % tm == 0
    wa = n_heads * HEAD_DIM
    m, wb = b.shape
    tiles_per_seq = rows * VIEW_RES // tm
    return pl.pallas_call(
        _rms_pair_kernel,
        grid=(m // tm,),
        in_specs=[pl.BlockSpec((n_heads, None, tm // VIEW_RES, width),
                               lambda i: (0, i // tiles_per_seq, i % tiles_per_seq, 0)),
                  pl.BlockSpec((tm, wb), lambda i: (i, 0)),
                  pl.BlockSpec((1, wa), lambda i: (0, 0)),
                  pl.BlockSpec((1, wb), lambda i: (0, 0))],
        out_specs=pl.BlockSpec((tm, wa + wb), lambda i: (i, 0)),
        out_shape=jax.ShapeDtypeStruct((m, wa + wb), BF16),
        scratch_shapes=[pltpu.VMEM((n_heads, tm, HEAD_DIM), F32)],
        compiler_params=_params("parallel"),
        name="rms_pair",
    )(a, b, ga.reshape(1, wa), gb.reshape(1, wb))


def _rope_tables(seq):
    inv = ROPE_THETA ** (-jnp.arange(0, ROPE_DIM, 2, dtype=F32) / ROPE_DIM)
    ang = jnp.arange(seq, dtype=F32)[:, None] * inv[None, :]
    cos, sin = jnp.cos(ang), jnp.sin(ang)
    ones = jnp.ones((seq, HEAD_DIM - ROPE_DIM), F32)
    cos_t = jnp.concatenate([cos, cos, ones], axis=1)
    sin_t = jnp.concatenate([-sin, sin, 0.0 * ones], axis=1)
    return cos_t, sin_t


PROJ_ROWS = 256
VIEW_RES = DILATED_CONFIGS[-1][1]


def _proj_kernel(h_ref, w_ref, cos_ref, sin_ref, o_ref, *scratch, src_tiles, layout):
    jt = pl.program_id(1)
    is_v = functools.reduce(jnp.logical_or, [jt == n for n, t in enumerate(src_tiles) if t in V_TILES], False)
    is_q = functools.reduce(jnp.logical_or, [jt == n for n, t in enumerate(src_tiles) if t in Q_TILES], False)
    heads_per_tile = PROJ_TILE // HEAD_DIM
    tm = h_ref.shape[0]
    scale = jnp.where(is_q, Q_SCALE, 1.0).astype(F32)
    lane = lax.broadcasted_iota(jnp.int32, (1, HEAD_DIM), 1)
    first_half = lane < ROPE_DIM // 2

    for c in range(tm // PROJ_ROWS):
        rows = slice(c * PROJ_ROWS, (c + 1) * PROJ_ROWS)
        y = _dot(h_ref[rows, :], w_ref[...])
        cos_t = jnp.where(is_v, 1.0, cos_ref[rows, :]) * scale
        sin_t = jnp.where(is_v, 0.0, sin_ref[rows, :]) * scale
        for hh in range(heads_per_tile):
            yh = y[:, hh * HEAD_DIM:(hh + 1) * HEAD_DIM]
            partner = jnp.where(first_half,
                                pltpu.roll(yh, HEAD_DIM - ROPE_DIM // 2, axis=1),
                                pltpu.roll(yh, ROPE_DIM // 2, axis=1))
            val = yh * cos_t + partner * sin_t
            if layout == "cols":
                for cc in range(PROJ_ROWS // LANES):
                    col0 = c * PROJ_ROWS + cc * LANES
                    o_ref[hh, :, col0:col0 + LANES] = val[cc * LANES:(cc + 1) * LANES, :].T.astype(o_ref.dtype)
            elif layout == "view":
                stage_ref, = scratch
                stage_ref[hh] = val
                nv = PROJ_ROWS // VIEW_RES
                for r in range(VIEW_RES):
                    o_ref[hh, c * nv:(c + 1) * nv, r * HEAD_DIM:(r + 1) * HEAD_DIM] = (
                        stage_ref[hh, pl.ds(r, nv, stride=VIEW_RES), :].astype(o_ref.dtype))
            else:
                o_ref[hh, rows, :] = val.astype(o_ref.dtype)


def project(h, w, layer, cos_t, sin_t, src_tiles, out_dtype, batch, seq, layout="rows", tm=1024):
    m, d = h.shape
    n_tiles = len(src_tiles)
    heads_per_tile = PROJ_TILE // HEAD_DIM
    tm = min(tm, seq)
    pos_blocks = seq // tm

    def w_map(i, j):
        col = functools.reduce(lambda acc, nt: jnp.where(j == nt[0], nt[1], acc),
                               list(enumerate(src_tiles)), 0)
        return (layer, 0, col)

    scratch = []
    if layout == "cols":
        out_spec = pl.BlockSpec((heads_per_tile, None, HEAD_DIM, tm),
                                lambda i, j: (j, i // pos_blocks, 0, i % pos_blocks))
        out_shape = jax.ShapeDtypeStruct((n_tiles * heads_per_tile, batch, HEAD_DIM, seq), out_dtype)
    elif layout == "view":
        out_spec = pl.BlockSpec((heads_per_tile, None, tm // VIEW_RES, VIEW_RES * HEAD_DIM),
                                lambda i, j: (j, i // pos_blocks, i % pos_blocks, 0))
        out_shape = jax.ShapeDtypeStruct((n_tiles * heads_per_tile, batch, seq // VIEW_RES, VIEW_RES * HEAD_DIM),
                                         out_dtype)
        scratch = [pltpu.VMEM((heads_per_tile, PROJ_ROWS, HEAD_DIM), F32)]
    else:
        out_spec = pl.BlockSpec((heads_per_tile, tm, HEAD_DIM), lambda i, j: (j, i, 0))
        out_shape = jax.ShapeDtypeStruct((n_tiles * heads_per_tile, m, HEAD_DIM), out_dtype)

    return pl.pallas_call(
        functools.partial(_proj_kernel, src_tiles=src_tiles, layout=layout),
        grid=(m // tm, n_tiles),
        in_specs=[pl.BlockSpec((tm, d), lambda i, j: (i, 0)),
                  pl.BlockSpec((None, d, PROJ_TILE), w_map),
                  pl.BlockSpec((tm, HEAD_DIM), lambda i, j: (i % pos_blocks, 0)),
                  pl.BlockSpec((tm, HEAD_DIM), lambda i, j: (i % pos_blocks, 0))],
        out_specs=out_spec,
        out_shape=out_shape,
        scratch_shapes=scratch,
        compiler_params=_params("parallel", "arbitrary"),
        name="project_" + layout,
    )(h, w, cos_t, sin_t)


DIL_UNROLL = 8
DIL_RES = VIEW_RES


def _dilated_kernel(q_ref, k_ref, v_ref, o_ref, acc_ref, m_ref, l_ref, bias_ref, *, seq):
    for ci, (window, dil) in enumerate(DILATED_CONFIGS):
        band = window // dil
        n_pieces = DIL_RES // dil
        pr = BLK // n_pieces
        kr = 2 * pr
        tiles_per_res = seq // (BLK * dil)
        assert band <= BLK and pr % 8 == 0 and pr & (pr - 1) == 0

        q_row = lax.broadcasted_iota(jnp.int32, (BLK, 1), 0)
        k_row = lax.broadcasted_iota(jnp.int32, (1, 2 * BLK), 1)
        q_sub = n_pieces * (q_row & (pr - 1)) + (q_row >> (pr.bit_length() - 1))
        k_sub = n_pieces * (k_row & (kr - 1)) + (k_row >> (kr.bit_length() - 1))
        for lead in range(2):
            dist = (q_sub + lead * BLK) - k_sub
            bias_ref[2 * ci + lead] = jnp.where((dist >= 0) & (dist <= band), 0.0, NEG)

        def body(step, carry, ci=ci, dil=dil, band=band, n_pieces=n_pieces, pr=pr, kr=kr,
                 tiles_per_res=tiles_per_res):
            def pieces(ref, row0, nrows, r):
                return [ref[pl.ds(row0, nrows), pl.ds(pl.multiple_of((r + dil * a) * HEAD_DIM, HEAD_DIM), HEAD_DIM)]
                        for a in range(n_pieces)]

            tiles = []
            for u in range(DIL_UNROLL):
                idx = step * DIL_UNROLL + u
                r = idx // tiles_per_res
                i = idx - r * tiles_per_res
                kb = jnp.maximum(i - 1, 0)
                q_row0 = pl.multiple_of(i * pr, pr)
                k_row0 = pl.multiple_of(kb * pr, pr)
                m_old = None if ci == 0 else jnp.concatenate(pieces(m_ref, q_row0, pr, r), axis=0)
                tiles.append((r, i, kb, q_row0, k_row0, m_old))

            def update(r, q_row0, m_old, m_new, l_tile, pv):
                if ci == 0:
                    l_new = jnp.broadcast_to(l_tile, (BLK, LANES))
                    acc_new = pv
                else:
                    alpha = jnp.exp2(m_old - m_new)
                    l_new = alpha * jnp.concatenate(pieces(l_ref, q_row0, pr, r), axis=0) + l_tile
                    acc_new = alpha * jnp.concatenate(pieces(acc_ref, q_row0, pr, r), axis=0) + pv
                for a in range(n_pieces):
                    col = pl.ds(pl.multiple_of((r + dil * a) * HEAD_DIM, HEAD_DIM), HEAD_DIM)
                    rows = slice(a * pr, (a + 1) * pr)
                    m_ref[pl.ds(q_row0, pr), col] = m_new[rows]
                    l_ref[pl.ds(q_row0, pr), col] = l_new[rows]
                    acc_ref[pl.ds(q_row0, pr), col] = acc_new[rows]

            scores, maxes, probs, results = {}, {}, {}, []
            for t in range(DIL_UNROLL + 4):
                if t < DIL_UNROLL:
                    (r, i, kb, q_row0, k_row0, m_old) = tiles[t]
                    q = jnp.concatenate(pieces(q_ref, q_row0, pr, r), axis=0).astype(BF16)
                    k = jnp.concatenate(pieces(k_ref, k_row0, kr, r), axis=0).astype(BF16)
                    scores[t] = _dot_nt(q, k) + bias_ref[2 * ci + i - kb]
                if 0 <= t - 1 < DIL_UNROLL:
                    m_tile = jnp.max(scores[t - 1], axis=1, keepdims=True)
                    m_old = tiles[t - 1][5]
                    maxes[t - 1] = jnp.broadcast_to(m_tile, (BLK, LANES)) if ci == 0 else jnp.maximum(m_old, m_tile)
                if 0 <= t - 2 < DIL_UNROLL:
                    m_new = maxes[t - 2]
                    probs[t - 2] = jnp.exp2(scores.pop(t - 2) - jnp.concatenate([m_new, m_new], axis=1))
                if 0 <= t - 3 < DIL_UNROLL:
                    (r, i, kb, q_row0, k_row0, m_old) = tiles[t - 3]
                    p = probs.pop(t - 3)
                    v = jnp.concatenate(pieces(v_ref, k_row0, kr, r), axis=0).astype(BF16)
                    l_tile = jnp.sum(p, axis=1, keepdims=True)
                    results.append((r, q_row0, m_old, maxes.pop(t - 3), l_tile, _dot(p.astype(BF16), v)))
                if 0 <= t - 4 < DIL_UNROLL:
                    update(*results[t - 4])
            return carry

        lax.fori_loop(0, seq // (BLK * DIL_UNROLL), body, 0)
    o_ref[...] = acc_ref[...] / l_ref[...]


def dilated_attention(pf, batch, seq):
    assert all(DIL_RES % d == 0 for _, d in DILATED_CONFIGS)
    assert seq % (BLK * DIL_RES) == 0 and seq // DIL_RES >= 2 * BLK and (seq // BLK) % DIL_UNROLL == 0
    rows = seq // DIL_RES
    width = DIL_RES * HEAD_DIM
    view = pf

    def spec(off):
        return pl.BlockSpec((None, None, rows, width), lambda b, h: (off + h, b, 0, 0))

    out = pl.pallas_call(
        functools.partial(_dilated_kernel, seq=seq),
        grid=(batch, HEADS_A),
        in_specs=[spec(PF_QA), spec(PF_KA), spec(PF_VA)],
        out_specs=spec(0),
        out_shape=jax.ShapeDtypeStruct((HEADS_A, batch, rows, width), F32),
        scratch_shapes=[pltpu.VMEM((rows, width), F32)] * 3
        + [pltpu.VMEM((2 * len(DILATED_CONFIGS), BLK, 2 * BLK), F32)],
        compiler_params=_params("parallel", "parallel"),
        name="dilated_attention",
    )(view, view, view)
    return out


def _compress_kernel(t_ref, pe_ref, w1_ref, w2_ref, o_ref, ot_ref, *, seq):
    n_chunks = seq // CMP_STRIDE
    first = jnp.zeros((n_chunks, HEAD_DIM), F32)
    second = jnp.zeros((n_chunks, HEAD_DIM), F32)
    for i in range(CMP_STRIDE):
        ti = t_ref[:, i * HEAD_DIM:(i + 1) * HEAD_DIM]
        first += _dot((ti + pe_ref[pl.ds(i, 1), :]).astype(BF16), w1_ref[i])
        second += _dot((ti + pe_ref[pl.ds(CMP_STRIDE + i, 1), :]).astype(BF16), w1_ref[CMP_STRIDE + i])
    pre = first + pltpu.roll(second, n_chunks - 1, axis=0)
    out = _dot(jax.nn.gelu(pre).astype(BF16), w2_ref[...])
    o_ref[...] = out.astype(o_ref.dtype)
    ot_ref[...] = out.T.astype(ot_ref.dtype)


def compress(pf, pe, w1, w2, batch, seq):
    assert CMP_LEN == 2 * CMP_STRIDE and CMP_STRIDE == VIEW_RES
    n_chunks = seq // CMP_STRIDE
    pf4 = pf
    bg = batch * KV_GROUPS_B
    return pl.pallas_call(
        functools.partial(_compress_kernel, seq=seq),
        grid=(2, batch, KV_GROUPS_B),
        in_specs=[pl.BlockSpec((None, None, n_chunks, VIEW_RES * HEAD_DIM),
                               lambda kv, b, g: (PF_KC + kv * KV_GROUPS_B + g, b, 0, 0)),
                  pl.BlockSpec((None, CMP_LEN, HEAD_DIM), lambda kv, b, g: (kv, 0, 0)),
                  pl.BlockSpec((None, CMP_LEN, HEAD_DIM, HEAD_DIM), lambda kv, b, g: (kv, 0, 0, 0)),
                  pl.BlockSpec((None, HEAD_DIM, HEAD_DIM), lambda kv, b, g: (kv, 0, 0))],
        out_specs=[pl.BlockSpec((None, None, n_chunks, HEAD_DIM), lambda kv, b, g: (kv, b * KV_GROUPS_B + g, 0, 0)),
                   pl.BlockSpec((None, None, HEAD_DIM, n_chunks), lambda kv, b, g: (kv, b * KV_GROUPS_B + g, 0, 0))],
        out_shape=[jax.ShapeDtypeStruct((2, bg, n_chunks, HEAD_DIM), BF16),
                   jax.ShapeDtypeStruct((2, bg, HEAD_DIM, n_chunks), BF16)],
        compiler_params=_params("parallel", "parallel", "parallel"),
        name="compress",
    )(pf4, pe, w1, w2)


NSA_TQ = 512
NSA_TK = 512


def _select_blocks(score):
    n_s, tq = score.shape
    groups = n_s // 8
    rows8 = [score[8 * v:8 * v + 8, :] for v in range(groups)]
    rank8 = [jnp.zeros((8, tq), F32) for _ in range(groups)]
    sub = lax.broadcasted_iota(jnp.int32, (8, 1), 0)
    for jp in range(n_s):
        vp, sp = divmod(jp, 8)
        row = jnp.broadcast_to(rows8[vp][sp:sp + 1, :], (8, tq))
        for v in range(groups):
            if v > vp:
                beats = jnp.where(row >= rows8[v], 1.0, 0.0)
            elif v < vp:
                beats = jnp.where(row > rows8[v], 1.0, 0.0)
            else:
                beats = jnp.where(sub > sp, jnp.where(row >= rows8[v], 1.0, 0.0),
                                  jnp.where(row > rows8[v], 1.0, 0.0))
            rank8[v] = rank8[v] + beats
    rank = jnp.concatenate(rank8, axis=0)
    return (rank < N_SELECT) & (score > -jnp.inf)


def _flash_block(s, vt, carry):
    m_i, l_i, acc = carry
    m_new = jnp.maximum(m_i, jnp.max(s, axis=0, keepdims=True))
    alpha = jnp.exp2(m_i - m_new)
    p = jnp.exp2(s - m_new)
    return m_new, alpha * l_i + jnp.sum(p, axis=0, keepdims=True), alpha * acc + _dot(vt, p.astype(BF16))


def _nsa_kernel(q_ref, kc_ref, vct_ref, ks_ref, vst_ref, kw_ref, vwt_ref, e_ref, gl_ref, o_ref, *, seq):
    tq, tk = NSA_TQ, NSA_TK
    cols = REP_B * tq
    n_cp = seq // CMP_STRIDE
    n_s = seq // SLC_BLOCK
    qi = pl.program_id(2)
    t0 = qi * tq
    q_t = jnp.concatenate([q_ref[r] for r in range(REP_B)], axis=1)
    lane = lax.broadcasted_iota(jnp.int32, (1, tq), 1)
    tpos = t0 + lane

    def per_head(x):
        return jnp.concatenate([x] * REP_B, axis=1)

    csub = lax.broadcasted_iota(jnp.int32, (n_cp, 1), 0)
    c_ok = (csub * CMP_STRIDE + (CMP_LEN - 1) <= tpos) & (csub < n_cp - 1)
    sc = _dot(kc_ref[...], q_t) + per_head(jnp.where(c_ok, 0.0, NEG))
    e = jnp.exp2(sc - jnp.max(sc, axis=0, keepdims=True))
    has_block = per_head(jnp.where(tpos >= CMP_LEN - 1, 1.0, 0.0))
    p_cmp = e * (has_block / jnp.maximum(jnp.sum(e, axis=0, keepdims=True), 1e-30))
    o_cmp = _dot(vct_ref[...], p_cmp.astype(BF16))

    ws = WIN + tq
    w0 = pl.multiple_of(jnp.maximum(t0 - WIN, 0), tq)
    dist = tpos - (w0 + lax.broadcasted_iota(jnp.int32, (ws, 1), 0))
    win_bias = per_head(jnp.where((dist >= 0) & (dist <= WIN - 1), 0.0, NEG))
    s_w = _dot(kw_ref[pl.ds(w0, ws), :], q_t) + win_bias
    p_w = jnp.exp2(s_w - jnp.max(s_w, axis=0, keepdims=True))
    o_win = _dot(vwt_ref[:, pl.ds(w0, ws)], p_w.astype(BF16)) * (1.0 / jnp.sum(p_w, axis=0, keepdims=True))

    p_sum = p_cmp[:, 0:tq]
    for r in range(1, REP_B):
        p_sum = p_sum + p_cmp[:, r * tq:(r + 1) * tq]
    jj = lax.broadcasted_iota(jnp.int32, (n_s, 1), 0)
    cidx = lax.broadcasted_iota(jnp.int32, (1, n_cp), 1)
    ratio = SLC_BLOCK // CMP_STRIDE
    c_first = ratio * jj - (CMP_LEN // CMP_STRIDE - 1)
    hits = jnp.where((cidx >= c_first) & (cidx < ratio * (jj + 1)), 1.0, 0.0).astype(BF16)
    p_hi = p_sum.astype(BF16)
    p_lo = (p_sum - p_hi.astype(F32)).astype(BF16)
    imp = _dot(hits, p_hi) + _dot(hits, p_lo)
    qblk = tpos >> (SLC_BLOCK.bit_length() - 1)
    forced = (jj == 0) | (jj == qblk) | (jj == qblk - 1)
    valid = jj * SLC_BLOCK <= tpos
    score = jnp.where(forced, FORCE_SCORE, jnp.where(valid, imp, -jnp.inf))
    sel = _select_blocks(score)
    bias = jnp.where(sel, 0.0, SEL_BIAS)
    if n_s < LANES:
        bias = jnp.concatenate([bias, jnp.full((LANES - n_s, tq), SEL_BIAS, F32)], axis=0)
    q_aug = jnp.concatenate([q_t, per_head(bias.astype(BF16))], axis=0)

    def slc_scores(kt):
        k0 = pl.multiple_of(kt * tk, tk)
        k_aug = jnp.concatenate([ks_ref[pl.ds(k0, tk), :], e_ref[pl.ds(k0, tk), :]], axis=1)
        return _dot(k_aug, q_aug)

    def slc_values(kt):
        return vst_ref[:, pl.ds(pl.multiple_of(kt * tk, tk), tk)]

    def slc_step(kt, stats):
        return _flash_block(slc_scores(kt), slc_values(kt), stats)

    last = (t0 + tq - 1) // tk
    stats = (jnp.full((1, cols), NEG, F32), jnp.zeros((1, cols), F32), jnp.zeros((HEAD_DIM, cols), F32))
    stats = lax.fori_loop(0, last, slc_step, stats)
    tok = last * tk + lax.broadcasted_iota(jnp.int32, (tk, 1), 0)
    s_last = slc_scores(last) + per_head(jnp.where(tok <= tpos, 0.0, NEG))
    _, l_s, acc_s = _flash_block(s_last, slc_values(last), stats)
    o_slc = acc_s * (1.0 / l_s)

    gates = jax.nn.sigmoid(gl_ref[...].T)
    for r in range(REP_B):
        sl = slice(r * tq, (r + 1) * tq)
        merged = (gates[3 * r:3 * r + 1, :] * o_cmp[:, sl]
                  + gates[3 * r + 1:3 * r + 2, :] * o_slc[:, sl]
                  + gates[3 * r + 2:3 * r + 3, :] * o_win[:, sl])
        o_ref[:, r * HEAD_DIM:(r + 1) * HEAD_DIM] = merged.T


def nsa_attention(pr, pc, kvc, kvct, gl, batch, seq):
    tq = NSA_TQ
    n_s = seq // SLC_BLOCK
    assert seq % NSA_TK == 0 and NSA_TK % tq == 0 and seq >= WIN + tq and n_s <= LANES and n_s % 8 == 0
    assert SLC_BLOCK & (SLC_BLOCK - 1) == 0
    n_cp = seq // CMP_STRIDE
    nq = seq // tq
    pr4 = pr.reshape(pr.shape[0], batch, seq, HEAD_DIM)
    block_onehot = (jnp.arange(seq)[:, None] // SLC_BLOCK == jnp.arange(LANES)[None, :]).astype(BF16)

    def k_spec(off):
        return pl.BlockSpec((None, None, seq, HEAD_DIM), lambda b, g, i: (off + g, b, 0, 0))

    def vt_spec(off):
        return pl.BlockSpec((None, None, HEAD_DIM, seq), lambda b, g, i: (off + g, b, 0, 0))

    return pl.pallas_call(
        functools.partial(_nsa_kernel, seq=seq),
        grid=(batch, KV_GROUPS_B, nq),
        in_specs=[pl.BlockSpec((REP_B, None, HEAD_DIM, tq), lambda b, g, i: (PC_QB // REP_B + g, b, 0, i)),
                  pl.BlockSpec((None, None, n_cp, HEAD_DIM), lambda b, g, i: (0, b * KV_GROUPS_B + g, 0, 0)),
                  pl.BlockSpec((None, None, HEAD_DIM, n_cp), lambda b, g, i: (1, b * KV_GROUPS_B + g, 0, 0)),
                  k_spec(PR_KS), vt_spec(PC_VS), k_spec(PR_KW), vt_spec(PC_VW),
                  pl.BlockSpec((seq, LANES), lambda b, g, i: (0, 0)),
                  pl.BlockSpec((tq, LANES), lambda b, g, i: (b * nq + i, g))],
        out_specs=pl.BlockSpec((tq, REP_B * HEAD_DIM), lambda b, g, i: (b * nq + i, g)),
        out_shape=jax.ShapeDtypeStruct((batch * seq, WIDTH_B), F32),
        compiler_params=_params("parallel", "parallel", "arbitrary"),
        name="nsa_attention",
    )(pc, kvc, kvct, pr4, pc, pr4, pc, block_onehot, gl)


def _mm_res_kernel(a_ref, w_ref, r_ref, o_ref):
    o_ref[...] = _dot(a_ref[...], w_ref[...]) + r_ref[...]


def matmul_residual(a, w, res, tm, tn):
    m, k = a.shape
    n = w.shape[1]
    return pl.pallas_call(
        _mm_res_kernel,
        grid=(m // tm, n // tn),
        in_specs=[pl.BlockSpec((tm, k), lambda i, j: (i, 0)),
                  pl.BlockSpec((k, tn), lambda i, j: (0, j)),
                  pl.BlockSpec((tm, tn), lambda i, j: (i, j))],
        out_specs=pl.BlockSpec((tm, tn), lambda i, j: (i, j)),
        out_shape=jax.ShapeDtypeStruct((m, n), F32),
        compiler_params=_params("parallel", "arbitrary"),
        name="matmul_residual",
    )(a, w, res)


def _ffn_up_kernel(h_ref, wg_ref, wu_ref, o_ref):
    h = h_ref[...]
    g = _dot(h, wg_ref[...])
    u = _dot(h, wu_ref[...])
    o_ref[...] = (g * jax.nn.sigmoid(g) * u).astype(o_ref.dtype)


def ffn_up(h, wg, wu, tm, tn):
    m, k = h.shape
    n = wg.shape[1]
    return pl.pallas_call(
        _ffn_up_kernel,
        grid=(m // tm, n // tn),
        in_specs=[pl.BlockSpec((tm, k), lambda i, j: (i, 0)),
                  pl.BlockSpec((k, tn), lambda i, j: (0, j)),
                  pl.BlockSpec((k, tn), lambda i, j: (0, j))],
        out_specs=pl.BlockSpec((tm, tn), lambda i, j: (i, j)),
        out_shape=jax.ShapeDtypeStruct((m, n), BF16),
        compiler_params=_params("parallel", "arbitrary"),
        name="ffn_up",
    )(h, wg, wu)


def _cast_kernel(w_ref, o_ref):
    o_ref[...] = w_ref[...].astype(o_ref.dtype)


def cast_columns(w, layer, n_cols, tk=512, tn=2816):
    _, k, _ = w.shape
    return pl.pallas_call(
        _cast_kernel,
        grid=(k // tk, n_cols // tn),
        in_specs=[pl.BlockSpec((None, tk, tn), lambda i, j: (layer, i, j))],
        out_specs=pl.BlockSpec((tk, tn), lambda i, j: (i, j)),
        out_shape=jax.ShapeDtypeStruct((k, n_cols), BF16),
        compiler_params=_params("parallel", "parallel"),
        name="cast_columns",
    )(w)


def _layer(x, norm_attn, w_in, layer, ck_pe, ck_w1, ck_w2, cv_pe, cv_w1, cv_w2,
           out_norm_a, out_norm_b, w_out, norm_ffn, w_gate, w_up, w_down, cos_t, sin_t, batch, seq):
    w_main = w_in.astype(BF16)
    w_gl = lax.slice(w_in, (layer, 0, D_MAIN), (layer + 1, D_MODEL, D_MAIN + N_GATES))
    w_gl = w_gl.reshape(D_MODEL, KV_GROUPS_B, REP_B * 3)
    w_gl = jnp.pad(w_gl, ((0, 0), (0, 0), (0, LANES - REP_B * 3))).reshape(D_MODEL, KV_GROUPS_B * LANES).astype(BF16)
    pe = jnp.stack([ck_pe, cv_pe])
    w1 = jnp.stack([ck_w1, cv_w1]).reshape(2, CMP_LEN, HEAD_DIM, HEAD_DIM).astype(BF16)
    w2 = jnp.stack([ck_w2, cv_w2]).astype(BF16)

    h, gl = rms_gate(x, norm_attn, w_gl)
    pf = project(h, w_main, layer, cos_t, sin_t, F32_TILES, F32, batch, seq, layout="view")
    pr = project(h, w_main, layer, cos_t, sin_t, ROW_TILES, BF16, batch, seq)
    pc = project(h, w_main, layer, cos_t, sin_t, COL_TILES, BF16, batch, seq, layout="cols")
    o_a = dilated_attention(pf, batch, seq)
    kvc, kvct = compress(pf, pe, w1, w2, batch, seq)
    o_b = nsa_attention(pr, pc, kvc, kvct, gl, batch, seq)
    mixed = rms_pair(o_a, o_b, out_norm_a, out_norm_b)
    x1 = matmul_residual(mixed, w_out.astype(BF16), x, tm=1024, tn=1024)
    h2 = rms(x1, norm_ffn, BF16)
    act = ffn_up(h2, w_gate.astype(BF16), w_up.astype(BF16), tm=2048, tn=256)
    x2 = matmul_residual(act, w_down.astype(BF16), x1, tm=512, tn=512)
    return x2


def kernel(x, norm_attn, w_in, ck_pe, ck_w1, ck_w2, cv_pe, cv_w1, cv_w2, out_norm_a, out_norm_b, w_out,
           norm_ffn, w_gate, w_up, w_down, norm_final):
    batch, seq, d = x.shape
    depth = w_in.shape[0]
    cos_t, sin_t = _rope_tables(seq)
    xf = x.reshape(batch * seq, d)
    for l in range(depth):
        xf = _layer(xf, norm_attn[l], w_in, l, ck_pe[l], ck_w1[l], ck_w2[l], cv_pe[l], cv_w1[l], cv_w2[l],
                    out_norm_a[l], out_norm_b[l], w_out[l], norm_ffn[l], w_gate[l], w_up[l], w_down[l],
                    cos_t, sin_t, batch, seq)
    return rms(xf, norm_final, F32).reshape(batch, seq, d)
```

```python
import functools
import math

import jax
import jax.numpy as jnp
from jax import lax
from jax.experimental import pallas as pl
from jax.experimental.pallas import tpu as pltpu

F32 = jnp.float32
BF16 = jnp.bfloat16

D_MODEL = 4096
HEAD_DIM = 128
HEADS_A = 16
HEADS_B = 16
KV_GROUPS_B = 4
REP_B = HEADS_B // KV_GROUPS_B
WIDTH_A = HEADS_A * HEAD_DIM
WIDTH_B = HEADS_B * HEAD_DIM
DILATED_CONFIGS = ((128, 1), (512, 4), (2048, 16))
BLK = 128
CMP_LEN = 32
CMP_STRIDE = 16
SLC_BLOCK = 64
N_SELECT = 16
WIN = 512
FORCE_SCORE = 1e6
ROPE_THETA = 500000.0
ROPE_DIM = HEAD_DIM // 4
EPS = 1e-5
Q_SCALE = HEAD_DIM ** -0.5 * math.log2(math.e)
N_GATES = 3 * HEADS_B
D_MAIN = 3 * WIDTH_A + WIDTH_B + 6 * KV_GROUPS_B * HEAD_DIM

VMEM_LIMIT_BYTES = 56 * 1024 * 1024
LANES = 128
NEG = -1e30
SEL_BIAS = -32768.0

PROJ_TILE = 4 * HEAD_DIM
F32_TILES = tuple(range(12)) + (16, 17)
ROW_TILES = (18, 20)
COL_TILES = (12, 13, 14, 15, 19, 21)
Q_TILES = (0, 1, 2, 3, 12, 13, 14, 15)
V_TILES = (8, 9, 10, 11, 17, 19, 21)
PF_QA, PF_KA, PF_VA, PF_KC, PF_VC = 0, 16, 32, 48, 52
PR_KS, PR_KW = 0, 4
PC_QB, PC_VS, PC_VW = 0, 16, 20


def _params(*sem):
    return pltpu.CompilerParams(dimension_semantics=sem, vmem_limit_bytes=VMEM_LIMIT_BYTES)


def _dot(a, b):
    return jnp.dot(a, b, preferred_element_type=F32)


def _dot_nt(a, b):
    return lax.dot_general(a, b, (((1,), (1,)), ((), ())), preferred_element_type=F32)


def _rms(x, g):
    return x * lax.rsqrt(jnp.mean(x * x, axis=-1, keepdims=True) + EPS) * g


def _rms_gate_kernel(x_ref, g_ref, wgl_ref, h_ref, gl_ref):
    hb = _rms(x_ref[...], g_ref[...]).astype(BF16)
    h_ref[...] = hb
    gl_ref[...] = _dot(hb, wgl_ref[...])


def rms_gate(x, gain, w_gl, tm=256):
    m, d = x.shape
    ng = w_gl.shape[1]
    return pl.pallas_call(
        _rms_gate_kernel,
        grid=(m // tm,),
        in_specs=[pl.BlockSpec((tm, d), lambda i: (i, 0)),
                  pl.BlockSpec((1, d), lambda i: (0, 0)),
                  pl.BlockSpec((d, ng), lambda i: (0, 0))],
        out_specs=[pl.BlockSpec((tm, d), lambda i: (i, 0)),
                   pl.BlockSpec((tm, ng), lambda i: (i, 0))],
        out_shape=[jax.ShapeDtypeStruct((m, d), BF16), jax.ShapeDtypeStruct((m, ng), F32)],
        compiler_params=_params("parallel"),
        name="rms_gate",
    )(x, gain.reshape(1, d), w_gl)


def _rms_kernel(x_ref, g_ref, o_ref):
    o_ref[...] = _rms(x_ref[...], g_ref[...]).astype(o_ref.dtype)


def rms(x, gain, out_dtype, tm=256):
    m, d = x.shape
    return pl.pallas_call(
        _rms_kernel,
        grid=(m // tm,),
        in_specs=[pl.BlockSpec((tm, d), lambda i: (i, 0)),
                  pl.BlockSpec((1, d), lambda i: (0, 0))],
        out_specs=pl.BlockSpec((tm, d), lambda i: (i, 0)),
        out_shape=jax.ShapeDtypeStruct((m, d), out_dtype),
        compiler_params=_params("parallel"),
        name="rms",
    )(x, gain.reshape(1, d))


def _rms_pair_kernel(a_ref, b_ref, ga_ref, gb_ref, o_ref):
    n_heads, nv, _ = a_ref.shape
    wa = n_heads * HEAD_DIM
    lanes = [slice(r * HEAD_DIM, (r + 1) * HEAD_DIM) for r in range(VIEW_RES)]
    inv = []
    for r in range(VIEW_RES):
        sq = a_ref[0, :, lanes[r]] * a_ref[0, :, lanes[r]]
        for hh in range(1, n_heads):
            sq = sq + a_ref[hh, :, lanes[r]] * a_ref[hh, :, lanes[r]]
        inv.append(lax.rsqrt(jnp.sum(sq, axis=-1, keepdims=True) * (1.0 / wa) + EPS))
    for hh in range(n_heads):
        by_res = jnp.stack([a_ref[hh, :, lanes[r]] * inv[r] for r in range(VIEW_RES)], axis=0)
        tokens = jnp.swapaxes(by_res, 0, 1).reshape(nv * VIEW_RES, HEAD_DIM)
        cols = slice(hh * HEAD_DIM, (hh + 1) * HEAD_DIM)
        o_ref[:, cols] = (tokens * ga_ref[:, cols]).astype(o_ref.dtype)
    o_ref[:, wa:] = _rms(b_ref[...], gb_ref[...]).astype(o_ref.dtype)


def rms_pair(a, b, ga, gb, tm=256):
    n_heads, batch, rows, width = a.shape
    assert width == VIEW_RES * HEAD_DIM and (rows * VIEW_RES) % tm == 0
    wa = n_heads * HEAD_DIM
    m, wb = b.shape
    tiles_per_seq = rows * VIEW_RES // tm
    return pl.pallas_call(
        _rms_pair_kernel,
        grid=(m // tm,),
        in_specs=[pl.BlockSpec((n_heads, None, tm // VIEW_RES, width),
                               lambda i: (0, i // tiles_per_seq, i % tiles_per_seq, 0)),
                  pl.BlockSpec((tm, wb), lambda i: (i, 0)),
                  pl.BlockSpec((1, wa), lambda i: (0, 0)),
                  pl.BlockSpec((1, wb), lambda i: (0, 0))],
        out_specs=pl.BlockSpec((tm, wa + wb), lambda i: (i, 0)),
        out_shape=jax.ShapeDtypeStruct((m, wa + wb), BF16),
        compiler_params=_params("parallel"),
        name="rms_pair",
    )(a, b, ga.reshape(1, wa), gb.reshape(1, wb))


def _rope_tables(seq):
    inv = ROPE_THETA ** (-jnp.arange(0, ROPE_DIM, 2, dtype=F32) / ROPE_DIM)
    ang = jnp.arange(seq, dtype=F32)[:, None] * inv[None, :]
    cos, sin = jnp.cos(ang), jnp.sin(ang)
    ones = jnp.ones((seq, HEAD_DIM - ROPE_DIM), F32)
    cos_t = jnp.concatenate([cos, cos, ones], axis=1)
    sin_t = jnp.concatenate([-sin, sin, 0.0 * ones], axis=1)
    return cos_t, sin_t


PROJ_ROWS = 256
VIEW_RES = DILATED_CONFIGS[-1][1]


def _proj_kernel(h_ref, w_ref, cos_ref, sin_ref, o_ref, *, src_tiles, layout):
    jt = pl.program_id(1)
    is_v = functools.reduce(jnp.logical_or, [jt == n for n, t in enumerate(src_tiles) if t in V_TILES], False)
    is_q = functools.reduce(jnp.logical_or, [jt == n for n, t in enumerate(src_tiles) if t in Q_TILES], False)
    heads_per_tile = PROJ_TILE // HEAD_DIM
    tm = h_ref.shape[0]
    scale = jnp.where(is_q, Q_SCALE, 1.0).astype(F32)
    lane = lax.broadcasted_iota(jnp.int32, (1, HEAD_DIM), 1)
    first_half = lane < ROPE_DIM // 2

    for c in range(tm // PROJ_ROWS):
        rows = slice(c * PROJ_ROWS, (c + 1) * PROJ_ROWS)
        y = _dot(h_ref[rows, :], w_ref[...])
        cos_t = jnp.where(is_v, 1.0, cos_ref[rows, :]) * scale
        sin_t = jnp.where(is_v, 0.0, sin_ref[rows, :]) * scale
        for hh in range(heads_per_tile):
            yh = y[:, hh * HEAD_DIM:(hh + 1) * HEAD_DIM]
            partner = jnp.where(first_half,
                                pltpu.roll(yh, HEAD_DIM - ROPE_DIM // 2, axis=1),
                                pltpu.roll(yh, ROPE_DIM // 2, axis=1))
            val = yh * cos_t + partner * sin_t
            if layout == "cols":
                for cc in range(PROJ_ROWS // LANES):
                    col0 = c * PROJ_ROWS + cc * LANES
                    o_ref[hh, :, col0:col0 + LANES] = val[cc * LANES:(cc + 1) * LANES, :].T.astype(o_ref.dtype)
            elif layout == "view":
                nv = PROJ_ROWS // VIEW_RES
                by_res = jnp.swapaxes(val.reshape(nv, VIEW_RES, HEAD_DIM), 0, 1)
                for r in range(VIEW_RES):
                    o_ref[hh, c * nv:(c + 1) * nv, r * HEAD_DIM:(r + 1) * HEAD_DIM] = by_res[r].astype(o_ref.dtype)
            else:
                o_ref[hh, rows, :] = val.astype(o_ref.dtype)


def project(h, w, layer, cos_t, sin_t, src_tiles, out_dtype, batch, seq, layout="rows", tm=1024):
    m, d = h.shape
    n_tiles = len(src_tiles)
    heads_per_tile = PROJ_TILE // HEAD_DIM
    tm = min(tm, seq)
    pos_blocks = seq // tm

    def w_map(i, j):
        col = functools.reduce(lambda acc, nt: jnp.where(j == nt[0], nt[1], acc),
                               list(enumerate(src_tiles)), 0)
        return (layer, 0, col)

    if layout == "cols":
        out_spec = pl.BlockSpec((heads_per_tile, None, HEAD_DIM, tm),
                                lambda i, j: (j, i // pos_blocks, 0, i % pos_blocks))
        out_shape = jax.ShapeDtypeStruct((n_tiles * heads_per_tile, batch, HEAD_DIM, seq), out_dtype)
    elif layout == "view":
        out_spec = pl.BlockSpec((heads_per_tile, None, tm // VIEW_RES, VIEW_RES * HEAD_DIM),
                                lambda i, j: (j, i // pos_blocks, i % pos_blocks, 0))
        out_shape = jax.ShapeDtypeStruct((n_tiles * heads_per_tile, batch, seq // VIEW_RES, VIEW_RES * HEAD_DIM),
                                         out_dtype)
    else:
        out_spec = pl.BlockSpec((heads_per_tile, tm, HEAD_DIM), lambda i, j: (j, i, 0))
        out_shape = jax.ShapeDtypeStruct((n_tiles * heads_per_tile, m, HEAD_DIM), out_dtype)

    return pl.pallas_call(
        functools.partial(_proj_kernel, src_tiles=src_tiles, layout=layout),
        grid=(m // tm, n_tiles),
        in_specs=[pl.BlockSpec((tm, d), lambda i, j: (i, 0)),
                  pl.BlockSpec((None, d, PROJ_TILE), w_map),
                  pl.BlockSpec((tm, HEAD_DIM), lambda i, j: (i % pos_blocks, 0)),
                  pl.BlockSpec((tm, HEAD_DIM), lambda i, j: (i % pos_blocks, 0))],
        out_specs=out_spec,
        out_shape=out_shape,
        compiler_params=_params("parallel", "arbitrary"),
        name="project_" + layout,
    )(h, w, cos_t, sin_t)


DIL_UNROLL = 8
DIL_RES = VIEW_RES


def _dilated_kernel(q_ref, k_ref, v_ref, o_ref, acc_ref, m_ref, l_ref, bias_ref, *, seq):
    for ci, (window, dil) in enumerate(DILATED_CONFIGS):
        band = window // dil
        n_pieces = DIL_RES // dil
        pr = BLK // n_pieces
        kr = 2 * pr
        tiles_per_res = seq // (BLK * dil)
        assert band <= BLK and pr % 8 == 0 and pr & (pr - 1) == 0

        q_row = lax.broadcasted_iota(jnp.int32, (BLK, 1), 0)
        k_row = lax.broadcasted_iota(jnp.int32, (1, 2 * BLK), 1)
        q_sub = n_pieces * (q_row & (pr - 1)) + (q_row >> (pr.bit_length() - 1))
        k_sub = n_pieces * (k_row & (kr - 1)) + (k_row >> (kr.bit_length() - 1))
        for lead in range(2):
            dist = (q_sub + lead * BLK) - k_sub
            bias_ref[2 * ci + lead] = jnp.where((dist >= 0) & (dist <= band), 0.0, NEG)

        def body(step, carry, ci=ci, dil=dil, band=band, n_pieces=n_pieces, pr=pr, kr=kr,
                 tiles_per_res=tiles_per_res):
            def pieces(ref, row0, nrows, r):
                return [ref[pl.ds(row0, nrows), pl.ds(pl.multiple_of((r + dil * a) * HEAD_DIM, HEAD_DIM), HEAD_DIM)]
                        for a in range(n_pieces)]

            tiles = []
            for u in range(DIL_UNROLL):
                idx = step * DIL_UNROLL + u
                r = idx // tiles_per_res
                i = idx - r * tiles_per_res
                kb = jnp.maximum(i - 1, 0)
                q_row0 = pl.multiple_of(i * pr, pr)
                k_row0 = pl.multiple_of(kb * pr, pr)
                m_old = None if ci == 0 else jnp.concatenate(pieces(m_ref, q_row0, pr, r), axis=0)
                tiles.append((r, i, kb, q_row0, k_row0, m_old))

            def update(r, q_row0, m_old, m_new, l_tile, pv):
                if ci == 0:
                    l_new = jnp.broadcast_to(l_tile, (BLK, LANES))
                    acc_new = pv
                else:
                    alpha = jnp.exp2(m_old - m_new)
                    l_new = alpha * jnp.concatenate(pieces(l_ref, q_row0, pr, r), axis=0) + l_tile
                    acc_new = alpha * jnp.concatenate(pieces(acc_ref, q_row0, pr, r), axis=0) + pv
                for a in range(n_pieces):
                    col = pl.ds(pl.multiple_of((r + dil * a) * HEAD_DIM, HEAD_DIM), HEAD_DIM)
                    rows = slice(a * pr, (a + 1) * pr)
                    m_ref[pl.ds(q_row0, pr), col] = m_new[rows]
                    l_ref[pl.ds(q_row0, pr), col] = l_new[rows]
                    acc_ref[pl.ds(q_row0, pr), col] = acc_new[rows]

            scores, maxes, probs, results = {}, {}, {}, []
            for t in range(DIL_UNROLL + 4):
                if t < DIL_UNROLL:
                    (r, i, kb, q_row0, k_row0, m_old) = tiles[t]
                    q = jnp.concatenate(pieces(q_ref, q_row0, pr, r), axis=0).astype(BF16)
                    k = jnp.concatenate(pieces(k_ref, k_row0, kr, r), axis=0).astype(BF16)
                    scores[t] = _dot_nt(q, k) + bias_ref[2 * ci + i - kb]
                if 0 <= t - 1 < DIL_UNROLL:
                    m_tile = jnp.max(scores[t - 1], axis=1, keepdims=True)
                    m_old = tiles[t - 1][5]
                    maxes[t - 1] = jnp.broadcast_to(m_tile, (BLK, LANES)) if ci == 0 else jnp.maximum(m_old, m_tile)
                if 0 <= t - 2 < DIL_UNROLL:
                    m_new = maxes[t - 2]
                    probs[t - 2] = jnp.exp2(scores.pop(t - 2) - jnp.concatenate([m_new, m_new], axis=1))
                if 0 <= t - 3 < DIL_UNROLL:
                    (r, i, kb, q_row0, k_row0, m_old) = tiles[t - 3]
                    p = probs.pop(t - 3)
                    v = jnp.concatenate(pieces(v_ref, k_row0, kr, r), axis=0).astype(BF16)
                    l_tile = jnp.sum(p, axis=1, keepdims=True)
                    results.append((r, q_row0, m_old, maxes.pop(t - 3), l_tile, _dot(p.astype(BF16), v)))
                if 0 <= t - 4 < DIL_UNROLL:
                    update(*results[t - 4])
            return carry

        lax.fori_loop(0, seq // (BLK * DIL_UNROLL), body, 0)
    o_ref[...] = acc_ref[...] / l_ref[...]


def dilated_attention(pf, batch, seq):
    assert all(DIL_RES % d == 0 for _, d in DILATED_CONFIGS)
    assert seq % (BLK * DIL_RES) == 0 and seq // DIL_RES >= 2 * BLK and (seq // BLK) % DIL_UNROLL == 0
    rows = seq // DIL_RES
    width = DIL_RES * HEAD_DIM
    view = pf

    def spec(off):
        return pl.BlockSpec((None, None, rows, width), lambda b, h: (off + h, b, 0, 0))

    out = pl.pallas_call(
        functools.partial(_dilated_kernel, seq=seq),
        grid=(batch, HEADS_A),
        in_specs=[spec(PF_QA), spec(PF_KA), spec(PF_VA)],
        out_specs=spec(0),
        out_shape=jax.ShapeDtypeStruct((HEADS_A, batch, rows, width), F32),
        scratch_shapes=[pltpu.VMEM((rows, width), F32)] * 3
        + [pltpu.VMEM((2 * len(DILATED_CONFIGS), BLK, 2 * BLK), F32)],
        compiler_params=_params("parallel", "parallel"),
        name="dilated_attention",
    )(view, view, view)
    return out


def _compress_kernel(t_ref, pe_ref, w1_ref, w2_ref, o_ref, ot_ref, *, seq):
    n_chunks = seq // CMP_STRIDE
    first = jnp.zeros((n_chunks, HEAD_DIM), F32)
    second = jnp.zeros((n_chunks, HEAD_DIM), F32)
    for i in range(CMP_STRIDE):
        ti = t_ref[:, i * HEAD_DIM:(i + 1) * HEAD_DIM]
        first += _dot((ti + pe_ref[pl.ds(i, 1), :]).astype(BF16), w1_ref[i])
        second += _dot((ti + pe_ref[pl.ds(CMP_STRIDE + i, 1), :]).astype(BF16), w1_ref[CMP_STRIDE + i])
    pre = first + pltpu.roll(second, n_chunks - 1, axis=0)
    out = _dot(jax.nn.gelu(pre).astype(BF16), w2_ref[...])
    o_ref[...] = out.astype(o_ref.dtype)
    ot_ref[...] = out.T.astype(ot_ref.dtype)


def compress(pf, pe, w1, w2, batch, seq):
    assert CMP_LEN == 2 * CMP_STRIDE and CMP_STRIDE == VIEW_RES
    n_chunks = seq // CMP_STRIDE
    pf4 = pf
    bg = batch * KV_GROUPS_B
    return pl.pallas_call(
        functools.partial(_compress_kernel, seq=seq),
        grid=(2, batch, KV_GROUPS_B),
        in_specs=[pl.BlockSpec((None, None, n_chunks, VIEW_RES * HEAD_DIM),
                               lambda kv, b, g: (PF_KC + kv * KV_GROUPS_B + g, b, 0, 0)),
                  pl.BlockSpec((None, CMP_LEN, HEAD_DIM), lambda kv, b, g: (kv, 0, 0)),
                  pl.BlockSpec((None, CMP_LEN, HEAD_DIM, HEAD_DIM), lambda kv, b, g: (kv, 0, 0, 0)),
                  pl.BlockSpec((None, HEAD_DIM, HEAD_DIM), lambda kv, b, g: (kv, 0, 0))],
        out_specs=[pl.BlockSpec((None, None, n_chunks, HEAD_DIM), lambda kv, b, g: (kv, b * KV_GROUPS_B + g, 0, 0)),
                   pl.BlockSpec((None, None, HEAD_DIM, n_chunks), lambda kv, b, g: (kv, b * KV_GROUPS_B + g, 0, 0))],
        out_shape=[jax.ShapeDtypeStruct((2, bg, n_chunks, HEAD_DIM), BF16),
                   jax.ShapeDtypeStruct((2, bg, HEAD_DIM, n_chunks), BF16)],
        compiler_params=_params("parallel", "parallel", "parallel"),
        name="compress",
    )(pf4, pe, w1, w2)


NSA_TQ = 512
NSA_TK = 512


def _select_blocks(score):
    n_s, tq = score.shape
    groups = n_s // 8
    rows8 = [score[8 * v:8 * v + 8, :] for v in range(groups)]
    rank8 = [jnp.zeros((8, tq), F32) for _ in range(groups)]
    sub = lax.broadcasted_iota(jnp.int32, (8, 1), 0)
    for jp in range(n_s):
        vp, sp = divmod(jp, 8)
        row = jnp.broadcast_to(rows8[vp][sp:sp + 1, :], (8, tq))
        for v in range(groups):
            if v > vp:
                beats = jnp.where(row >= rows8[v], 1.0, 0.0)
            elif v < vp:
                beats = jnp.where(row > rows8[v], 1.0, 0.0)
            else:
                beats = jnp.where(sub > sp, jnp.where(row >= rows8[v], 1.0, 0.0),
                                  jnp.where(row > rows8[v], 1.0, 0.0))
            rank8[v] = rank8[v] + beats
    rank = jnp.concatenate(rank8, axis=0)
    return (rank < N_SELECT) & (score > -jnp.inf)


def _flash_block(s, vt, carry):
    m_i, l_i, acc = carry
    m_new = jnp.maximum(m_i, jnp.max(s, axis=0, keepdims=True))
    alpha = jnp.exp2(m_i - m_new)
    p = jnp.exp2(s - m_new)
    return m_new, alpha * l_i + jnp.sum(p, axis=0, keepdims=True), alpha * acc + _dot(vt, p.astype(BF16))


def _nsa_kernel(q_ref, kc_ref, vct_ref, ks_ref, vst_ref, kw_ref, vwt_ref, e_ref, gl_ref, o_ref, *, seq):
    tq, tk = NSA_TQ, NSA_TK
    cols = REP_B * tq
    n_cp = seq // CMP_STRIDE
    n_s = seq // SLC_BLOCK
    qi = pl.program_id(2)
    t0 = qi * tq
    q_t = jnp.concatenate([q_ref[r] for r in range(REP_B)], axis=1)
    lane = lax.broadcasted_iota(jnp.int32, (1, tq), 1)
    tpos = t0 + lane

    def per_head(x):
        return jnp.concatenate([x] * REP_B, axis=1)

    csub = lax.broadcasted_iota(jnp.int32, (n_cp, 1), 0)
    c_ok = (csub * CMP_STRIDE + (CMP_LEN - 1) <= tpos) & (csub < n_cp - 1)
    sc = _dot(kc_ref[...], q_t) + per_head(jnp.where(c_ok, 0.0, NEG))
    e = jnp.exp2(sc - jnp.max(sc, axis=0, keepdims=True))
    has_block = per_head(jnp.where(tpos >= CMP_LEN - 1, 1.0, 0.0))
    p_cmp = e * (has_block / jnp.maximum(jnp.sum(e, axis=0, keepdims=True), 1e-30))
    o_cmp = _dot(vct_ref[...], p_cmp.astype(BF16))

    ws = WIN + tq
    w0 = pl.multiple_of(jnp.maximum(t0 - WIN, 0), tq)
    dist = tpos - (w0 + lax.broadcasted_iota(jnp.int32, (ws, 1), 0))
    win_bias = per_head(jnp.where((dist >= 0) & (dist <= WIN - 1), 0.0, NEG))
    s_w = _dot(kw_ref[pl.ds(w0, ws), :], q_t) + win_bias
    p_w = jnp.exp2(s_w - jnp.max(s_w, axis=0, keepdims=True))
    o_win = _dot(vwt_ref[:, pl.ds(w0, ws)], p_w.astype(BF16)) * (1.0 / jnp.sum(p_w, axis=0, keepdims=True))

    p_sum = p_cmp[:, 0:tq]
    for r in range(1, REP_B):
        p_sum = p_sum + p_cmp[:, r * tq:(r + 1) * tq]
    jj = lax.broadcasted_iota(jnp.int32, (n_s, 1), 0)
    cidx = lax.broadcasted_iota(jnp.int32, (1, n_cp), 1)
    ratio = SLC_BLOCK // CMP_STRIDE
    c_first = ratio * jj - (CMP_LEN // CMP_STRIDE - 1)
    hits = jnp.where((cidx >= c_first) & (cidx < ratio * (jj + 1)), 1.0, 0.0).astype(BF16)
    p_hi = p_sum.astype(BF16)
    p_lo = (p_sum - p_hi.astype(F32)).astype(BF16)
    imp = _dot(hits, p_hi) + _dot(hits, p_lo)
    qblk = tpos >> (SLC_BLOCK.bit_length() - 1)
    forced = (jj == 0) | (jj == qblk) | (jj == qblk - 1)
    valid = jj * SLC_BLOCK <= tpos
    score = jnp.where(forced, FORCE_SCORE, jnp.where(valid, imp, -jnp.inf))
    sel = _select_blocks(score)
    bias = jnp.where(sel, 0.0, SEL_BIAS)
    if n_s < LANES:
        bias = jnp.concatenate([bias, jnp.full((LANES - n_s, tq), SEL_BIAS, F32)], axis=0)
    q_aug = jnp.concatenate([q_t, per_head(bias.astype(BF16))], axis=0)

    def slc_scores(kt):
        k0 = pl.multiple_of(kt * tk, tk)
        k_aug = jnp.concatenate([ks_ref[pl.ds(k0, tk), :], e_ref[pl.ds(k0, tk), :]], axis=1)
        return _dot(k_aug, q_aug)

    def slc_values(kt):
        return vst_ref[:, pl.ds(pl.multiple_of(kt * tk, tk), tk)]

    def slc_step(kt, stats):
        return _flash_block(slc_scores(kt), slc_values(kt), stats)

    last = (t0 + tq - 1) // tk
    stats = (jnp.full((1, cols), NEG, F32), jnp.zeros((1, cols), F32), jnp.zeros((HEAD_DIM, cols), F32))
    stats = lax.fori_loop(0, last, slc_step, stats)
    tok = last * tk + lax.broadcasted_iota(jnp.int32, (tk, 1), 0)
    s_last = slc_scores(last) + per_head(jnp.where(tok <= tpos, 0.0, NEG))
    _, l_s, acc_s = _flash_block(s_last, slc_values(last), stats)
    o_slc = acc_s * (1.0 / l_s)

    gates = jax.nn.sigmoid(gl_ref[...].T)
    for r in range(REP_B):
        sl = slice(r * tq, (r + 1) * tq)
        merged = (gates[3 * r:3 * r + 1, :] * o_cmp[:, sl]
                  + gates[3 * r + 1:3 * r + 2, :] * o_slc[:, sl]
                  + gates[3 * r + 2:3 * r + 3, :] * o_win[:, sl])
        o_ref[:, r * HEAD_DIM:(r + 1) * HEAD_DIM] = merged.T


def nsa_attention(pr, pc, kvc, kvct, gl, batch, seq):
    tq = NSA_TQ
    n_s = seq // SLC_BLOCK
    assert seq % NSA_TK == 0 and NSA_TK % tq == 0 and seq >= WIN + tq and n_s <= LANES and n_s % 8 == 0
    assert SLC_BLOCK & (SLC_BLOCK - 1) == 0
    n_cp = seq // CMP_STRIDE
    nq = seq // tq
    pr4 = pr.reshape(pr.shape[0], batch, seq, HEAD_DIM)
    block_onehot = (jnp.arange(seq)[:, None] // SLC_BLOCK == jnp.arange(LANES)[None, :]).astype(BF16)

    def k_spec(off):
        return pl.BlockSpec((None, None, seq, HEAD_DIM), lambda b, g, i: (off + g, b, 0, 0))

    def vt_spec(off):
        return pl.BlockSpec((None, None, HEAD_DIM, seq), lambda b, g, i: (off + g, b, 0, 0))

    return pl.pallas_call(
        functools.partial(_nsa_kernel, seq=seq),
        grid=(batch, KV_GROUPS_B, nq),
        in_specs=[pl.BlockSpec((REP_B, None, HEAD_DIM, tq), lambda b, g, i: (PC_QB // REP_B + g, b, 0, i)),
                  pl.BlockSpec((None, None, n_cp, HEAD_DIM), lambda b, g, i: (0, b * KV_GROUPS_B + g, 0, 0)),
                  pl.BlockSpec((None, None, HEAD_DIM, n_cp), lambda b, g, i: (1, b * KV_GROUPS_B + g, 0, 0)),
                  k_spec(PR_KS), vt_spec(PC_VS), k_spec(PR_KW), vt_spec(PC_VW),
                  pl.BlockSpec((seq, LANES), lambda b, g, i: (0, 0)),
                  pl.BlockSpec((tq, LANES), lambda b, g, i: (b * nq + i, g))],
        out_specs=pl.BlockSpec((tq, REP_B * HEAD_DIM), lambda b, g, i: (b * nq + i, g)),
        out_shape=jax.ShapeDtypeStruct((batch * seq, WIDTH_B), F32),
        compiler_params=_params("parallel", "parallel", "arbitrary"),
        name="nsa_attention",
    )(pc, kvc, kvct, pr4, pc, pr4, pc, block_onehot, gl)


def _mm_res_kernel(a_ref, w_ref, r_ref, o_ref):
    o_ref[...] = _dot(a_ref[...], w_ref[...]) + r_ref[...]


def matmul_residual(a, w, res, tm, tn):
    m, k = a.shape
    n = w.shape[1]
    return pl.pallas_call(
        _mm_res_kernel,
        grid=(m // tm, n // tn),
        in_specs=[pl.BlockSpec((tm, k), lambda i, j: (i, 0)),
                  pl.BlockSpec((k, tn), lambda i, j: (0, j)),
                  pl.BlockSpec((tm, tn), lambda i, j: (i, j))],
        out_specs=pl.BlockSpec((tm, tn), lambda i, j: (i, j)),
        out_shape=jax.ShapeDtypeStruct((m, n), F32),
        compiler_params=_params("parallel", "arbitrary"),
        name="matmul_residual",
    )(a, w, res)


def _ffn_up_kernel(h_ref, wg_ref, wu_ref, o_ref):
    h = h_ref[...]
    g = _dot(h, wg_ref[...].astype(BF16))
    u = _dot(h, wu_ref[...].astype(BF16))
    o_ref[...] = (g * jax.nn.sigmoid(g) * u).astype(o_ref.dtype)


def ffn_up(h, wg, wu, layer, tm, tn):
    m, k = h.shape
    n = wg.shape[2]
    return pl.pallas_call(
        _ffn_up_kernel,
        grid=(m // tm, n // tn),
        in_specs=[pl.BlockSpec((tm, k), lambda i, j: (i, 0), pipeline_mode=pl.Buffered(1)),
                  pl.BlockSpec((None, k, tn), lambda i, j: (layer, 0, j)),
                  pl.BlockSpec((None, k, tn), lambda i, j: (layer, 0, j))],
        out_specs=pl.BlockSpec((tm, tn), lambda i, j: (i, j)),
        out_shape=jax.ShapeDtypeStruct((m, n), BF16),
        compiler_params=_params("parallel", "arbitrary"),
        name="ffn_up",
    )(h, wg, wu)


def _cast_kernel(w_ref, o_ref):
    o_ref[...] = w_ref[...].astype(o_ref.dtype)


def cast_columns(w, layer, n_cols, tk=512, tn=2816):
    _, k, _ = w.shape
    return pl.pallas_call(
        _cast_kernel,
        grid=(k // tk, n_cols // tn),
        in_specs=[pl.BlockSpec((None, tk, tn), lambda i, j: (layer, i, j))],
        out_specs=pl.BlockSpec((tk, tn), lambda i, j: (i, j)),
        out_shape=jax.ShapeDtypeStruct((k, n_cols), BF16),
        compiler_params=_params("parallel", "parallel"),
        name="cast_columns",
    )(w)


def _layer(x, norm_attn, w_in, layer, ck_pe, ck_w1, ck_w2, cv_pe, cv_w1, cv_w2,
           out_norm_a, out_norm_b, w_out, norm_ffn, w_gate, w_up, w_down, cos_t, sin_t, batch, seq):
    w_main = w_in.astype(BF16)
    w_gl = lax.slice(w_in, (layer, 0, D_MAIN), (layer + 1, D_MODEL, D_MAIN + N_GATES))
    w_gl = w_gl.reshape(D_MODEL, KV_GROUPS_B, REP_B * 3)
    w_gl = jnp.pad(w_gl, ((0, 0), (0, 0), (0, LANES - REP_B * 3))).reshape(D_MODEL, KV_GROUPS_B * LANES).astype(BF16)
    pe = jnp.stack([ck_pe, cv_pe])
    w1 = jnp.stack([ck_w1, cv_w1]).reshape(2, CMP_LEN, HEAD_DIM, HEAD_DIM).astype(BF16)
    w2 = jnp.stack([ck_w2, cv_w2]).astype(BF16)

    h, gl = rms_gate(x, norm_attn, w_gl)
    pf = project(h, w_main, layer, cos_t, sin_t, F32_TILES, F32, batch, seq, layout="view")
    pr = project(h, w_main, layer, cos_t, sin_t, ROW_TILES, BF16, batch, seq)
    pc = project(h, w_main, layer, cos_t, sin_t, COL_TILES, BF16, batch, seq, layout="cols")
    o_a = dilated_attention(pf, batch, seq)
    kvc, kvct = compress(pf, pe, w1, w2, batch, seq)
    o_b = nsa_attention(pr, pc, kvc, kvct, gl, batch, seq)
    mixed = rms_pair(o_a, o_b, out_norm_a, out_norm_b)
    x1 = matmul_residual(mixed, w_out.astype(BF16), x, tm=1024, tn=1024)
    h2 = rms(x1, norm_ffn, BF16)
    act = ffn_up(h2, w_gate, w_up, layer, tm=2048, tn=256)
    x2 = matmul_residual(act, w_down.astype(BF16), x1, tm=512, tn=512)
    return x2


def kernel(x, norm_attn, w_in, ck_pe, ck_w1, ck_w2, cv_pe, cv_w1, cv_w2, out_norm_a, out_norm_b, w_out,
           norm_ffn, w_gate, w_up, w_down, norm_final):
    batch, seq, d = x.shape
    depth = w_in.shape[0]
    cos_t, sin_t = _rope_tables(seq)
    xf = x.reshape(batch * seq, d)
    for l in range(depth):
        xf = _layer(xf, norm_attn[l], w_in, l, ck_pe[l], ck_w1[l], ck_w2[l], cv_pe[l], cv_w1[l], cv_w2[l],
                    out_norm_a[l], out_norm_b[l], w_out[l], norm_ffn[l], w_gate, w_up, w_down[l],
                    cos_t, sin_t, batch, seq)
    return rms(xf, norm_final, F32).reshape(batch, seq, d)
```

```python
import functools
import math

import jax
import jax.numpy as jnp
from jax import lax
from jax.experimental import pallas as pl
from jax.experimental.pallas import tpu as pltpu

F32 = jnp.float32
BF16 = jnp.bfloat16

D_MODEL = 4096
HEAD_DIM = 128
HEADS_A = 16
HEADS_B = 16
KV_GROUPS_B = 4
REP_B = HEADS_B // KV_GROUPS_B
WIDTH_A = HEADS_A * HEAD_DIM
WIDTH_B = HEADS_B * HEAD_DIM
DILATED_CONFIGS = ((128, 1), (512, 4), (2048, 16))
BLK = 128
CMP_LEN = 32
CMP_STRIDE = 16
SLC_BLOCK = 64
N_SELECT = 16
WIN = 512
FORCE_SCORE = 1e6
ROPE_THETA = 500000.0
ROPE_DIM = HEAD_DIM // 4
EPS = 1e-5
Q_SCALE = HEAD_DIM ** -0.5 * math.log2(math.e)
N_GATES = 3 * HEADS_B
D_MAIN = 3 * WIDTH_A + WIDTH_B + 6 * KV_GROUPS_B * HEAD_DIM

VMEM_LIMIT_BYTES = 56 * 1024 * 1024
LANES = 128
NEG = -1e30
SEL_BIAS = -32768.0

PROJ_TILE = 4 * HEAD_DIM
F32_TILES = tuple(range(12)) + (16, 17)
ROW_TILES = (18, 20)
COL_TILES = (12, 13, 14, 15, 19, 21)
Q_TILES = (0, 1, 2, 3, 12, 13, 14, 15)
V_TILES = (8, 9, 10, 11, 17, 19, 21)
PF_QA, PF_KA, PF_VA, PF_KC, PF_VC = 0, 16, 32, 48, 52
PR_KS, PR_KW = 0, 4
PC_QB, PC_VS, PC_VW = 0, 16, 20


def _params(*sem):
    return pltpu.CompilerParams(dimension_semantics=sem, vmem_limit_bytes=VMEM_LIMIT_BYTES)


def _dot(a, b):
    return jnp.dot(a, b, preferred_element_type=F32)


def _dot_nt(a, b):
    return lax.dot_general(a, b, (((1,), (1,)), ((), ())), preferred_element_type=F32)


def _rms(x, g):
    return x * lax.rsqrt(jnp.mean(x * x, axis=-1, keepdims=True) + EPS) * g


def _rms_gate_kernel(x_ref, g_ref, wgl_ref, h_ref, gl_ref):
    hb = _rms(x_ref[...], g_ref[...]).astype(BF16)
    h_ref[...] = hb
    gl_ref[...] = _dot(hb, wgl_ref[...])


def rms_gate(x, gain, w_gl, tm=256):
    m, d = x.shape
    ng = w_gl.shape[1]
    return pl.pallas_call(
        _rms_gate_kernel,
        grid=(m // tm,),
        in_specs=[pl.BlockSpec((tm, d), lambda i: (i, 0)),
                  pl.BlockSpec((1, d), lambda i: (0, 0)),
                  pl.BlockSpec((d, ng), lambda i: (0, 0))],
        out_specs=[pl.BlockSpec((tm, d), lambda i: (i, 0)),
                   pl.BlockSpec((tm, ng), lambda i: (i, 0))],
        out_shape=[jax.ShapeDtypeStruct((m, d), BF16), jax.ShapeDtypeStruct((m, ng), F32)],
        compiler_params=_params("parallel"),
        name="rms_gate",
    )(x, gain.reshape(1, d), w_gl)


def _rms_kernel(x_ref, g_ref, o_ref):
    o_ref[...] = _rms(x_ref[...], g_ref[...]).astype(o_ref.dtype)


def rms(x, gain, out_dtype, tm=256):
    m, d = x.shape
    return pl.pallas_call(
        _rms_kernel,
        grid=(m // tm,),
        in_specs=[pl.BlockSpec((tm, d), lambda i: (i, 0)),
                  pl.BlockSpec((1, d), lambda i: (0, 0))],
        out_specs=pl.BlockSpec((tm, d), lambda i: (i, 0)),
        out_shape=jax.ShapeDtypeStruct((m, d), out_dtype),
        compiler_params=_params("parallel"),
        name="rms",
    )(x, gain.reshape(1, d))


def _rms_pair_kernel(a_ref, b_ref, ga_ref, gb_ref, o_ref):
    n_heads, nv, _ = a_ref.shape
    wa = n_heads * HEAD_DIM
    lanes = [slice(r * HEAD_DIM, (r + 1) * HEAD_DIM) for r in range(VIEW_RES)]
    inv = []
    for r in range(VIEW_RES):
        sq = a_ref[0, :, lanes[r]] * a_ref[0, :, lanes[r]]
        for hh in range(1, n_heads):
            sq = sq + a_ref[hh, :, lanes[r]] * a_ref[hh, :, lanes[r]]
        inv.append(lax.rsqrt(jnp.sum(sq, axis=-1, keepdims=True) * (1.0 / wa) + EPS))
    for hh in range(n_heads):
        by_res = jnp.stack([a_ref[hh, :, lanes[r]] * inv[r] for r in range(VIEW_RES)], axis=0)
        tokens = jnp.swapaxes(by_res, 0, 1).reshape(nv * VIEW_RES, HEAD_DIM)
        cols = slice(hh * HEAD_DIM, (hh + 1) * HEAD_DIM)
        o_ref[:, cols] = (tokens * ga_ref[:, cols]).astype(o_ref.dtype)
    o_ref[:, wa:] = _rms(b_ref[...], gb_ref[...]).astype(o_ref.dtype)


def rms_pair(a, b, ga, gb, tm=256):
    n_heads, batch, rows, width = a.shape
    assert width == VIEW_RES * HEAD_DIM and (rows * VIEW_RES) % tm == 0
    wa = n_heads * HEAD_DIM
    m, wb = b.shape
    tiles_per_seq = rows * VIEW_RES // tm
    return pl.pallas_call(
        _rms_pair_kernel,
        grid=(m // tm,),
        in_specs=[pl.BlockSpec((n_heads, None, tm // VIEW_RES, width),
                               lambda i: (0, i // tiles_per_seq, i % tiles_per_seq, 0)),
                  pl.BlockSpec((tm, wb), lambda i: (i, 0)),
                  pl.BlockSpec((1, wa), lambda i: (0, 0)),
                  pl.BlockSpec((1, wb), lambda i: (0, 0))],
        out_specs=pl.BlockSpec((tm, wa + wb), lambda i: (i, 0)),
        out_shape=jax.ShapeDtypeStruct((m, wa + wb), BF16),
        compiler_params=_params("parallel"),
        name="rms_pair",
    )(a, b, ga.reshape(1, wa), gb.reshape(1, wb))


def _rope_tables(seq):
    inv = ROPE_THETA ** (-jnp.arange(0, ROPE_DIM, 2, dtype=F32) / ROPE_DIM)
    ang = jnp.arange(seq, dtype=F32)[:, None] * inv[None, :]
    cos, sin = jnp.cos(ang), jnp.sin(ang)
    ones = jnp.ones((seq, HEAD_DIM - ROPE_DIM), F32)
    cos_t = jnp.concatenate([cos, cos, ones], axis=1)
    sin_t = jnp.concatenate([-sin, sin, 0.0 * ones], axis=1)
    return cos_t, sin_t


PROJ_ROWS = 256
VIEW_RES = DILATED_CONFIGS[-1][1]


def _proj_kernel(h_ref, w_ref, cos_ref, sin_ref, o_ref, *, src_tiles, layout):
    jt = pl.program_id(1)
    is_v = functools.reduce(jnp.logical_or, [jt == n for n, t in enumerate(src_tiles) if t in V_TILES], False)
    is_q = functools.reduce(jnp.logical_or, [jt == n for n, t in enumerate(src_tiles) if t in Q_TILES], False)
    heads_per_tile = PROJ_TILE // HEAD_DIM
    tm = h_ref.shape[0]
    scale = jnp.where(is_q, Q_SCALE, 1.0).astype(F32)
    lane = lax.broadcasted_iota(jnp.int32, (1, HEAD_DIM), 1)
    first_half = lane < ROPE_DIM // 2

    for c in range(tm // PROJ_ROWS):
        rows = slice(c * PROJ_ROWS, (c + 1) * PROJ_ROWS)
        y = _dot(h_ref[rows, :], w_ref[...])
        cos_t = jnp.where(is_v, 1.0, cos_ref[rows, :]) * scale
        sin_t = jnp.where(is_v, 0.0, sin_ref[rows, :]) * scale
        for hh in range(heads_per_tile):
            yh = y[:, hh * HEAD_DIM:(hh + 1) * HEAD_DIM]
            partner = jnp.where(first_half,
                                pltpu.roll(yh, HEAD_DIM - ROPE_DIM // 2, axis=1),
                                pltpu.roll(yh, ROPE_DIM // 2, axis=1))
            val = yh * cos_t + partner * sin_t
            if layout == "cols":
                for cc in range(PROJ_ROWS // LANES):
                    col0 = c * PROJ_ROWS + cc * LANES
                    o_ref[hh, :, col0:col0 + LANES] = val[cc * LANES:(cc + 1) * LANES, :].T.astype(o_ref.dtype)
            elif layout == "view":
                nv = PROJ_ROWS // VIEW_RES
                by_res = jnp.swapaxes(val.reshape(nv, VIEW_RES, HEAD_DIM), 0, 1)
                for r in range(VIEW_RES):
                    o_ref[hh, c * nv:(c + 1) * nv, r * HEAD_DIM:(r + 1) * HEAD_DIM] = by_res[r].astype(o_ref.dtype)
            else:
                o_ref[hh, rows, :] = val.astype(o_ref.dtype)


def project(h, w, layer, cos_t, sin_t, src_tiles, out_dtype, batch, seq, layout="rows", tm=1024):
    m, d = h.shape
    n_tiles = len(src_tiles)
    heads_per_tile = PROJ_TILE // HEAD_DIM
    tm = min(tm, seq)
    pos_blocks = seq // tm

    def w_map(i, j):
        col = functools.reduce(lambda acc, nt: jnp.where(j == nt[0], nt[1], acc),
                               list(enumerate(src_tiles)), 0)
        return (layer, 0, col)

    if layout == "cols":
        out_spec = pl.BlockSpec((heads_per_tile, None, HEAD_DIM, tm),
                                lambda i, j: (j, i // pos_blocks, 0, i % pos_blocks))
        out_shape = jax.ShapeDtypeStruct((n_tiles * heads_per_tile, batch, HEAD_DIM, seq), out_dtype)
    elif layout == "view":
        out_spec = pl.BlockSpec((heads_per_tile, None, tm // VIEW_RES, VIEW_RES * HEAD_DIM),
                                lambda i, j: (j, i // pos_blocks, i % pos_blocks, 0))
        out_shape = jax.ShapeDtypeStruct((n_tiles * heads_per_tile, batch, seq // VIEW_RES, VIEW_RES * HEAD_DIM),
                                         out_dtype)
    else:
        out_spec = pl.BlockSpec((heads_per_tile, tm, HEAD_DIM), lambda i, j: (j, i, 0))
        out_shape = jax.ShapeDtypeStruct((n_tiles * heads_per_tile, m, HEAD_DIM), out_dtype)

    return pl.pallas_call(
        functools.partial(_proj_kernel, src_tiles=src_tiles, layout=layout),
        grid=(m // tm, n_tiles),
        in_specs=[pl.BlockSpec((tm, d), lambda i, j: (i, 0)),
                  pl.BlockSpec((None, d, PROJ_TILE), w_map),
                  pl.BlockSpec((tm, HEAD_DIM), lambda i, j: (i % pos_blocks, 0)),
                  pl.BlockSpec((tm, HEAD_DIM), lambda i, j: (i % pos_blocks, 0))],
        out_specs=out_spec,
        out_shape=out_shape,
        compiler_params=_params("parallel", "arbitrary"),
        name="project_" + layout,
    )(h, w, cos_t, sin_t)


DIL_UNROLL = 8
DIL_RES = VIEW_RES


def _dilated_kernel(q_ref, k_ref, v_ref, o_ref, acc_ref, m_ref, l_ref, bias_ref, *, seq):
    for ci, (window, dil) in enumerate(DILATED_CONFIGS):
        band = window // dil
        n_pieces = DIL_RES // dil
        pr = BLK // n_pieces
        kr = 2 * pr
        tiles_per_res = seq // (BLK * dil)
        assert band <= BLK and pr % 8 == 0 and pr & (pr - 1) == 0

        q_row = lax.broadcasted_iota(jnp.int32, (BLK, 1), 0)
        k_row = lax.broadcasted_iota(jnp.int32, (1, 2 * BLK), 1)
        q_sub = n_pieces * (q_row & (pr - 1)) + (q_row >> (pr.bit_length() - 1))
        k_sub = n_pieces * (k_row & (kr - 1)) + (k_row >> (kr.bit_length() - 1))
        for lead in range(2):
            dist = (q_sub + lead * BLK) - k_sub
            bias_ref[2 * ci + lead] = jnp.where((dist >= 0) & (dist <= band), 0.0, NEG)

        def body(step, carry, ci=ci, dil=dil, band=band, n_pieces=n_pieces, pr=pr, kr=kr,
                 tiles_per_res=tiles_per_res):
            def pieces(ref, row0, nrows, r):
                return [ref[pl.ds(row0, nrows), pl.ds(pl.multiple_of((r + dil * a) * HEAD_DIM, HEAD_DIM), HEAD_DIM)]
                        for a in range(n_pieces)]

            tiles = []
            for u in range(DIL_UNROLL):
                idx = step * DIL_UNROLL + u
                r = idx // tiles_per_res
                i = idx - r * tiles_per_res
                kb = jnp.maximum(i - 1, 0)
                q_row0 = pl.multiple_of(i * pr, pr)
                k_row0 = pl.multiple_of(kb * pr, pr)
                m_old = None if ci == 0 else jnp.concatenate(pieces(m_ref, q_row0, pr, r), axis=0)
                tiles.append((r, i, kb, q_row0, k_row0, m_old))

            def update(r, q_row0, m_old, m_new, l_tile, pv):
                if ci == 0:
                    l_new = l_tile
                    acc_new = pv
                else:
                    alpha = jnp.exp2(m_old - m_new)
                    l_new = alpha * jnp.concatenate(pieces(l_ref, q_row0, pr, r), axis=0) + l_tile
                    acc_new = alpha * jnp.concatenate(pieces(acc_ref, q_row0, pr, r), axis=0) + pv
                for a in range(n_pieces):
                    col = pl.ds(pl.multiple_of((r + dil * a) * HEAD_DIM, HEAD_DIM), HEAD_DIM)
                    rows = slice(a * pr, (a + 1) * pr)
                    m_ref[pl.ds(q_row0, pr), col] = m_new[rows]
                    l_ref[pl.ds(q_row0, pr), col] = l_new[rows]
                    acc_ref[pl.ds(q_row0, pr), col] = acc_new[rows]

            scores, maxes, probs, results = {}, {}, {}, []
            for t in range(DIL_UNROLL + 4):
                if t < DIL_UNROLL:
                    (r, i, kb, q_row0, k_row0, m_old) = tiles[t]
                    q = jnp.concatenate(pieces(q_ref, q_row0, pr, r), axis=0).astype(BF16)
                    k = jnp.concatenate(pieces(k_ref, k_row0, kr, r), axis=0).astype(BF16)
                    scores[t] = _dot_nt(q, k) + bias_ref[2 * ci + i - kb]
                if 0 <= t - 1 < DIL_UNROLL:
                    m_tile = jnp.max(scores[t - 1], axis=1, keepdims=True)
                    m_old = tiles[t - 1][5]
                    maxes[t - 1] = jnp.broadcast_to(m_tile, (BLK, LANES)) if ci == 0 else jnp.maximum(m_old, m_tile)
                if 0 <= t - 2 < DIL_UNROLL:
                    m_new = maxes[t - 2]
                    probs[t - 2] = jnp.exp2(scores.pop(t - 2) - jnp.concatenate([m_new, m_new], axis=1))
                if 0 <= t - 3 < DIL_UNROLL:
                    (r, i, kb, q_row0, k_row0, m_old) = tiles[t - 3]
                    p = probs.pop(t - 3)
                    v = jnp.concatenate(pieces(v_ref, k_row0, kr, r), axis=0).astype(BF16)
                    v_sum = jnp.concatenate([v, jnp.ones((2 * BLK, LANES), BF16)], axis=1)
                    pv = _dot(p.astype(BF16), v_sum)
                    results.append((r, q_row0, m_old, maxes.pop(t - 3), pv[:, HEAD_DIM:], pv[:, :HEAD_DIM]))
                if 0 <= t - 4 < DIL_UNROLL:
                    update(*results[t - 4])
            return carry

        lax.fori_loop(0, seq // (BLK * DIL_UNROLL), body, 0)
    o_ref[...] = acc_ref[...] / l_ref[...]


def dilated_attention(pf, batch, seq):
    assert all(DIL_RES % d == 0 for _, d in DILATED_CONFIGS)
    assert seq % (BLK * DIL_RES) == 0 and seq // DIL_RES >= 2 * BLK and (seq // BLK) % DIL_UNROLL == 0
    rows = seq // DIL_RES
    width = DIL_RES * HEAD_DIM
    view = pf

    def spec(off):
        return pl.BlockSpec((None, None, rows, width), lambda b, h: (off + h, b, 0, 0))

    out = pl.pallas_call(
        functools.partial(_dilated_kernel, seq=seq),
        grid=(batch, HEADS_A),
        in_specs=[spec(PF_QA), spec(PF_KA), spec(PF_VA)],
        out_specs=spec(0),
        out_shape=jax.ShapeDtypeStruct((HEADS_A, batch, rows, width), F32),
        scratch_shapes=[pltpu.VMEM((rows, width), F32)] * 3
        + [pltpu.VMEM((2 * len(DILATED_CONFIGS), BLK, 2 * BLK), F32)],
        compiler_params=_params("parallel", "parallel"),
        name="dilated_attention",
    )(view, view, view)
    return out


def _compress_kernel(t_ref, pe_ref, w1_ref, w2_ref, o_ref, ot_ref, *, seq):
    n_chunks = seq // CMP_STRIDE
    first = jnp.zeros((n_chunks, HEAD_DIM), F32)
    second = jnp.zeros((n_chunks, HEAD_DIM), F32)
    for i in range(CMP_STRIDE):
        ti = t_ref[:, i * HEAD_DIM:(i + 1) * HEAD_DIM]
        first += _dot((ti + pe_ref[pl.ds(i, 1), :]).astype(BF16), w1_ref[i])
        second += _dot((ti + pe_ref[pl.ds(CMP_STRIDE + i, 1), :]).astype(BF16), w1_ref[CMP_STRIDE + i])
    pre = first + pltpu.roll(second, n_chunks - 1, axis=0)
    out = _dot(jax.nn.gelu(pre).astype(BF16), w2_ref[...])
    o_ref[...] = out.astype(o_ref.dtype)
    ot_ref[...] = out.T.astype(ot_ref.dtype)


def compress(pf, pe, w1, w2, batch, seq):
    assert CMP_LEN == 2 * CMP_STRIDE and CMP_STRIDE == VIEW_RES
    n_chunks = seq // CMP_STRIDE
    pf4 = pf
    bg = batch * KV_GROUPS_B
    return pl.pallas_call(
        functools.partial(_compress_kernel, seq=seq),
        grid=(2, batch, KV_GROUPS_B),
        in_specs=[pl.BlockSpec((None, None, n_chunks, VIEW_RES * HEAD_DIM),
                               lambda kv, b, g: (PF_KC + kv * KV_GROUPS_B + g, b, 0, 0)),
                  pl.BlockSpec((None, CMP_LEN, HEAD_DIM), lambda kv, b, g: (kv, 0, 0)),
                  pl.BlockSpec((None, CMP_LEN, HEAD_DIM, HEAD_DIM), lambda kv, b, g: (kv, 0, 0, 0)),
                  pl.BlockSpec((None, HEAD_DIM, HEAD_DIM), lambda kv, b, g: (kv, 0, 0))],
        out_specs=[pl.BlockSpec((None, None, n_chunks, HEAD_DIM), lambda kv, b, g: (kv, b * KV_GROUPS_B + g, 0, 0)),
                   pl.BlockSpec((None, None, HEAD_DIM, n_chunks), lambda kv, b, g: (kv, b * KV_GROUPS_B + g, 0, 0))],
        out_shape=[jax.ShapeDtypeStruct((2, bg, n_chunks, HEAD_DIM), BF16),
                   jax.ShapeDtypeStruct((2, bg, HEAD_DIM, n_chunks), BF16)],
        compiler_params=_params("parallel", "parallel", "parallel"),
        name="compress",
    )(pf4, pe, w1, w2)


NSA_TQ = 512
NSA_TK = 512


def _select_blocks(score):
    n_s, tq = score.shape
    groups = n_s // 8
    rows8 = [score[8 * v:8 * v + 8, :] for v in range(groups)]
    rank8 = [jnp.zeros((8, tq), F32) for _ in range(groups)]
    sub = lax.broadcasted_iota(jnp.int32, (8, 1), 0)
    for jp in range(n_s):
        vp, sp = divmod(jp, 8)
        row = jnp.broadcast_to(rows8[vp][sp:sp + 1, :], (8, tq))
        for v in range(groups):
            if v > vp:
                beats = jnp.where(row >= rows8[v], 1.0, 0.0)
            elif v < vp:
                beats = jnp.where(row > rows8[v], 1.0, 0.0)
            else:
                beats = jnp.where(sub > sp, jnp.where(row >= rows8[v], 1.0, 0.0),
                                  jnp.where(row > rows8[v], 1.0, 0.0))
            rank8[v] = rank8[v] + beats
    rank = jnp.concatenate(rank8, axis=0)
    return (rank < N_SELECT) & (score > -jnp.inf)


SUM_ROWS = 16


def _with_sum_rows(vt):
    return jnp.concatenate([vt, jnp.ones((SUM_ROWS, vt.shape[1]), vt.dtype)], axis=0)


def _flash_block(s, vt_sum, carry):
    m_i, acc = carry
    m_new = jnp.maximum(m_i, jnp.max(s, axis=0, keepdims=True))
    alpha = jnp.exp2(m_i - m_new)
    p = jnp.exp2(s - m_new)
    return m_new, alpha * acc + _dot(vt_sum, p.astype(BF16))


def _normalized(acc):
    return acc[:HEAD_DIM] * (1.0 / acc[HEAD_DIM:HEAD_DIM + 1])


def _nsa_kernel(q_ref, kc_ref, vct_ref, ks_ref, vst_ref, kw_ref, vwt_ref, e_ref, gl_ref, o_ref, *, seq):
    tq, tk = NSA_TQ, NSA_TK
    cols = REP_B * tq
    n_cp = seq // CMP_STRIDE
    n_s = seq // SLC_BLOCK
    qi = pl.program_id(2)
    t0 = qi * tq
    q_t = jnp.concatenate([q_ref[r] for r in range(REP_B)], axis=1)
    lane = lax.broadcasted_iota(jnp.int32, (1, tq), 1)
    tpos = t0 + lane

    def per_head(x):
        return jnp.concatenate([x] * REP_B, axis=1)

    csub = lax.broadcasted_iota(jnp.int32, (n_cp, 1), 0)
    c_ok = (csub * CMP_STRIDE + (CMP_LEN - 1) <= tpos) & (csub < n_cp - 1)
    sc = _dot(kc_ref[...], q_t) + per_head(jnp.where(c_ok, 0.0, NEG))
    e = jnp.exp2(sc - jnp.max(sc, axis=0, keepdims=True))
    has_block = per_head(jnp.where(tpos >= CMP_LEN - 1, 1.0, 0.0))
    p_cmp = e * (has_block / jnp.maximum(jnp.sum(e, axis=0, keepdims=True), 1e-30))
    o_cmp = _dot(vct_ref[...], p_cmp.astype(BF16))

    ws = WIN + tq
    w0 = pl.multiple_of(jnp.maximum(t0 - WIN, 0), tq)
    dist = tpos - (w0 + lax.broadcasted_iota(jnp.int32, (ws, 1), 0))
    win_bias = per_head(jnp.where((dist >= 0) & (dist <= WIN - 1), 0.0, NEG))
    s_w = _dot(kw_ref[pl.ds(w0, ws), :], q_t) + win_bias
    p_w = jnp.exp2(s_w - jnp.max(s_w, axis=0, keepdims=True))
    o_win = _normalized(_dot(_with_sum_rows(vwt_ref[:, pl.ds(w0, ws)]), p_w.astype(BF16)))

    p_sum = p_cmp[:, 0:tq]
    for r in range(1, REP_B):
        p_sum = p_sum + p_cmp[:, r * tq:(r + 1) * tq]
    jj = lax.broadcasted_iota(jnp.int32, (n_s, 1), 0)
    cidx = lax.broadcasted_iota(jnp.int32, (1, n_cp), 1)
    ratio = SLC_BLOCK // CMP_STRIDE
    c_first = ratio * jj - (CMP_LEN // CMP_STRIDE - 1)
    hits = jnp.where((cidx >= c_first) & (cidx < ratio * (jj + 1)), 1.0, 0.0).astype(BF16)
    p_hi = p_sum.astype(BF16)
    p_lo = (p_sum - p_hi.astype(F32)).astype(BF16)
    imp = _dot(hits, p_hi) + _dot(hits, p_lo)
    qblk = tpos >> (SLC_BLOCK.bit_length() - 1)
    forced = (jj == 0) | (jj == qblk) | (jj == qblk - 1)
    valid = jj * SLC_BLOCK <= tpos
    score = jnp.where(forced, FORCE_SCORE, jnp.where(valid, imp, -jnp.inf))
    sel = _select_blocks(score)
    bias = jnp.where(sel, 0.0, SEL_BIAS)
    if n_s < LANES:
        bias = jnp.concatenate([bias, jnp.full((LANES - n_s, tq), SEL_BIAS, F32)], axis=0)
    q_aug = jnp.concatenate([q_t, per_head(bias.astype(BF16))], axis=0)

    def slc_scores(kt):
        k0 = pl.multiple_of(kt * tk, tk)
        k_aug = jnp.concatenate([ks_ref[pl.ds(k0, tk), :], e_ref[pl.ds(k0, tk), :]], axis=1)
        return _dot(k_aug, q_aug)

    def slc_values(kt):
        return _with_sum_rows(vst_ref[:, pl.ds(pl.multiple_of(kt * tk, tk), tk)])

    def slc_step(kt, stats):
        return _flash_block(slc_scores(kt), slc_values(kt), stats)

    last = (t0 + tq - 1) // tk
    stats = (jnp.full((1, cols), NEG, F32), jnp.zeros((HEAD_DIM + SUM_ROWS, cols), F32))
    stats = lax.fori_loop(0, last, slc_step, stats)
    tok = last * tk + lax.broadcasted_iota(jnp.int32, (tk, 1), 0)
    s_last = slc_scores(last) + per_head(jnp.where(tok <= tpos, 0.0, NEG))
    o_slc = _normalized(_flash_block(s_last, slc_values(last), stats)[1])

    gates = jax.nn.sigmoid(gl_ref[...].T)
    for r in range(REP_B):
        sl = slice(r * tq, (r + 1) * tq)
        merged = (gates[3 * r:3 * r + 1, :] * o_cmp[:, sl]
                  + gates[3 * r + 1:3 * r + 2, :] * o_slc[:, sl]
                  + gates[3 * r + 2:3 * r + 3, :] * o_win[:, sl])
        o_ref[:, r * HEAD_DIM:(r + 1) * HEAD_DIM] = merged.T


def nsa_attention(pr, pc, kvc, kvct, gl, batch, seq):
    tq = NSA_TQ
    n_s = seq // SLC_BLOCK
    assert seq % NSA_TK == 0 and NSA_TK % tq == 0 and seq >= WIN + tq and n_s <= LANES and n_s % 8 == 0
    assert SLC_BLOCK & (SLC_BLOCK - 1) == 0
    n_cp = seq // CMP_STRIDE
    nq = seq // tq
    pr4 = pr.reshape(pr.shape[0], batch, seq, HEAD_DIM)
    block_onehot = (jnp.arange(seq)[:, None] // SLC_BLOCK == jnp.arange(LANES)[None, :]).astype(BF16)

    def k_spec(off):
        return pl.BlockSpec((None, None, seq, HEAD_DIM), lambda b, g, i: (off + g, b, 0, 0))

    def vt_spec(off):
        return pl.BlockSpec((None, None, HEAD_DIM, seq), lambda b, g, i: (off + g, b, 0, 0))

    return pl.pallas_call(
        functools.partial(_nsa_kernel, seq=seq),
        grid=(batch, KV_GROUPS_B, nq),
        in_specs=[pl.BlockSpec((REP_B, None, HEAD_DIM, tq), lambda b, g, i: (PC_QB // REP_B + g, b, 0, i)),
                  pl.BlockSpec((None, None, n_cp, HEAD_DIM), lambda b, g, i: (0, b * KV_GROUPS_B + g, 0, 0)),
                  pl.BlockSpec((None, None, HEAD_DIM, n_cp), lambda b, g, i: (1, b * KV_GROUPS_B + g, 0, 0)),
                  k_spec(PR_KS), vt_spec(PC_VS), k_spec(PR_KW), vt_spec(PC_VW),
                  pl.BlockSpec((seq, LANES), lambda b, g, i: (0, 0)),
                  pl.BlockSpec((tq, LANES), lambda b, g, i: (b * nq + i, g))],
        out_specs=pl.BlockSpec((tq, REP_B * HEAD_DIM), lambda b, g, i: (b * nq + i, g)),
        out_shape=jax.ShapeDtypeStruct((batch * seq, WIDTH_B), F32),
        compiler_params=_params("parallel", "parallel", "arbitrary"),
        name="nsa_attention",
    )(pc, kvc, kvct, pr4, pc, pr4, pc, block_onehot, gl)


def _mm_res_kernel(a_ref, w_ref, r_ref, o_ref):
    o_ref[...] = _dot(a_ref[...], w_ref[...]) + r_ref[...]


def matmul_residual(a, w, res, tm, tn):
    m, k = a.shape
    n = w.shape[1]
    return pl.pallas_call(
        _mm_res_kernel,
        grid=(m // tm, n // tn),
        in_specs=[pl.BlockSpec((tm, k), lambda i, j: (i, 0)),
                  pl.BlockSpec((k, tn), lambda i, j: (0, j)),
                  pl.BlockSpec((tm, tn), lambda i, j: (i, j))],
        out_specs=pl.BlockSpec((tm, tn), lambda i, j: (i, j)),
        out_shape=jax.ShapeDtypeStruct((m, n), F32),
        compiler_params=_params("parallel", "arbitrary"),
        name="matmul_residual",
    )(a, w, res)


def _ffn_up_kernel(h_ref, wg_ref, wu_ref, o_ref):
    h = h_ref[...]
    g = _dot(h, wg_ref[...].astype(BF16))
    u = _dot(h, wu_ref[...].astype(BF16))
    o_ref[...] = (g * jax.nn.sigmoid(g) * u).astype(o_ref.dtype)


def ffn_up(h, wg, wu, layer, tm, tn):
    m, k = h.shape
    n = wg.shape[2]
    return pl.pallas_call(
        _ffn_up_kernel,
        grid=(m // tm, n // tn),
        in_specs=[pl.BlockSpec((tm, k), lambda i, j: (i, 0)),
                  pl.BlockSpec((None, k, tn), lambda i, j: (layer, 0, j)),
                  pl.BlockSpec((None, k, tn), lambda i, j: (layer, 0, j))],
        out_specs=pl.BlockSpec((tm, tn), lambda i, j: (i, j)),
        out_shape=jax.ShapeDtypeStruct((m, n), BF16),
        compiler_params=_params("parallel", "arbitrary"),
        name="ffn_up",
    )(h, wg, wu)


def _cast_kernel(w_ref, o_ref):
    o_ref[...] = w_ref[...].astype(o_ref.dtype)


def cast_columns(w, layer, n_cols, tk=512, tn=2816):
    _, k, _ = w.shape
    return pl.pallas_call(
        _cast_kernel,
        grid=(k // tk, n_cols // tn),
        in_specs=[pl.BlockSpec((None, tk, tn), lambda i, j: (layer, i, j))],
        out_specs=pl.BlockSpec((tk, tn), lambda i, j: (i, j)),
        out_shape=jax.ShapeDtypeStruct((k, n_cols), BF16),
        compiler_params=_params("parallel", "parallel"),
        name="cast_columns",
    )(w)


def _layer(x, norm_attn, w_in, layer, ck_pe, ck_w1, ck_w2, cv_pe, cv_w1, cv_w2,
           out_norm_a, out_norm_b, w_out, norm_ffn, w_gate, w_up, w_down, cos_t, sin_t, batch, seq):
    w_main = w_in.astype(BF16)
    w_gl = lax.slice(w_in, (layer, 0, D_MAIN), (layer + 1, D_MODEL, D_MAIN + N_GATES))
    w_gl = w_gl.reshape(D_MODEL, KV_GROUPS_B, REP_B * 3)
    w_gl = jnp.pad(w_gl, ((0, 0), (0, 0), (0, LANES - REP_B * 3))).reshape(D_MODEL, KV_GROUPS_B * LANES).astype(BF16)
    pe = jnp.stack([ck_pe, cv_pe])
    w1 = jnp.stack([ck_w1, cv_w1]).reshape(2, CMP_LEN, HEAD_DIM, HEAD_DIM).astype(BF16)
    w2 = jnp.stack([ck_w2, cv_w2]).astype(BF16)

    h, gl = rms_gate(x, norm_attn, w_gl)
    pf = project(h, w_main, layer, cos_t, sin_t, F32_TILES, F32, batch, seq, layout="view")
    pr = project(h, w_main, layer, cos_t, sin_t, ROW_TILES, BF16, batch, seq)
    pc = project(h, w_main, layer, cos_t, sin_t, COL_TILES, BF16, batch, seq, layout="cols")
    o_a = dilated_attention(pf, batch, seq)
    kvc, kvct = compress(pf, pe, w1, w2, batch, seq)
    o_b = nsa_attention(pr, pc, kvc, kvct, gl, batch, seq)
    mixed = rms_pair(o_a, o_b, out_norm_a, out_norm_b)
    x1 = matmul_residual(mixed, w_out.astype(BF16), x, tm=1024, tn=1024)
    h2 = rms(x1, norm_ffn, BF16)
    act = ffn_up(h2, w_gate, w_up, layer, tm=2048, tn=256)
    x2 = matmul_residual(act, w_down.astype(BF16), x1, tm=512, tn=512)
    return x2


def kernel(x, norm_attn, w_in, ck_pe, ck_w1, ck_w2, cv_pe, cv_w1, cv_w2, out_norm_a, out_norm_b, w_out,
           norm_ffn, w_gate, w_up, w_down, norm_final):
    batch, seq, d = x.shape
    depth = w_in.shape[0]
    cos_t, sin_t = _rope_tables(seq)
    xf = x.reshape(batch * seq, d)
    for l in range(depth):
        xf = _layer(xf, norm_attn[l], w_in, l, ck_pe[l], ck_w1[l], ck_w2[l], cv_pe[l], cv_w1[l], cv_w2[l],
                    out_norm_a[l], out_norm_b[l], w_out[l], norm_ffn[l], w_gate, w_up, w_down[l],
                    cos_t, sin_t, batch, seq)
    return rms(xf, norm_final, F32).reshape(batch, seq, d)
```

```python
import functools
import math

import jax
import jax.numpy as jnp
from jax import lax
from jax.experimental import pallas as pl
from jax.experimental.pallas import tpu as pltpu

F32 = jnp.float32
BF16 = jnp.bfloat16

D_MODEL = 4096
HEAD_DIM = 128
HEADS_A = 16
HEADS_B = 16
KV_GROUPS_B = 4
REP_B = HEADS_B // KV_GROUPS_B
WIDTH_A = HEADS_A * HEAD_DIM
WIDTH_B = HEADS_B * HEAD_DIM
DILATED_CONFIGS = ((128, 1), (512, 4), (2048, 16))
BLK = 128
CMP_LEN = 32
CMP_STRIDE = 16
SLC_BLOCK = 64
N_SELECT = 16
WIN = 512
FORCE_SCORE = 1e6
ROPE_THETA = 500000.0
ROPE_DIM = HEAD_DIM // 4
EPS = 1e-5
Q_SCALE = HEAD_DIM ** -0.5 * math.log2(math.e)
N_GATES = 3 * HEADS_B
D_MAIN = 3 * WIDTH_A + WIDTH_B + 6 * KV_GROUPS_B * HEAD_DIM

VMEM_LIMIT_BYTES = 56 * 1024 * 1024
LANES = 128
NEG = -1e30
SEL_BIAS = -32768.0

PROJ_TILE = 4 * HEAD_DIM
F32_TILES = tuple(range(12)) + (16, 17)
ROW_TILES = (18, 20)
COL_TILES = (12, 13, 14, 15, 19, 21)
Q_TILES = (0, 1, 2, 3, 12, 13, 14, 15)
V_TILES = (8, 9, 10, 11, 17, 19, 21)
PF_QA, PF_KA, PF_VA, PF_KC, PF_VC = 0, 16, 32, 48, 52
PR_KS, PR_KW = 0, 4
PC_QB, PC_VS, PC_VW = 0, 16, 20


def _params(*sem):
    return pltpu.CompilerParams(dimension_semantics=sem, vmem_limit_bytes=VMEM_LIMIT_BYTES)


def _dot(a, b):
    return jnp.dot(a, b, preferred_element_type=F32)


def _dot_nt(a, b):
    return lax.dot_general(a, b, (((1,), (1,)), ((), ())), preferred_element_type=F32)


def _rms(x, g):
    return x * lax.rsqrt(jnp.mean(x * x, axis=-1, keepdims=True) + EPS) * g


def _rms_gate_kernel(x_ref, g_ref, wgl_ref, h_ref, gl_ref):
    hb = _rms(x_ref[...], g_ref[...]).astype(BF16)
    h_ref[...] = hb
    gl_ref[...] = _dot(hb, wgl_ref[...])


def rms_gate(x, gain, w_gl, tm=512):
    m, d = x.shape
    ng = w_gl.shape[1]
    return pl.pallas_call(
        _rms_gate_kernel,
        grid=(m // tm,),
        in_specs=[pl.BlockSpec((tm, d), lambda i: (i, 0)),
                  pl.BlockSpec((1, d), lambda i: (0, 0)),
                  pl.BlockSpec((d, ng), lambda i: (0, 0))],
        out_specs=[pl.BlockSpec((tm, d), lambda i: (i, 0)),
                   pl.BlockSpec((tm, ng), lambda i: (i, 0))],
        out_shape=[jax.ShapeDtypeStruct((m, d), BF16), jax.ShapeDtypeStruct((m, ng), F32)],
        compiler_params=_params("parallel"),
        name="rms_gate",
    )(x, gain.reshape(1, d), w_gl)


def _rms_kernel(x_ref, g_ref, o_ref):
    o_ref[...] = _rms(x_ref[...], g_ref[...]).astype(o_ref.dtype)


def rms(x, gain, out_dtype, tm=512):
    m, d = x.shape
    return pl.pallas_call(
        _rms_kernel,
        grid=(m // tm,),
        in_specs=[pl.BlockSpec((tm, d), lambda i: (i, 0)),
                  pl.BlockSpec((1, d), lambda i: (0, 0))],
        out_specs=pl.BlockSpec((tm, d), lambda i: (i, 0)),
        out_shape=jax.ShapeDtypeStruct((m, d), out_dtype),
        compiler_params=_params("parallel"),
        name="rms",
    )(x, gain.reshape(1, d))


def _rms_pair_kernel(a_ref, b_ref, ga_ref, gb_ref, o_ref):
    n_heads, nv, _ = a_ref.shape
    wa = n_heads * HEAD_DIM
    lanes = [slice(r * HEAD_DIM, (r + 1) * HEAD_DIM) for r in range(VIEW_RES)]
    inv = []
    for r in range(VIEW_RES):
        sq = a_ref[0, :, lanes[r]] * a_ref[0, :, lanes[r]]
        for hh in range(1, n_heads):
            sq = sq + a_ref[hh, :, lanes[r]] * a_ref[hh, :, lanes[r]]
        inv.append(lax.rsqrt(jnp.sum(sq, axis=-1, keepdims=True) * (1.0 / wa) + EPS))
    for hh in range(n_heads):
        by_res = jnp.stack([a_ref[hh, :, lanes[r]] * inv[r] for r in range(VIEW_RES)], axis=0)
        tokens = jnp.swapaxes(by_res, 0, 1).reshape(nv * VIEW_RES, HEAD_DIM)
        cols = slice(hh * HEAD_DIM, (hh + 1) * HEAD_DIM)
        o_ref[:, cols] = (tokens * ga_ref[:, cols]).astype(o_ref.dtype)
    o_ref[:, wa:] = _rms(b_ref[...], gb_ref[...]).astype(o_ref.dtype)


def rms_pair(a, b, ga, gb, tm=512):
    n_heads, batch, rows, width = a.shape
    assert width == VIEW_RES * HEAD_DIM and (rows * VIEW_RES) % tm == 0
    wa = n_heads * HEAD_DIM
    m, wb = b.shape
    tiles_per_seq = rows * VIEW_RES // tm
    return pl.pallas_call(
        _rms_pair_kernel,
        grid=(m // tm,),
        in_specs=[pl.BlockSpec((n_heads, None, tm // VIEW_RES, width),
                               lambda i: (0, i // tiles_per_seq, i % tiles_per_seq, 0)),
                  pl.BlockSpec((tm, wb), lambda i: (i, 0)),
                  pl.BlockSpec((1, wa), lambda i: (0, 0)),
                  pl.BlockSpec((1, wb), lambda i: (0, 0))],
        out_specs=pl.BlockSpec((tm, wa + wb), lambda i: (i, 0)),
        out_shape=jax.ShapeDtypeStruct((m, wa + wb), BF16),
        compiler_params=_params("parallel"),
        name="rms_pair",
    )(a, b, ga.reshape(1, wa), gb.reshape(1, wb))


def _rope_tables(seq):
    inv = ROPE_THETA ** (-jnp.arange(0, ROPE_DIM, 2, dtype=F32) / ROPE_DIM)
    ang = jnp.arange(seq, dtype=F32)[:, None] * inv[None, :]
    cos, sin = jnp.cos(ang), jnp.sin(ang)
    ones = jnp.ones((seq, HEAD_DIM - ROPE_DIM), F32)
    cos_t = jnp.concatenate([cos, cos, ones], axis=1)
    sin_t = jnp.concatenate([-sin, sin, 0.0 * ones], axis=1)
    return cos_t, sin_t


PROJ_ROWS = 256
VIEW_RES = DILATED_CONFIGS[-1][1]


def _proj_kernel(h_ref, w_ref, cos_ref, sin_ref, o_ref, *, src_tiles, layout):
    jt = pl.program_id(1)
    is_v = functools.reduce(jnp.logical_or, [jt == n for n, t in enumerate(src_tiles) if t in V_TILES], False)
    is_q = functools.reduce(jnp.logical_or, [jt == n for n, t in enumerate(src_tiles) if t in Q_TILES], False)
    heads_per_tile = PROJ_TILE // HEAD_DIM
    tm = h_ref.shape[0]
    scale = jnp.where(is_q, Q_SCALE, 1.0).astype(F32)
    lane = lax.broadcasted_iota(jnp.int32, (1, HEAD_DIM), 1)
    first_half = lane < ROPE_DIM // 2

    for c in range(tm // PROJ_ROWS):
        rows = slice(c * PROJ_ROWS, (c + 1) * PROJ_ROWS)
        y = _dot(h_ref[rows, :], w_ref[...])
        cos_t = jnp.where(is_v, 1.0, cos_ref[rows, :]) * scale
        sin_t = jnp.where(is_v, 0.0, sin_ref[rows, :]) * scale
        for hh in range(heads_per_tile):
            yh = y[:, hh * HEAD_DIM:(hh + 1) * HEAD_DIM]
            partner = jnp.where(first_half,
                                pltpu.roll(yh, HEAD_DIM - ROPE_DIM // 2, axis=1),
                                pltpu.roll(yh, ROPE_DIM // 2, axis=1))
            val = yh * cos_t + partner * sin_t
            if layout == "cols":
                for cc in range(PROJ_ROWS // LANES):
                    col0 = c * PROJ_ROWS + cc * LANES
                    o_ref[hh, :, col0:col0 + LANES] = val[cc * LANES:(cc + 1) * LANES, :].T.astype(o_ref.dtype)
            elif layout == "view":
                nv = PROJ_ROWS // VIEW_RES
                by_res = jnp.swapaxes(val.reshape(nv, VIEW_RES, HEAD_DIM), 0, 1)
                for r in range(VIEW_RES):
                    o_ref[hh, c * nv:(c + 1) * nv, r * HEAD_DIM:(r + 1) * HEAD_DIM] = by_res[r].astype(o_ref.dtype)
            else:
                o_ref[hh, rows, :] = val.astype(o_ref.dtype)


def project(h, w, layer, cos_t, sin_t, src_tiles, out_dtype, batch, seq, layout="rows", tm=2048):
    m, d = h.shape
    n_tiles = len(src_tiles)
    heads_per_tile = PROJ_TILE // HEAD_DIM
    tm = min(tm, seq)
    pos_blocks = seq // tm

    def w_map(i, j):
        col = functools.reduce(lambda acc, nt: jnp.where(j == nt[0], nt[1], acc),
                               list(enumerate(src_tiles)), 0)
        return (layer, 0, col)

    if layout == "cols":
        out_spec = pl.BlockSpec((heads_per_tile, None, HEAD_DIM, tm),
                                lambda i, j: (j, i // pos_blocks, 0, i % pos_blocks))
        out_shape = jax.ShapeDtypeStruct((n_tiles * heads_per_tile, batch, HEAD_DIM, seq), out_dtype)
    elif layout == "view":
        out_spec = pl.BlockSpec((heads_per_tile, None, tm // VIEW_RES, VIEW_RES * HEAD_DIM),
                                lambda i, j: (j, i // pos_blocks, i % pos_blocks, 0))
        out_shape = jax.ShapeDtypeStruct((n_tiles * heads_per_tile, batch, seq // VIEW_RES, VIEW_RES * HEAD_DIM),
                                         out_dtype)
    else:
        out_spec = pl.BlockSpec((heads_per_tile, tm, HEAD_DIM), lambda i, j: (j, i, 0))
        out_shape = jax.ShapeDtypeStruct((n_tiles * heads_per_tile, m, HEAD_DIM), out_dtype)

    return pl.pallas_call(
        functools.partial(_proj_kernel, src_tiles=src_tiles, layout=layout),
        grid=(m // tm, n_tiles),
        in_specs=[pl.BlockSpec((tm, d), lambda i, j: (i, 0)),
                  pl.BlockSpec((None, d, PROJ_TILE), w_map),
                  pl.BlockSpec((tm, HEAD_DIM), lambda i, j: (i % pos_blocks, 0)),
                  pl.BlockSpec((tm, HEAD_DIM), lambda i, j: (i % pos_blocks, 0))],
        out_specs=out_spec,
        out_shape=out_shape,
        compiler_params=_params("parallel", "arbitrary"),
        name="project_" + layout,
    )(h, w, cos_t, sin_t)


DIL_UNROLL = 8
DIL_RES = VIEW_RES


def _dilated_kernel(q_ref, k_ref, v_ref, o_ref, acc_ref, m_ref, l_ref, bias_ref, *, seq):
    for ci, (window, dil) in enumerate(DILATED_CONFIGS):
        band = window // dil
        n_pieces = DIL_RES // dil
        pr = BLK // n_pieces
        kr = 2 * pr
        tiles_per_res = seq // (BLK * dil)
        assert band <= BLK and pr % 8 == 0 and pr & (pr - 1) == 0

        q_row = lax.broadcasted_iota(jnp.int32, (BLK, 1), 0)
        k_row = lax.broadcasted_iota(jnp.int32, (1, 2 * BLK), 1)
        q_sub = n_pieces * (q_row & (pr - 1)) + (q_row >> (pr.bit_length() - 1))
        k_sub = n_pieces * (k_row & (kr - 1)) + (k_row >> (kr.bit_length() - 1))
        for lead in range(2):
            dist = (q_sub + lead * BLK) - k_sub
            bias_ref[2 * ci + lead] = jnp.where((dist >= 0) & (dist <= band), 0.0, NEG)

        def body(step, carry, ci=ci, dil=dil, band=band, n_pieces=n_pieces, pr=pr, kr=kr,
                 tiles_per_res=tiles_per_res):
            def pieces(ref, row0, nrows, r):
                return [ref[pl.ds(row0, nrows), pl.ds(pl.multiple_of((r + dil * a) * HEAD_DIM, HEAD_DIM), HEAD_DIM)]
                        for a in range(n_pieces)]

            tiles = []
            for u in range(DIL_UNROLL):
                idx = step * DIL_UNROLL + u
                r = idx // tiles_per_res
                i = idx - r * tiles_per_res
                kb = jnp.maximum(i - 1, 0)
                q_row0 = pl.multiple_of(i * pr, pr)
                k_row0 = pl.multiple_of(kb * pr, pr)
                m_old = None if ci == 0 else jnp.concatenate(pieces(m_ref, q_row0, pr, r), axis=0)
                tiles.append((r, i, kb, q_row0, k_row0, m_old))

            def update(r, q_row0, m_old, m_new, l_tile, pv):
                if ci == 0:
                    l_new = l_tile
                    acc_new = pv
                else:
                    alpha = jnp.exp2(m_old - m_new)
                    l_new = alpha * jnp.concatenate(pieces(l_ref, q_row0, pr, r), axis=0) + l_tile
                    acc_new = alpha * jnp.concatenate(pieces(acc_ref, q_row0, pr, r), axis=0) + pv
                for a in range(n_pieces):
                    col = pl.ds(pl.multiple_of((r + dil * a) * HEAD_DIM, HEAD_DIM), HEAD_DIM)
                    rows = slice(a * pr, (a + 1) * pr)
                    m_ref[pl.ds(q_row0, pr), col] = m_new[rows]
                    l_ref[pl.ds(q_row0, pr), col] = l_new[rows]
                    acc_ref[pl.ds(q_row0, pr), col] = acc_new[rows]

            scores, maxes, probs, results = {}, {}, {}, []
            for t in range(DIL_UNROLL + 4):
                if t < DIL_UNROLL:
                    (r, i, kb, q_row0, k_row0, m_old) = tiles[t]
                    q = jnp.concatenate(pieces(q_ref, q_row0, pr, r), axis=0).astype(BF16)
                    k = jnp.concatenate(pieces(k_ref, k_row0, kr, r), axis=0).astype(BF16)
                    scores[t] = _dot_nt(q, k) + bias_ref[2 * ci + i - kb]
                if 0 <= t - 1 < DIL_UNROLL:
                    m_tile = jnp.max(scores[t - 1], axis=1, keepdims=True)
                    m_old = tiles[t - 1][5]
                    maxes[t - 1] = jnp.broadcast_to(m_tile, (BLK, LANES)) if ci == 0 else jnp.maximum(m_old, m_tile)
                if 0 <= t - 2 < DIL_UNROLL:
                    m_new = maxes[t - 2]
                    probs[t - 2] = jnp.exp2(scores.pop(t - 2) - jnp.concatenate([m_new, m_new], axis=1))
                if 0 <= t - 3 < DIL_UNROLL:
                    (r, i, kb, q_row0, k_row0, m_old) = tiles[t - 3]
                    p = probs.pop(t - 3)
                    v = jnp.concatenate(pieces(v_ref, k_row0, kr, r), axis=0).astype(BF16)
                    v_sum = jnp.concatenate([v, jnp.ones((2 * BLK, LANES), BF16)], axis=1)
                    pv = _dot(p.astype(BF16), v_sum)
                    results.append((r, q_row0, m_old, maxes.pop(t - 3), pv[:, HEAD_DIM:], pv[:, :HEAD_DIM]))
                if 0 <= t - 4 < DIL_UNROLL:
                    update(*results[t - 4])
            return carry

        lax.fori_loop(0, seq // (BLK * DIL_UNROLL), body, 0)
    o_ref[...] = acc_ref[...] / l_ref[...]


def dilated_attention(pf, batch, seq):
    assert all(DIL_RES % d == 0 for _, d in DILATED_CONFIGS)
    assert seq % (BLK * DIL_RES) == 0 and seq // DIL_RES >= 2 * BLK and (seq // BLK) % DIL_UNROLL == 0
    rows = seq // DIL_RES
    width = DIL_RES * HEAD_DIM
    view = pf

    def spec(off):
        return pl.BlockSpec((None, None, rows, width), lambda b, h: (off + h, b, 0, 0))

    out = pl.pallas_call(
        functools.partial(_dilated_kernel, seq=seq),
        grid=(batch, HEADS_A),
        in_specs=[spec(PF_QA), spec(PF_KA), spec(PF_VA)],
        out_specs=spec(0),
        out_shape=jax.ShapeDtypeStruct((HEADS_A, batch, rows, width), F32),
        scratch_shapes=[pltpu.VMEM((rows, width), F32)] * 3
        + [pltpu.VMEM((2 * len(DILATED_CONFIGS), BLK, 2 * BLK), F32)],
        compiler_params=_params("parallel", "parallel"),
        name="dilated_attention",
    )(view, view, view)
    return out


def _compress_kernel(t_ref, pe_ref, w1_ref, w2_ref, o_ref, ot_ref, *, seq):
    n_chunks = seq // CMP_STRIDE
    first = jnp.zeros((n_chunks, HEAD_DIM), F32)
    second = jnp.zeros((n_chunks, HEAD_DIM), F32)
    for i in range(CMP_STRIDE):
        ti = t_ref[:, i * HEAD_DIM:(i + 1) * HEAD_DIM]
        first += _dot((ti + pe_ref[pl.ds(i, 1), :]).astype(BF16), w1_ref[i])
        second += _dot((ti + pe_ref[pl.ds(CMP_STRIDE + i, 1), :]).astype(BF16), w1_ref[CMP_STRIDE + i])
    pre = first + pltpu.roll(second, n_chunks - 1, axis=0)
    out = _dot(jax.nn.gelu(pre).astype(BF16), w2_ref[...])
    o_ref[...] = out.astype(o_ref.dtype)
    ot_ref[...] = out.T.astype(ot_ref.dtype)


def compress(pf, pe, w1, w2, batch, seq):
    assert CMP_LEN == 2 * CMP_STRIDE and CMP_STRIDE == VIEW_RES
    n_chunks = seq // CMP_STRIDE
    pf4 = pf
    bg = batch * KV_GROUPS_B
    return pl.pallas_call(
        functools.partial(_compress_kernel, seq=seq),
        grid=(2, batch, KV_GROUPS_B),
        in_specs=[pl.BlockSpec((None, None, n_chunks, VIEW_RES * HEAD_DIM),
                               lambda kv, b, g: (PF_KC + kv * KV_GROUPS_B + g, b, 0, 0)),
                  pl.BlockSpec((None, CMP_LEN, HEAD_DIM), lambda kv, b, g: (kv, 0, 0)),
                  pl.BlockSpec((None, CMP_LEN, HEAD_DIM, HEAD_DIM), lambda kv, b, g: (kv, 0, 0, 0)),
                  pl.BlockSpec((None, HEAD_DIM, HEAD_DIM), lambda kv, b, g: (kv, 0, 0))],
        out_specs=[pl.BlockSpec((None, None, n_chunks, HEAD_DIM), lambda kv, b, g: (kv, b * KV_GROUPS_B + g, 0, 0)),
                   pl.BlockSpec((None, None, HEAD_DIM, n_chunks), lambda kv, b, g: (kv, b * KV_GROUPS_B + g, 0, 0))],
        out_shape=[jax.ShapeDtypeStruct((2, bg, n_chunks, HEAD_DIM), BF16),
                   jax.ShapeDtypeStruct((2, bg, HEAD_DIM, n_chunks), BF16)],
        compiler_params=_params("parallel", "parallel", "parallel"),
        name="compress",
    )(pf4, pe, w1, w2)


NSA_TQ = 512
NSA_TK = 512


def _select_blocks(score):
    n_s, tq = score.shape
    groups = n_s // 8
    rows8 = [score[8 * v:8 * v + 8, :] for v in range(groups)]
    rank8 = [jnp.zeros((8, tq), F32) for _ in range(groups)]
    sub = lax.broadcasted_iota(jnp.int32, (8, 1), 0)
    for jp in range(n_s):
        vp, sp = divmod(jp, 8)
        row = jnp.broadcast_to(rows8[vp][sp:sp + 1, :], (8, tq))
        for v in range(groups):
            if v > vp:
                beats = jnp.where(row >= rows8[v], 1.0, 0.0)
            elif v < vp:
                beats = jnp.where(row > rows8[v], 1.0, 0.0)
            else:
                beats = jnp.where(sub > sp, jnp.where(row >= rows8[v], 1.0, 0.0),
                                  jnp.where(row > rows8[v], 1.0, 0.0))
            rank8[v] = rank8[v] + beats
    rank = jnp.concatenate(rank8, axis=0)
    return (rank < N_SELECT) & (score > -jnp.inf)


SUM_ROWS = 16


def _with_sum_rows(vt):
    return jnp.concatenate([vt, jnp.ones((SUM_ROWS, vt.shape[1]), vt.dtype)], axis=0)


def _flash_block(s, vt_sum, carry):
    m_i, acc = carry
    m_new = jnp.maximum(m_i, jnp.max(s, axis=0, keepdims=True))
    alpha = jnp.exp2(m_i - m_new)
    p = jnp.exp2(s - m_new)
    return m_new, alpha * acc + _dot(vt_sum, p.astype(BF16))


def _normalized(acc):
    return acc[:HEAD_DIM] * (1.0 / acc[HEAD_DIM:HEAD_DIM + 1])


def _nsa_kernel(q_ref, kc_ref, vct_ref, ks_ref, vst_ref, kw_ref, vwt_ref, e_ref, wb_ref, gl_ref, o_ref, *, seq):
    tq, tk = NSA_TQ, NSA_TK
    cols = REP_B * tq
    n_cp = seq // CMP_STRIDE
    n_s = seq // SLC_BLOCK
    qi = pl.program_id(2)
    t0 = qi * tq
    q_t = jnp.concatenate([q_ref[r] for r in range(REP_B)], axis=1)
    lane = lax.broadcasted_iota(jnp.int32, (1, tq), 1)
    tpos = t0 + lane

    def per_head(x):
        return jnp.concatenate([x] * REP_B, axis=1)

    csub = lax.broadcasted_iota(jnp.int32, (n_cp, 1), 0)
    c_ok = (csub * CMP_STRIDE + (CMP_LEN - 1) <= tpos) & (csub < n_cp - 1)
    sc = _dot(kc_ref[...], q_t) + per_head(jnp.where(c_ok, 0.0, NEG))
    e = jnp.exp2(sc - jnp.max(sc, axis=0, keepdims=True))
    has_block = per_head(jnp.where(tpos >= CMP_LEN - 1, 1.0, 0.0))
    p_cmp = e * (has_block / jnp.maximum(jnp.sum(e, axis=0, keepdims=True), 1e-30))
    o_cmp = _dot(vct_ref[...], p_cmp.astype(BF16))

    ws = WIN + tq
    w0 = pl.multiple_of(jnp.maximum(t0 - WIN, 0), tq)
    s_w = _dot(kw_ref[pl.ds(w0, ws), :], q_t) + per_head(wb_ref[...])
    p_w = jnp.exp2(s_w - jnp.max(s_w, axis=0, keepdims=True))
    o_win = _normalized(_dot(_with_sum_rows(vwt_ref[:, pl.ds(w0, ws)]), p_w.astype(BF16)))

    p_sum = p_cmp[:, 0:tq]
    for r in range(1, REP_B):
        p_sum = p_sum + p_cmp[:, r * tq:(r + 1) * tq]
    jj = lax.broadcasted_iota(jnp.int32, (n_s, 1), 0)
    cidx = lax.broadcasted_iota(jnp.int32, (1, n_cp), 1)
    ratio = SLC_BLOCK // CMP_STRIDE
    c_first = ratio * jj - (CMP_LEN // CMP_STRIDE - 1)
    hits = jnp.where((cidx >= c_first) & (cidx < ratio * (jj + 1)), 1.0, 0.0).astype(BF16)
    p_hi = p_sum.astype(BF16)
    p_lo = (p_sum - p_hi.astype(F32)).astype(BF16)
    imp = _dot(hits, p_hi) + _dot(hits, p_lo)
    qblk = tpos >> (SLC_BLOCK.bit_length() - 1)
    forced = (jj == 0) | (jj == qblk) | (jj == qblk - 1)
    valid = jj * SLC_BLOCK <= tpos
    score = jnp.where(forced, FORCE_SCORE, jnp.where(valid, imp, -jnp.inf))
    sel = _select_blocks(score)
    bias = jnp.where(sel, 0.0, SEL_BIAS)
    if n_s < LANES:
        bias = jnp.concatenate([bias, jnp.full((LANES - n_s, tq), SEL_BIAS, F32)], axis=0)
    q_aug = jnp.concatenate([q_t, per_head(bias.astype(BF16))], axis=0)

    def slc_scores(kt):
        k0 = pl.multiple_of(kt * tk, tk)
        k_aug = jnp.concatenate([ks_ref[pl.ds(k0, tk), :], e_ref[pl.ds(k0, tk), :]], axis=1)
        return _dot(k_aug, q_aug)

    def slc_values(kt):
        return _with_sum_rows(vst_ref[:, pl.ds(pl.multiple_of(kt * tk, tk), tk)])

    def slc_step(kt, stats):
        return _flash_block(slc_scores(kt), slc_values(kt), stats)

    last = (t0 + tq - 1) // tk
    stats = (jnp.full((1, cols), NEG, F32), jnp.zeros((HEAD_DIM + SUM_ROWS, cols), F32))
    stats = lax.fori_loop(0, last, slc_step, stats)
    tok = last * tk + lax.broadcasted_iota(jnp.int32, (tk, 1), 0)
    s_last = slc_scores(last) + per_head(jnp.where(tok <= tpos, 0.0, NEG))
    o_slc = _normalized(_flash_block(s_last, slc_values(last), stats)[1])

    gates = jax.nn.sigmoid(gl_ref[...].T)
    for r in range(REP_B):
        sl = slice(r * tq, (r + 1) * tq)
        merged = (gates[3 * r:3 * r + 1, :] * o_cmp[:, sl]
                  + gates[3 * r + 1:3 * r + 2, :] * o_slc[:, sl]
                  + gates[3 * r + 2:3 * r + 3, :] * o_win[:, sl])
        o_ref[:, r * HEAD_DIM:(r + 1) * HEAD_DIM] = merged.T


def nsa_attention(pr, pc, kvc, kvct, gl, batch, seq):
    tq = NSA_TQ
    n_s = seq // SLC_BLOCK
    assert seq % NSA_TK == 0 and NSA_TK % tq == 0 and seq >= WIN + tq and n_s <= LANES and n_s % 8 == 0
    assert WIN % tq == 0
    assert SLC_BLOCK & (SLC_BLOCK - 1) == 0
    n_cp = seq // CMP_STRIDE
    nq = seq // tq
    pr4 = pr.reshape(pr.shape[0], batch, seq, HEAD_DIM)
    block_onehot = (jnp.arange(seq)[:, None] // SLC_BLOCK == jnp.arange(LANES)[None, :]).astype(BF16)
    lead = jnp.arange(WIN // tq + 1)[:, None, None] * tq
    dist = lead + jnp.arange(tq)[None, None, :] - jnp.arange(WIN + tq)[None, :, None]
    win_bias = jnp.where((dist >= 0) & (dist <= WIN - 1), 0.0, NEG).astype(F32)

    def k_spec(off):
        return pl.BlockSpec((None, None, seq, HEAD_DIM), lambda b, g, i: (off + g, b, 0, 0))

    def vt_spec(off):
        return pl.BlockSpec((None, None, HEAD_DIM, seq), lambda b, g, i: (off + g, b, 0, 0))

    return pl.pallas_call(
        functools.partial(_nsa_kernel, seq=seq),
        grid=(batch, KV_GROUPS_B, nq),
        in_specs=[pl.BlockSpec((REP_B, None, HEAD_DIM, tq), lambda b, g, i: (PC_QB // REP_B + g, b, 0, i)),
                  pl.BlockSpec((None, None, n_cp, HEAD_DIM), lambda b, g, i: (0, b * KV_GROUPS_B + g, 0, 0)),
                  pl.BlockSpec((None, None, HEAD_DIM, n_cp), lambda b, g, i: (1, b * KV_GROUPS_B + g, 0, 0)),
                  k_spec(PR_KS), vt_spec(PC_VS), k_spec(PR_KW), vt_spec(PC_VW),
                  pl.BlockSpec((seq, LANES), lambda b, g, i: (0, 0)),
                  pl.BlockSpec((None, WIN + tq, tq), lambda b, g, i: (jnp.minimum(i, WIN // tq), 0, 0)),
                  pl.BlockSpec((tq, LANES), lambda b, g, i: (b * nq + i, g))],
        out_specs=pl.BlockSpec((tq, REP_B * HEAD_DIM), lambda b, g, i: (b * nq + i, g)),
        out_shape=jax.ShapeDtypeStruct((batch * seq, WIDTH_B), F32),
        compiler_params=_params("parallel", "parallel", "arbitrary"),
        name="nsa_attention",
    )(pc, kvc, kvct, pr4, pc, pr4, pc, block_onehot, win_bias, gl)


def _mm_res_kernel(a_ref, w_ref, r_ref, o_ref):
    o_ref[...] = _dot(a_ref[...], w_ref[...]) + r_ref[...]


def matmul_residual(a, w, res, tm, tn):
    m, k = a.shape
    n = w.shape[1]
    return pl.pallas_call(
        _mm_res_kernel,
        grid=(m // tm, n // tn),
        in_specs=[pl.BlockSpec((tm, k), lambda i, j: (i, 0)),
                  pl.BlockSpec((k, tn), lambda i, j: (0, j)),
                  pl.BlockSpec((tm, tn), lambda i, j: (i, j))],
        out_specs=pl.BlockSpec((tm, tn), lambda i, j: (i, j)),
        out_shape=jax.ShapeDtypeStruct((m, n), F32),
        compiler_params=_params("parallel", "arbitrary"),
        name="matmul_residual",
    )(a, w, res)


def _ffn_up_kernel(h_ref, wg_ref, wu_ref, o_ref):
    h = h_ref[...]
    g = _dot(h, wg_ref[...].astype(BF16))
    u = _dot(h, wu_ref[...].astype(BF16))
    o_ref[...] = (g * jax.nn.sigmoid(g) * u).astype(o_ref.dtype)


def ffn_up(h, wg, wu, layer, tm, tn):
    m, k = h.shape
    n = wg.shape[2]
    return pl.pallas_call(
        _ffn_up_kernel,
        grid=(m // tm, n // tn),
        in_specs=[pl.BlockSpec((tm, k), lambda i, j: (i, 0)),
                  pl.BlockSpec((None, k, tn), lambda i, j: (layer, 0, j)),
                  pl.BlockSpec((None, k, tn), lambda i, j: (layer, 0, j))],
        out_specs=pl.BlockSpec((tm, tn), lambda i, j: (i, j)),
        out_shape=jax.ShapeDtypeStruct((m, n), BF16),
        compiler_params=_params("parallel", "arbitrary"),
        name="ffn_up",
    )(h, wg, wu)


def _cast_kernel(w_ref, o_ref):
    o_ref[...] = w_ref[...].astype(o_ref.dtype)


def cast_columns(w, layer, n_cols, tk=512, tn=2816):
    _, k, _ = w.shape
    return pl.pallas_call(
        _cast_kernel,
        grid=(k // tk, n_cols // tn),
        in_specs=[pl.BlockSpec((None, tk, tn), lambda i, j: (layer, i, j))],
        out_specs=pl.BlockSpec((tk, tn), lambda i, j: (i, j)),
        out_shape=jax.ShapeDtypeStruct((k, n_cols), BF16),
        compiler_params=_params("parallel", "parallel"),
        name="cast_columns",
    )(w)


def _layer(x, norm_attn, w_in, layer, ck_pe, ck_w1, ck_w2, cv_pe, cv_w1, cv_w2,
           out_norm_a, out_norm_b, w_out, norm_ffn, w_gate, w_up, w_down, cos_t, sin_t, batch, seq):
    w_main = w_in.astype(BF16)
    w_gl = lax.slice(w_in, (layer, 0, D_MAIN), (layer + 1, D_MODEL, D_MAIN + N_GATES))
    w_gl = jnp.pad(w_gl.reshape(D_MODEL, N_GATES), ((0, 0), (0, LANES - N_GATES))).astype(BF16)
    pe = jnp.stack([ck_pe, cv_pe])
    w1 = jnp.stack([ck_w1, cv_w1]).reshape(2, CMP_LEN, HEAD_DIM, HEAD_DIM).astype(BF16)
    w2 = jnp.stack([ck_w2, cv_w2]).astype(BF16)

    h, gl = rms_gate(x, norm_attn, w_gl)
    gl = gl[:, :N_GATES].reshape(-1, KV_GROUPS_B, REP_B * 3)
    gl = jnp.pad(gl, ((0, 0), (0, 0), (0, LANES - REP_B * 3))).reshape(-1, KV_GROUPS_B * LANES)
    pf = project(h, w_main, layer, cos_t, sin_t, F32_TILES, F32, batch, seq, layout="view")
    pr = project(h, w_main, layer, cos_t, sin_t, ROW_TILES, BF16, batch, seq)
    pc = project(h, w_main, layer, cos_t, sin_t, COL_TILES, BF16, batch, seq, layout="cols")
    o_a = dilated_attention(pf, batch, seq)
    kvc, kvct = compress(pf, pe, w1, w2, batch, seq)
    o_b = nsa_attention(pr, pc, kvc, kvct, gl, batch, seq)
    mixed = rms_pair(o_a, o_b, out_norm_a, out_norm_b)
    x1 = matmul_residual(mixed, w_out.astype(BF16), x, tm=1024, tn=1024)
    h2 = rms(x1, norm_ffn, BF16)
    act = ffn_up(h2, w_gate, w_up, layer, tm=2048, tn=256)
    x2 = matmul_residual(act, w_down.astype(BF16), x1, tm=512, tn=512)
    return x2


def kernel(x, norm_attn, w_in, ck_pe, ck_w1, ck_w2, cv_pe, cv_w1, cv_w2, out_norm_a, out_norm_b, w_out,
           norm_ffn, w_gate, w_up, w_down, norm_final):
    batch, seq, d = x.shape
    depth = w_in.shape[0]
    cos_t, sin_t = _rope_tables(seq)
    xf = x.reshape(batch * seq, d)
    for l in range(depth):
        xf = _layer(xf, norm_attn[l], w_in, l, ck_pe[l], ck_w1[l], ck_w2[l], cv_pe[l], cv_w1[l], cv_w2[l],
                    out_norm_a[l], out_norm_b[l], w_out[l], norm_ffn[l], w_gate, w_up, w_down[l],
                    cos_t, sin_t, batch, seq)
    return rms(xf, norm_final, F32).reshape(batch, seq, d)
```

```python
import functools
import math

import jax
import jax.numpy as jnp
from jax import lax
from jax.experimental import pallas as pl
from jax.experimental.pallas import tpu as pltpu

F32 = jnp.float32
BF16 = jnp.bfloat16

D_MODEL = 4096
HEAD_DIM = 128
HEADS_A = 16
HEADS_B = 16
KV_GROUPS_B = 4
REP_B = HEADS_B // KV_GROUPS_B
WIDTH_A = HEADS_A * HEAD_DIM
WIDTH_B = HEADS_B * HEAD_DIM
DILATED_CONFIGS = ((128, 1), (512, 4), (2048, 16))
BLK = 128
CMP_LEN = 32
CMP_STRIDE = 16
SLC_BLOCK = 64
N_SELECT = 16
WIN = 512
FORCE_SCORE = 1e6
ROPE_THETA = 500000.0
ROPE_DIM = HEAD_DIM // 4
EPS = 1e-5
Q_SCALE = HEAD_DIM ** -0.5 * math.log2(math.e)
N_GATES = 3 * HEADS_B
D_MAIN = 3 * WIDTH_A + WIDTH_B + 6 * KV_GROUPS_B * HEAD_DIM

VMEM_LIMIT_BYTES = 56 * 1024 * 1024
LANES = 128
NEG = -1e30
SEL_BIAS = -32768.0

PROJ_TILE = 4 * HEAD_DIM
F32_TILES = tuple(range(12)) + (16, 17)
ROW_TILES = (18, 20)
COL_TILES = (12, 13, 14, 15, 19, 21)
Q_TILES = (0, 1, 2, 3, 12, 13, 14, 15)
V_TILES = (8, 9, 10, 11, 17, 19, 21)
PF_QA, PF_KA, PF_VA, PF_KC, PF_VC = 0, 16, 32, 48, 52
PR_KS, PR_KW = 0, 4
PC_QB, PC_VS, PC_VW = 0, 16, 20


def _params(*sem):
    return pltpu.CompilerParams(dimension_semantics=sem, vmem_limit_bytes=VMEM_LIMIT_BYTES)


def _dot(a, b):
    return jnp.dot(a, b, preferred_element_type=F32)


def _dot_nt(a, b):
    return lax.dot_general(a, b, (((1,), (1,)), ((), ())), preferred_element_type=F32)


def _rms(x, g):
    return x * lax.rsqrt(jnp.mean(x * x, axis=-1, keepdims=True) + EPS) * g


def _rms_gate_kernel(x_ref, g_ref, wgl_ref, h_ref, gl_ref):
    hb = _rms(x_ref[...], g_ref[...]).astype(BF16)
    h_ref[...] = hb
    gl_ref[...] = _dot(hb, wgl_ref[...])


def rms_gate(x, gain, w_gl, tm=512):
    m, d = x.shape
    ng = w_gl.shape[1]
    return pl.pallas_call(
        _rms_gate_kernel,
        grid=(m // tm,),
        in_specs=[pl.BlockSpec((tm, d), lambda i: (i, 0)),
                  pl.BlockSpec((1, d), lambda i: (0, 0)),
                  pl.BlockSpec((d, ng), lambda i: (0, 0))],
        out_specs=[pl.BlockSpec((tm, d), lambda i: (i, 0)),
                   pl.BlockSpec((tm, ng), lambda i: (i, 0))],
        out_shape=[jax.ShapeDtypeStruct((m, d), BF16), jax.ShapeDtypeStruct((m, ng), F32)],
        compiler_params=_params("parallel"),
        name="rms_gate",
    )(x, gain.reshape(1, d), w_gl)


def _rms_kernel(x_ref, g_ref, o_ref):
    o_ref[...] = _rms(x_ref[...], g_ref[...]).astype(o_ref.dtype)


def rms(x, gain, out_dtype, tm=512):
    m, d = x.shape
    return pl.pallas_call(
        _rms_kernel,
        grid=(m // tm,),
        in_specs=[pl.BlockSpec((tm, d), lambda i: (i, 0)),
                  pl.BlockSpec((1, d), lambda i: (0, 0))],
        out_specs=pl.BlockSpec((tm, d), lambda i: (i, 0)),
        out_shape=jax.ShapeDtypeStruct((m, d), out_dtype),
        compiler_params=_params("parallel"),
        name="rms",
    )(x, gain.reshape(1, d))


def _rms_pair_kernel(a_ref, b_ref, ga_ref, gb_ref, o_ref):
    n_heads, nv, _ = a_ref.shape
    wa = n_heads * HEAD_DIM
    lanes = [slice(r * HEAD_DIM, (r + 1) * HEAD_DIM) for r in range(VIEW_RES)]
    inv = []
    for r in range(VIEW_RES):
        sq = a_ref[0, :, lanes[r]] * a_ref[0, :, lanes[r]]
        for hh in range(1, n_heads):
            sq = sq + a_ref[hh, :, lanes[r]] * a_ref[hh, :, lanes[r]]
        inv.append(lax.rsqrt(jnp.sum(sq, axis=-1, keepdims=True) * (1.0 / wa) + EPS))
    for hh in range(n_heads):
        by_res = jnp.stack([a_ref[hh, :, lanes[r]] * inv[r] for r in range(VIEW_RES)], axis=0)
        tokens = jnp.swapaxes(by_res, 0, 1).reshape(nv * VIEW_RES, HEAD_DIM)
        cols = slice(hh * HEAD_DIM, (hh + 1) * HEAD_DIM)
        o_ref[:, cols] = (tokens * ga_ref[:, cols]).astype(o_ref.dtype)
    o_ref[:, wa:] = _rms(b_ref[...], gb_ref[...]).astype(o_ref.dtype)


def rms_pair(a, b, ga, gb, tm=512):
    n_heads, batch, rows, width = a.shape
    assert width == VIEW_RES * HEAD_DIM and (rows * VIEW_RES) % tm == 0
    wa = n_heads * HEAD_DIM
    m, wb = b.shape
    tiles_per_seq = rows * VIEW_RES // tm
    return pl.pallas_call(
        _rms_pair_kernel,
        grid=(m // tm,),
        in_specs=[pl.BlockSpec((n_heads, None, tm // VIEW_RES, width),
                               lambda i: (0, i // tiles_per_seq, i % tiles_per_seq, 0)),
                  pl.BlockSpec((tm, wb), lambda i: (i, 0)),
                  pl.BlockSpec((1, wa), lambda i: (0, 0)),
                  pl.BlockSpec((1, wb), lambda i: (0, 0))],
        out_specs=pl.BlockSpec((tm, wa + wb), lambda i: (i, 0)),
        out_shape=jax.ShapeDtypeStruct((m, wa + wb), BF16),
        compiler_params=_params("parallel"),
        name="rms_pair",
    )(a, b, ga.reshape(1, wa), gb.reshape(1, wb))


def _rope_tables(seq):
    inv = ROPE_THETA ** (-jnp.arange(0, ROPE_DIM, 2, dtype=F32) / ROPE_DIM)
    ang = jnp.arange(seq, dtype=F32)[:, None] * inv[None, :]
    cos, sin = jnp.cos(ang), jnp.sin(ang)
    ones = jnp.ones((seq, HEAD_DIM - ROPE_DIM), F32)
    cos_t = jnp.concatenate([cos, cos, ones], axis=1)
    sin_t = jnp.concatenate([-sin, sin, 0.0 * ones], axis=1)
    return cos_t, sin_t


PROJ_ROWS = 256
VIEW_RES = DILATED_CONFIGS[-1][1]


def _proj_kernel(h_ref, w_ref, cos_ref, sin_ref, o_ref, *, src_tiles, layout):
    jt = pl.program_id(1)
    is_v = functools.reduce(jnp.logical_or, [jt == n for n, t in enumerate(src_tiles) if t in V_TILES], False)
    is_q = functools.reduce(jnp.logical_or, [jt == n for n, t in enumerate(src_tiles) if t in Q_TILES], False)
    heads_per_tile = PROJ_TILE // HEAD_DIM
    tm = h_ref.shape[0]
    scale = jnp.where(is_q, Q_SCALE, 1.0).astype(F32)
    lane = lax.broadcasted_iota(jnp.int32, (1, HEAD_DIM), 1)
    first_half = lane < ROPE_DIM // 2

    for c in range(tm // PROJ_ROWS):
        rows = slice(c * PROJ_ROWS, (c + 1) * PROJ_ROWS)
        y = _dot(h_ref[rows, :], w_ref[...])
        cos_t = jnp.where(is_v, 1.0, cos_ref[rows, :]) * scale
        sin_t = jnp.where(is_v, 0.0, sin_ref[rows, :]) * scale
        for hh in range(heads_per_tile):
            yh = y[:, hh * HEAD_DIM:(hh + 1) * HEAD_DIM]
            partner = jnp.where(first_half,
                                pltpu.roll(yh, HEAD_DIM - ROPE_DIM // 2, axis=1),
                                pltpu.roll(yh, ROPE_DIM // 2, axis=1))
            val = yh * cos_t + partner * sin_t
            if layout == "cols":
                for cc in range(PROJ_ROWS // LANES):
                    col0 = c * PROJ_ROWS + cc * LANES
                    o_ref[hh, :, col0:col0 + LANES] = val[cc * LANES:(cc + 1) * LANES, :].T.astype(o_ref.dtype)
            elif layout == "view":
                nv = PROJ_ROWS // VIEW_RES
                by_res = jnp.swapaxes(val.reshape(nv, VIEW_RES, HEAD_DIM), 0, 1)
                for r in range(VIEW_RES):
                    o_ref[hh, c * nv:(c + 1) * nv, r * HEAD_DIM:(r + 1) * HEAD_DIM] = by_res[r].astype(o_ref.dtype)
            else:
                o_ref[hh, rows, :] = val.astype(o_ref.dtype)


def project(h, w, layer, cos_t, sin_t, src_tiles, out_dtype, batch, seq, layout="rows", tm=2048):
    m, d = h.shape
    n_tiles = len(src_tiles)
    heads_per_tile = PROJ_TILE // HEAD_DIM
    tm = min(tm, seq)
    pos_blocks = seq // tm

    def w_map(i, j):
        col = functools.reduce(lambda acc, nt: jnp.where(j == nt[0], nt[1], acc),
                               list(enumerate(src_tiles)), 0)
        return (layer, 0, col)

    if layout == "cols":
        out_spec = pl.BlockSpec((heads_per_tile, None, HEAD_DIM, tm),
                                lambda i, j: (j, i // pos_blocks, 0, i % pos_blocks))
        out_shape = jax.ShapeDtypeStruct((n_tiles * heads_per_tile, batch, HEAD_DIM, seq), out_dtype)
    elif layout == "view":
        out_spec = pl.BlockSpec((heads_per_tile, None, tm // VIEW_RES, VIEW_RES * HEAD_DIM),
                                lambda i, j: (j, i // pos_blocks, i % pos_blocks, 0))
        out_shape = jax.ShapeDtypeStruct((n_tiles * heads_per_tile, batch, seq // VIEW_RES, VIEW_RES * HEAD_DIM),
                                         out_dtype)
    else:
        out_spec = pl.BlockSpec((heads_per_tile, tm, HEAD_DIM), lambda i, j: (j, i, 0))
        out_shape = jax.ShapeDtypeStruct((n_tiles * heads_per_tile, m, HEAD_DIM), out_dtype)

    return pl.pallas_call(
        functools.partial(_proj_kernel, src_tiles=src_tiles, layout=layout),
        grid=(m // tm, n_tiles),
        in_specs=[pl.BlockSpec((tm, d), lambda i, j: (i, 0)),
                  pl.BlockSpec((None, d, PROJ_TILE), w_map),
                  pl.BlockSpec((tm, HEAD_DIM), lambda i, j: (i % pos_blocks, 0)),
                  pl.BlockSpec((tm, HEAD_DIM), lambda i, j: (i % pos_blocks, 0))],
        out_specs=out_spec,
        out_shape=out_shape,
        compiler_params=_params("parallel", "arbitrary"),
        name="project_" + layout,
    )(h, w, cos_t, sin_t)


DIL_UNROLL = 8
DIL_RES = VIEW_RES


def _dilated_kernel(q_ref, k_ref, v_ref, o_ref, acc_ref, m_ref, l_ref, bias_ref, *, seq):
    for ci, (window, dil) in enumerate(DILATED_CONFIGS):
        band = window // dil
        n_pieces = DIL_RES // dil
        pr = BLK // n_pieces
        kr = 2 * pr
        tiles_per_res = seq // (BLK * dil)
        assert band <= BLK and pr % 8 == 0 and pr & (pr - 1) == 0

        q_row = lax.broadcasted_iota(jnp.int32, (BLK, 1), 0)
        k_row = lax.broadcasted_iota(jnp.int32, (1, 2 * BLK), 1)
        q_sub = n_pieces * (q_row & (pr - 1)) + (q_row >> (pr.bit_length() - 1))
        k_sub = n_pieces * (k_row & (kr - 1)) + (k_row >> (kr.bit_length() - 1))
        for lead in range(2):
            dist = (q_sub + lead * BLK) - k_sub
            bias_ref[2 * ci + lead] = jnp.where((dist >= 0) & (dist <= band), 0.0, NEG)

        def body(step, carry, ci=ci, dil=dil, band=band, n_pieces=n_pieces, pr=pr, kr=kr,
                 tiles_per_res=tiles_per_res):
            def pieces(ref, row0, nrows, r):
                return [ref[pl.ds(row0, nrows), pl.ds(pl.multiple_of((r + dil * a) * HEAD_DIM, HEAD_DIM), HEAD_DIM)]
                        for a in range(n_pieces)]

            tiles = []
            for u in range(DIL_UNROLL):
                idx = step * DIL_UNROLL + u
                r = idx // tiles_per_res
                i = idx - r * tiles_per_res
                kb = jnp.maximum(i - 1, 0)
                q_row0 = pl.multiple_of(i * pr, pr)
                k_row0 = pl.multiple_of(kb * pr, pr)
                m_old = None if ci == 0 else jnp.concatenate(pieces(m_ref, q_row0, pr, r), axis=0)
                tiles.append((r, i, kb, q_row0, k_row0, m_old))

            def update(r, q_row0, m_old, m_new, l_tile, pv):
                if ci == 0:
                    l_new = l_tile
                    acc_new = pv
                else:
                    alpha = jnp.exp2(m_old - m_new)
                    l_new = alpha * jnp.concatenate(pieces(l_ref, q_row0, pr, r), axis=0) + l_tile
                    acc_new = alpha * jnp.concatenate(pieces(acc_ref, q_row0, pr, r), axis=0) + pv
                for a in range(n_pieces):
                    col = pl.ds(pl.multiple_of((r + dil * a) * HEAD_DIM, HEAD_DIM), HEAD_DIM)
                    rows = slice(a * pr, (a + 1) * pr)
                    m_ref[pl.ds(q_row0, pr), col] = m_new[rows]
                    l_ref[pl.ds(q_row0, pr), col] = l_new[rows]
                    acc_ref[pl.ds(q_row0, pr), col] = acc_new[rows]

            scores, maxes, probs, results = {}, {}, {}, []
            for t in range(DIL_UNROLL + 4):
                if t < DIL_UNROLL:
                    (r, i, kb, q_row0, k_row0, m_old) = tiles[t]
                    q = jnp.concatenate(pieces(q_ref, q_row0, pr, r), axis=0).astype(BF16)
                    k = jnp.concatenate(pieces(k_ref, k_row0, kr, r), axis=0).astype(BF16)
                    scores[t] = _dot_nt(q, k) + bias_ref[2 * ci + i - kb]
                if 0 <= t - 1 < DIL_UNROLL:
                    m_tile = jnp.max(scores[t - 1], axis=1, keepdims=True)
                    m_old = tiles[t - 1][5]
                    maxes[t - 1] = jnp.broadcast_to(m_tile, (BLK, LANES)) if ci == 0 else jnp.maximum(m_old, m_tile)
                if 0 <= t - 2 < DIL_UNROLL:
                    m_new = maxes[t - 2]
                    probs[t - 2] = jnp.exp2(scores.pop(t - 2) - jnp.concatenate([m_new, m_new], axis=1))
                if 0 <= t - 3 < DIL_UNROLL:
                    (r, i, kb, q_row0, k_row0, m_old) = tiles[t - 3]
                    p = probs.pop(t - 3)
                    v = jnp.concatenate(pieces(v_ref, k_row0, kr, r), axis=0).astype(BF16)
                    v_sum = jnp.concatenate([v, jnp.ones((2 * BLK, LANES), BF16)], axis=1)
                    pv = _dot(p.astype(BF16), v_sum)
                    results.append((r, q_row0, m_old, maxes.pop(t - 3), pv[:, HEAD_DIM:], pv[:, :HEAD_DIM]))
                if 0 <= t - 4 < DIL_UNROLL:
                    update(*results[t - 4])
            return carry

        lax.fori_loop(0, seq // (BLK * DIL_UNROLL), body, 0)
    o_ref[...] = acc_ref[...] / l_ref[...]


def dilated_attention(pf, batch, seq):
    assert all(DIL_RES % d == 0 for _, d in DILATED_CONFIGS)
    assert seq % (BLK * DIL_RES) == 0 and seq // DIL_RES >= 2 * BLK and (seq // BLK) % DIL_UNROLL == 0
    rows = seq // DIL_RES
    width = DIL_RES * HEAD_DIM
    view = pf

    def spec(off):
        return pl.BlockSpec((None, None, rows, width), lambda b, h: (off + h, b, 0, 0))

    out = pl.pallas_call(
        functools.partial(_dilated_kernel, seq=seq),
        grid=(batch, HEADS_A),
        in_specs=[spec(PF_QA), spec(PF_KA), spec(PF_VA)],
        out_specs=spec(0),
        out_shape=jax.ShapeDtypeStruct((HEADS_A, batch, rows, width), F32),
        scratch_shapes=[pltpu.VMEM((rows, width), F32)] * 3
        + [pltpu.VMEM((2 * len(DILATED_CONFIGS), BLK, 2 * BLK), F32)],
        compiler_params=_params("parallel", "parallel"),
        name="dilated_attention",
    )(view, view, view)
    return out


def _compress_kernel(t_ref, pe_ref, w1_ref, w2_ref, o_ref, ot_ref, *, seq):
    n_chunks = seq // CMP_STRIDE
    first = jnp.zeros((n_chunks, HEAD_DIM), F32)
    second = jnp.zeros((n_chunks, HEAD_DIM), F32)
    for i in range(CMP_STRIDE):
        ti = t_ref[:, i * HEAD_DIM:(i + 1) * HEAD_DIM]
        first += _dot((ti + pe_ref[pl.ds(i, 1), :]).astype(BF16), w1_ref[i])
        second += _dot((ti + pe_ref[pl.ds(CMP_STRIDE + i, 1), :]).astype(BF16), w1_ref[CMP_STRIDE + i])
    pre = first + pltpu.roll(second, n_chunks - 1, axis=0)
    out = _dot(jax.nn.gelu(pre).astype(BF16), w2_ref[...])
    o_ref[...] = out.astype(o_ref.dtype)
    ot_ref[...] = out.T.astype(ot_ref.dtype)


def compress(pf, pe, w1, w2, batch, seq):
    assert CMP_LEN == 2 * CMP_STRIDE and CMP_STRIDE == VIEW_RES
    n_chunks = seq // CMP_STRIDE
    pf4 = pf
    bg = batch * KV_GROUPS_B
    return pl.pallas_call(
        functools.partial(_compress_kernel, seq=seq),
        grid=(2, batch, KV_GROUPS_B),
        in_specs=[pl.BlockSpec((None, None, n_chunks, VIEW_RES * HEAD_DIM),
                               lambda kv, b, g: (PF_KC + kv * KV_GROUPS_B + g, b, 0, 0)),
                  pl.BlockSpec((None, CMP_LEN, HEAD_DIM), lambda kv, b, g: (kv, 0, 0)),
                  pl.BlockSpec((None, CMP_LEN, HEAD_DIM, HEAD_DIM), lambda kv, b, g: (kv, 0, 0, 0)),
                  pl.BlockSpec((None, HEAD_DIM, HEAD_DIM), lambda kv, b, g: (kv, 0, 0))],
        out_specs=[pl.BlockSpec((None, None, n_chunks, HEAD_DIM), lambda kv, b, g: (kv, b * KV_GROUPS_B + g, 0, 0)),
                   pl.BlockSpec((None, None, HEAD_DIM, n_chunks), lambda kv, b, g: (kv, b * KV_GROUPS_B + g, 0, 0))],
        out_shape=[jax.ShapeDtypeStruct((2, bg, n_chunks, HEAD_DIM), BF16),
                   jax.ShapeDtypeStruct((2, bg, HEAD_DIM, n_chunks), BF16)],
        compiler_params=_params("parallel", "parallel", "parallel"),
        name="compress",
    )(pf4, pe, w1, w2)


NSA_TQ = 512
NSA_TK = 512


def _select_blocks(score):
    n_s, tq = score.shape
    groups = n_s // 8
    rows8 = [score[8 * v:8 * v + 8, :] for v in range(groups)]
    rank8 = [jnp.zeros((8, tq), F32) for _ in range(groups)]
    sub = lax.broadcasted_iota(jnp.int32, (8, 1), 0)
    for jp in range(n_s):
        vp, sp = divmod(jp, 8)
        row = jnp.broadcast_to(rows8[vp][sp:sp + 1, :], (8, tq))
        for v in range(groups):
            if v > vp:
                beats = jnp.where(row >= rows8[v], 1.0, 0.0)
            elif v < vp:
                beats = jnp.where(row > rows8[v], 1.0, 0.0)
            else:
                beats = jnp.where(sub > sp, jnp.where(row >= rows8[v], 1.0, 0.0),
                                  jnp.where(row > rows8[v], 1.0, 0.0))
            rank8[v] = rank8[v] + beats
    rank = jnp.concatenate(rank8, axis=0)
    return (rank < N_SELECT) & (score > -jnp.inf)


SUM_ROWS = 16


def _with_sum_rows(vt):
    return jnp.concatenate([vt, jnp.ones((SUM_ROWS, vt.shape[1]), vt.dtype)], axis=0)


def _flash_block(s, vt_sum, carry):
    m_i, acc = carry
    m_new = jnp.maximum(m_i, jnp.max(s, axis=0, keepdims=True))
    alpha = jnp.exp2(m_i - m_new)
    p = jnp.exp2(s - m_new)
    return m_new, alpha * acc + _dot(vt_sum, p.astype(BF16))


def _normalized(acc):
    return acc[:HEAD_DIM] * (1.0 / acc[HEAD_DIM:HEAD_DIM + 1])


def _nsa_kernel(q_ref, kc_ref, vct_ref, ks_ref, vst_ref, kw_ref, vwt_ref, e_ref, wb_ref, gl_ref, o_ref, gates_ref,
                *, seq):
    tq, tk = NSA_TQ, NSA_TK
    cols = REP_B * tq
    n_cp = seq // CMP_STRIDE
    n_s = seq // SLC_BLOCK
    qi = pl.program_id(2)
    t0 = qi * tq
    q_t = jnp.concatenate([q_ref[r] for r in range(REP_B)], axis=1)
    lane = lax.broadcasted_iota(jnp.int32, (1, tq), 1)
    tpos = t0 + lane

    def per_head(x):
        return jnp.concatenate([x] * REP_B, axis=1)

    csub = lax.broadcasted_iota(jnp.int32, (n_cp, 1), 0)
    c_ok = (csub * CMP_STRIDE + (CMP_LEN - 1) <= tpos) & (csub < n_cp - 1)
    sc = _dot(kc_ref[...], q_t) + per_head(jnp.where(c_ok, 0.0, NEG))
    e = jnp.exp2(sc - jnp.max(sc, axis=0, keepdims=True))
    has_block = per_head(jnp.where(tpos >= CMP_LEN - 1, 1.0, 0.0))
    p_cmp = e * (has_block / jnp.maximum(jnp.sum(e, axis=0, keepdims=True), 1e-30))
    o_cmp = _dot(vct_ref[...], p_cmp.astype(BF16))

    ws = WIN + tq
    w0 = pl.multiple_of(jnp.maximum(t0 - WIN, 0), tq)
    s_w = _dot(kw_ref[pl.ds(w0, ws), :], q_t) + per_head(wb_ref[...])
    p_w = jnp.exp2(s_w - jnp.max(s_w, axis=0, keepdims=True))
    o_win = _normalized(_dot(_with_sum_rows(vwt_ref[:, pl.ds(w0, ws)]), p_w.astype(BF16)))

    p_sum = p_cmp[:, 0:tq]
    for r in range(1, REP_B):
        p_sum = p_sum + p_cmp[:, r * tq:(r + 1) * tq]
    jj = lax.broadcasted_iota(jnp.int32, (n_s, 1), 0)
    cidx = lax.broadcasted_iota(jnp.int32, (1, n_cp), 1)
    ratio = SLC_BLOCK // CMP_STRIDE
    c_first = ratio * jj - (CMP_LEN // CMP_STRIDE - 1)
    hits = jnp.where((cidx >= c_first) & (cidx < ratio * (jj + 1)), 1.0, 0.0).astype(BF16)
    p_hi = p_sum.astype(BF16)
    p_lo = (p_sum - p_hi.astype(F32)).astype(BF16)
    imp = _dot(hits, p_hi) + _dot(hits, p_lo)
    qblk = tpos >> (SLC_BLOCK.bit_length() - 1)
    forced = (jj == 0) | (jj == qblk) | (jj == qblk - 1)
    valid = jj * SLC_BLOCK <= tpos
    score = jnp.where(forced, FORCE_SCORE, jnp.where(valid, imp, -jnp.inf))
    sel = _select_blocks(score)
    bias = jnp.where(sel, 0.0, SEL_BIAS)
    if n_s < LANES:
        bias = jnp.concatenate([bias, jnp.full((LANES - n_s, tq), SEL_BIAS, F32)], axis=0)
    q_aug = jnp.concatenate([q_t, per_head(bias.astype(BF16))], axis=0)

    def slc_scores(kt):
        k0 = pl.multiple_of(kt * tk, tk)
        k_aug = jnp.concatenate([ks_ref[pl.ds(k0, tk), :], e_ref[pl.ds(k0, tk), :]], axis=1)
        return _dot(k_aug, q_aug)

    def slc_values(kt):
        return _with_sum_rows(vst_ref[:, pl.ds(pl.multiple_of(kt * tk, tk), tk)])

    def slc_step(kt, stats):
        return _flash_block(slc_scores(kt), slc_values(kt), stats)

    last = (t0 + tq - 1) // tk
    stats = (jnp.full((1, cols), NEG, F32), jnp.zeros((HEAD_DIM + SUM_ROWS, cols), F32))
    stats = lax.fori_loop(0, last, slc_step, stats)
    tok = last * tk + lax.broadcasted_iota(jnp.int32, (tk, 1), 0)
    s_last = slc_scores(last) + per_head(jnp.where(tok <= tpos, 0.0, NEG))
    o_slc = _normalized(_flash_block(s_last, slc_values(last), stats)[1])

    gates_ref[...] = jax.nn.sigmoid(gl_ref[...].T)
    first = pl.program_id(1) * (REP_B * 3)

    def gate(k):
        return gates_ref[pl.ds(first + k, 1), :]

    for r in range(REP_B):
        sl = slice(r * tq, (r + 1) * tq)
        merged = (gate(3 * r) * o_cmp[:, sl] + gate(3 * r + 1) * o_slc[:, sl]
                  + gate(3 * r + 2) * o_win[:, sl])
        o_ref[:, r * HEAD_DIM:(r + 1) * HEAD_DIM] = merged.T


def nsa_attention(pr, pc, kvc, kvct, gl, batch, seq):
    tq = NSA_TQ
    n_s = seq // SLC_BLOCK
    assert seq % NSA_TK == 0 and NSA_TK % tq == 0 and seq >= WIN + tq and n_s <= LANES and n_s % 8 == 0
    assert WIN % tq == 0
    assert SLC_BLOCK & (SLC_BLOCK - 1) == 0
    n_cp = seq // CMP_STRIDE
    nq = seq // tq
    pr4 = pr.reshape(pr.shape[0], batch, seq, HEAD_DIM)
    block_onehot = (jnp.arange(seq)[:, None] // SLC_BLOCK == jnp.arange(LANES)[None, :]).astype(BF16)
    lead = jnp.arange(WIN // tq + 1)[:, None, None] * tq
    dist = lead + jnp.arange(tq)[None, None, :] - jnp.arange(WIN + tq)[None, :, None]
    win_bias = jnp.where((dist >= 0) & (dist <= WIN - 1), 0.0, NEG).astype(F32)

    def k_spec(off):
        return pl.BlockSpec((None, None, seq, HEAD_DIM), lambda b, g, i: (off + g, b, 0, 0))

    def vt_spec(off):
        return pl.BlockSpec((None, None, HEAD_DIM, seq), lambda b, g, i: (off + g, b, 0, 0))

    return pl.pallas_call(
        functools.partial(_nsa_kernel, seq=seq),
        grid=(batch, KV_GROUPS_B, nq),
        in_specs=[pl.BlockSpec((REP_B, None, HEAD_DIM, tq), lambda b, g, i: (PC_QB // REP_B + g, b, 0, i)),
                  pl.BlockSpec((None, None, n_cp, HEAD_DIM), lambda b, g, i: (0, b * KV_GROUPS_B + g, 0, 0)),
                  pl.BlockSpec((None, None, HEAD_DIM, n_cp), lambda b, g, i: (1, b * KV_GROUPS_B + g, 0, 0)),
                  k_spec(PR_KS), vt_spec(PC_VS), k_spec(PR_KW), vt_spec(PC_VW),
                  pl.BlockSpec((seq, LANES), lambda b, g, i: (0, 0)),
                  pl.BlockSpec((None, WIN + tq, tq), lambda b, g, i: (jnp.minimum(i, WIN // tq), 0, 0)),
                  pl.BlockSpec((tq, LANES), lambda b, g, i: (b * nq + i, 0))],
        out_specs=pl.BlockSpec((tq, REP_B * HEAD_DIM), lambda b, g, i: (b * nq + i, g)),
        out_shape=jax.ShapeDtypeStruct((batch * seq, WIDTH_B), F32),
        scratch_shapes=[pltpu.VMEM((LANES, tq), F32)],
        compiler_params=_params("parallel", "parallel", "arbitrary"),
        name="nsa_attention",
    )(pc, kvc, kvct, pr4, pc, pr4, pc, block_onehot, win_bias, gl)


def _mm_res_kernel(a_ref, w_ref, r_ref, o_ref):
    o_ref[...] = _dot(a_ref[...], w_ref[...]) + r_ref[...]


def matmul_residual(a, w, res, tm, tn, weights_resident=False):
    m, k = a.shape
    n = w.shape[1]
    if weights_resident:
        grid = (n // tn, m // tm)
        row, col = (lambda j, i: i), (lambda j, i: j)
    else:
        grid = (m // tm, n // tn)
        row, col = (lambda i, j: i), (lambda i, j: j)
    return pl.pallas_call(
        _mm_res_kernel,
        grid=grid,
        in_specs=[pl.BlockSpec((tm, k), lambda p, q: (row(p, q), 0)),
                  pl.BlockSpec((k, tn), lambda p, q: (0, col(p, q))),
                  pl.BlockSpec((tm, tn), lambda p, q: (row(p, q), col(p, q)))],
        out_specs=pl.BlockSpec((tm, tn), lambda p, q: (row(p, q), col(p, q))),
        out_shape=jax.ShapeDtypeStruct((m, n), F32),
        compiler_params=_params("parallel", "arbitrary"),
        name="matmul_residual",
    )(a, w, res)


def _ffn_up_kernel(h_ref, wg_ref, wu_ref, o_ref):
    h = h_ref[...]
    g = _dot(h, wg_ref[...].astype(BF16))
    u = _dot(h, wu_ref[...].astype(BF16))
    o_ref[...] = (g * jax.nn.sigmoid(g) * u).astype(o_ref.dtype)


def ffn_up(h, wg, wu, layer, tm, tn):
    m, k = h.shape
    n = wg.shape[2]
    return pl.pallas_call(
        _ffn_up_kernel,
        grid=(m // tm, n // tn),
        in_specs=[pl.BlockSpec((tm, k), lambda i, j: (i, 0)),
                  pl.BlockSpec((None, k, tn), lambda i, j: (layer, 0, j)),
                  pl.BlockSpec((None, k, tn), lambda i, j: (layer, 0, j))],
        out_specs=pl.BlockSpec((tm, tn), lambda i, j: (i, j)),
        out_shape=jax.ShapeDtypeStruct((m, n), BF16),
        compiler_params=_params("parallel", "arbitrary"),
        name="ffn_up",
    )(h, wg, wu)


def _cast_kernel(w_ref, o_ref):
    o_ref[...] = w_ref[...].astype(o_ref.dtype)


def cast_columns(w, layer, n_cols, tk=512, tn=2816):
    _, k, _ = w.shape
    return pl.pallas_call(
        _cast_kernel,
        grid=(k // tk, n_cols // tn),
        in_specs=[pl.BlockSpec((None, tk, tn), lambda i, j: (layer, i, j))],
        out_specs=pl.BlockSpec((tk, tn), lambda i, j: (i, j)),
        out_shape=jax.ShapeDtypeStruct((k, n_cols), BF16),
        compiler_params=_params("parallel", "parallel"),
        name="cast_columns",
    )(w)


def _layer(x, norm_attn, w_in, layer, ck_pe, ck_w1, ck_w2, cv_pe, cv_w1, cv_w2,
           out_norm_a, out_norm_b, w_out, norm_ffn, w_gate, w_up, w_down, cos_t, sin_t, batch, seq):
    w_main = w_in.astype(BF16)
    w_gl = lax.slice(w_in, (layer, 0, D_MAIN), (layer + 1, D_MODEL, D_MAIN + N_GATES))
    w_gl = jnp.pad(w_gl.reshape(D_MODEL, N_GATES), ((0, 0), (0, LANES - N_GATES))).astype(BF16)
    pe = jnp.stack([ck_pe, cv_pe])
    w1 = jnp.stack([ck_w1, cv_w1]).reshape(2, CMP_LEN, HEAD_DIM, HEAD_DIM).astype(BF16)
    w2 = jnp.stack([ck_w2, cv_w2]).astype(BF16)

    h, gl = rms_gate(x, norm_attn, w_gl)
    pf = project(h, w_main, layer, cos_t, sin_t, F32_TILES, F32, batch, seq, layout="view")
    pr = project(h, w_main, layer, cos_t, sin_t, ROW_TILES, BF16, batch, seq)
    pc = project(h, w_main, layer, cos_t, sin_t, COL_TILES, BF16, batch, seq, layout="cols")
    o_a = dilated_attention(pf, batch, seq)
    kvc, kvct = compress(pf, pe, w1, w2, batch, seq)
    o_b = nsa_attention(pr, pc, kvc, kvct, gl, batch, seq)
    mixed = rms_pair(o_a, o_b, out_norm_a, out_norm_b)
    x1 = matmul_residual(mixed, w_out.astype(BF16), x, tm=1024, tn=1024)
    h2 = rms(x1, norm_ffn, BF16)
    act = ffn_up(h2, w_gate, w_up, layer, tm=2048, tn=256)
    x2 = matmul_residual(act, w_down.astype(BF16), x1, tm=512, tn=512, weights_resident=True)
    return x2


def kernel(x, norm_attn, w_in, ck_pe, ck_w1, ck_w2, cv_pe, cv_w1, cv_w2, out_norm_a, out_norm_b, w_out,
           norm_ffn, w_gate, w_up, w_down, norm_final):
    batch, seq, d = x.shape
    depth = w_in.shape[0]
    cos_t, sin_t = _rope_tables(seq)
    xf = x.reshape(batch * seq, d)
    for l in range(depth):
        xf = _layer(xf, norm_attn[l], w_in, l, ck_pe[l], ck_w1[l], ck_w2[l], cv_pe[l], cv_w1[l], cv_w2[l],
                    out_norm_a[l], out_norm_b[l], w_out[l], norm_ffn[l], w_gate, w_up, w_down[l],
                    cos_t, sin_t, batch, seq)
    return rms(xf, norm_final, F32).reshape(batch, seq, d)
```

```python
import functools
import math

import jax
import jax.numpy as jnp
from jax import lax
from jax.experimental import pallas as pl
from jax.experimental.pallas import tpu as pltpu

F32 = jnp.float32
BF16 = jnp.bfloat16

D_MODEL = 4096
HEAD_DIM = 128
HEADS_A = 16
HEADS_B = 16
KV_GROUPS_B = 4
REP_B = HEADS_B // KV_GROUPS_B
WIDTH_A = HEADS_A * HEAD_DIM
WIDTH_B = HEADS_B * HEAD_DIM
DILATED_CONFIGS = ((128, 1), (512, 4), (2048, 16))
BLK = 128
CMP_LEN = 32
CMP_STRIDE = 16
SLC_BLOCK = 64
N_SELECT = 16
WIN = 512
FORCE_SCORE = 1e6
ROPE_THETA = 500000.0
ROPE_DIM = HEAD_DIM // 4
EPS = 1e-5
Q_SCALE = HEAD_DIM ** -0.5 * math.log2(math.e)
N_GATES = 3 * HEADS_B
D_MAIN = 3 * WIDTH_A + WIDTH_B + 6 * KV_GROUPS_B * HEAD_DIM

VMEM_LIMIT_BYTES = 56 * 1024 * 1024
LANES = 128
OUT_PROJ_TILE = (1024, 1024)
FFN_UP_TILE = (2048, 256)
FFN_DOWN_TILE = (512, 512)
NEG = -1e30
SEL_BIAS = -32768.0

PROJ_TILE = 4 * HEAD_DIM
F32_TILES = tuple(range(12)) + (16, 17)
ROW_TILES = (18, 20)
COL_TILES = (12, 13, 14, 15, 19, 21)
Q_TILES = (0, 1, 2, 3, 12, 13, 14, 15)
V_TILES = (8, 9, 10, 11, 17, 19, 21)
PF_QA, PF_KA, PF_VA, PF_KC, PF_VC = 0, 16, 32, 48, 52
PR_KS, PR_KW = 0, 4
PC_QB, PC_VS, PC_VW = 0, 16, 20


def _params(*sem):
    return pltpu.CompilerParams(dimension_semantics=sem, vmem_limit_bytes=VMEM_LIMIT_BYTES)


def _dot(a, b):
    return jnp.dot(a, b, preferred_element_type=F32)


def _dot_nt(a, b):
    return lax.dot_general(a, b, (((1,), (1,)), ((), ())), preferred_element_type=F32)


def _rms(x, g):
    return x * lax.rsqrt(jnp.mean(x * x, axis=-1, keepdims=True) + EPS) * g


def _rms_gate_kernel(x_ref, g_ref, wgl_ref, h_ref, gl_ref):
    hb = _rms(x_ref[...], g_ref[...]).astype(BF16)
    h_ref[...] = hb
    gl_ref[...] = _dot(hb, wgl_ref[...])


def rms_gate(x, gain, w_gl, tm=512):
    m, d = x.shape
    ng = w_gl.shape[1]
    return pl.pallas_call(
        _rms_gate_kernel,
        grid=(m // tm,),
        in_specs=[pl.BlockSpec((tm, d), lambda i: (i, 0)),
                  pl.BlockSpec((1, d), lambda i: (0, 0)),
                  pl.BlockSpec((d, ng), lambda i: (0, 0))],
        out_specs=[pl.BlockSpec((tm, d), lambda i: (i, 0)),
                   pl.BlockSpec((tm, ng), lambda i: (i, 0))],
        out_shape=[jax.ShapeDtypeStruct((m, d), BF16), jax.ShapeDtypeStruct((m, ng), F32)],
        compiler_params=_params("parallel"),
        name="rms_gate",
    )(x, gain.reshape(1, d), w_gl)


def _rms_kernel(x_ref, g_ref, o_ref):
    o_ref[...] = _rms(x_ref[...], g_ref[...]).astype(o_ref.dtype)


def rms(x, gain, out_dtype, tm=512):
    m, d = x.shape
    return pl.pallas_call(
        _rms_kernel,
        grid=(m // tm,),
        in_specs=[pl.BlockSpec((tm, d), lambda i: (i, 0)),
                  pl.BlockSpec((1, d), lambda i: (0, 0))],
        out_specs=pl.BlockSpec((tm, d), lambda i: (i, 0)),
        out_shape=jax.ShapeDtypeStruct((m, d), out_dtype),
        compiler_params=_params("parallel"),
        name="rms",
    )(x, gain.reshape(1, d))


def _rms_pair_kernel(a_ref, b_ref, ga_ref, gb_ref, o_ref):
    n_heads, nv, _ = a_ref.shape
    wa = n_heads * HEAD_DIM
    lanes = [slice(r * HEAD_DIM, (r + 1) * HEAD_DIM) for r in range(VIEW_RES)]
    inv = []
    for r in range(VIEW_RES):
        sq = a_ref[0, :, lanes[r]] * a_ref[0, :, lanes[r]]
        for hh in range(1, n_heads):
            sq = sq + a_ref[hh, :, lanes[r]] * a_ref[hh, :, lanes[r]]
        inv.append(lax.rsqrt(jnp.sum(sq, axis=-1, keepdims=True) * (1.0 / wa) + EPS))
    for hh in range(n_heads):
        by_res = jnp.stack([a_ref[hh, :, lanes[r]] * inv[r] for r in range(VIEW_RES)], axis=0)
        tokens = jnp.swapaxes(by_res, 0, 1).reshape(nv * VIEW_RES, HEAD_DIM)
        cols = slice(hh * HEAD_DIM, (hh + 1) * HEAD_DIM)
        o_ref[:, cols] = (tokens * ga_ref[:, cols]).astype(o_ref.dtype)
    o_ref[:, wa:] = _rms(b_ref[...], gb_ref[...]).astype(o_ref.dtype)


def rms_pair(a, b, ga, gb, tm=512):
    n_heads, batch, rows, width = a.shape
    assert width == VIEW_RES * HEAD_DIM and (rows * VIEW_RES) % tm == 0
    wa = n_heads * HEAD_DIM
    m, wb = b.shape
    tiles_per_seq = rows * VIEW_RES // tm
    return pl.pallas_call(
        _rms_pair_kernel,
        grid=(m // tm,),
        in_specs=[pl.BlockSpec((n_heads, None, tm // VIEW_RES, width),
                               lambda i: (0, i // tiles_per_seq, i % tiles_per_seq, 0)),
                  pl.BlockSpec((tm, wb), lambda i: (i, 0)),
                  pl.BlockSpec((1, wa), lambda i: (0, 0)),
                  pl.BlockSpec((1, wb), lambda i: (0, 0))],
        out_specs=pl.BlockSpec((tm, wa + wb), lambda i: (i, 0)),
        out_shape=jax.ShapeDtypeStruct((m, wa + wb), BF16),
        compiler_params=_params("parallel"),
        name="rms_pair",
    )(a, b, ga.reshape(1, wa), gb.reshape(1, wb))


def _rope_tables(seq):
    inv = ROPE_THETA ** (-jnp.arange(0, ROPE_DIM, 2, dtype=F32) / ROPE_DIM)
    ang = jnp.arange(seq, dtype=F32)[:, None] * inv[None, :]
    cos, sin = jnp.cos(ang), jnp.sin(ang)
    ones = jnp.ones((seq, HEAD_DIM - ROPE_DIM), F32)
    cos_t = jnp.concatenate([cos, cos, ones], axis=1)
    sin_t = jnp.concatenate([-sin, sin, 0.0 * ones], axis=1)
    return cos_t, sin_t


PROJ_ROWS = 256
VIEW_RES = DILATED_CONFIGS[-1][1]


def _proj_kernel(h_ref, w_ref, cos_ref, sin_ref, o_ref, *, src_tiles, layout):
    jt = pl.program_id(1)
    is_v = functools.reduce(jnp.logical_or, [jt == n for n, t in enumerate(src_tiles) if t in V_TILES], False)
    is_q = functools.reduce(jnp.logical_or, [jt == n for n, t in enumerate(src_tiles) if t in Q_TILES], False)
    heads_per_tile = PROJ_TILE // HEAD_DIM
    tm = h_ref.shape[0]
    scale = jnp.where(is_q, Q_SCALE, 1.0).astype(F32)
    lane = lax.broadcasted_iota(jnp.int32, (1, HEAD_DIM), 1)
    first_half = lane < ROPE_DIM // 2

    for c in range(tm // PROJ_ROWS):
        rows = slice(c * PROJ_ROWS, (c + 1) * PROJ_ROWS)
        y = _dot(h_ref[rows, :], w_ref[...])
        cos_t = jnp.where(is_v, 1.0, cos_ref[rows, :]) * scale
        sin_t = jnp.where(is_v, 0.0, sin_ref[rows, :]) * scale
        for hh in range(heads_per_tile):
            yh = y[:, hh * HEAD_DIM:(hh + 1) * HEAD_DIM]
            partner = jnp.where(first_half,
                                pltpu.roll(yh, HEAD_DIM - ROPE_DIM // 2, axis=1),
                                pltpu.roll(yh, ROPE_DIM // 2, axis=1))
            val = yh * cos_t + partner * sin_t
            if layout == "cols":
                for cc in range(PROJ_ROWS // LANES):
                    col0 = c * PROJ_ROWS + cc * LANES
                    o_ref[hh, :, col0:col0 + LANES] = val[cc * LANES:(cc + 1) * LANES, :].T.astype(o_ref.dtype)
            elif layout == "view":
                nv = PROJ_ROWS // VIEW_RES
                by_res = jnp.swapaxes(val.reshape(nv, VIEW_RES, HEAD_DIM), 0, 1)
                for r in range(VIEW_RES):
                    o_ref[hh, c * nv:(c + 1) * nv, r * HEAD_DIM:(r + 1) * HEAD_DIM] = by_res[r].astype(o_ref.dtype)
            else:
                o_ref[hh, rows, :] = val.astype(o_ref.dtype)


def project(h, w, layer, cos_t, sin_t, src_tiles, out_dtype, batch, seq, layout="rows", tm=2048):
    m, d = h.shape
    n_tiles = len(src_tiles)
    heads_per_tile = PROJ_TILE // HEAD_DIM
    tm = min(tm, seq)
    pos_blocks = seq // tm

    def w_map(i, j):
        col = functools.reduce(lambda acc, nt: jnp.where(j == nt[0], nt[1], acc),
                               list(enumerate(src_tiles)), 0)
        return (layer, 0, col)

    if layout == "cols":
        out_spec = pl.BlockSpec((heads_per_tile, None, HEAD_DIM, tm),
                                lambda i, j: (j, i // pos_blocks, 0, i % pos_blocks))
        out_shape = jax.ShapeDtypeStruct((n_tiles * heads_per_tile, batch, HEAD_DIM, seq), out_dtype)
    elif layout == "view":
        out_spec = pl.BlockSpec((heads_per_tile, None, tm // VIEW_RES, VIEW_RES * HEAD_DIM),
                                lambda i, j: (j, i // pos_blocks, i % pos_blocks, 0))
        out_shape = jax.ShapeDtypeStruct((n_tiles * heads_per_tile, batch, seq // VIEW_RES, VIEW_RES * HEAD_DIM),
                                         out_dtype)
    else:
        out_spec = pl.BlockSpec((heads_per_tile, tm, HEAD_DIM), lambda i, j: (j, i, 0))
        out_shape = jax.ShapeDtypeStruct((n_tiles * heads_per_tile, m, HEAD_DIM), out_dtype)

    return pl.pallas_call(
        functools.partial(_proj_kernel, src_tiles=src_tiles, layout=layout),
        grid=(m // tm, n_tiles),
        in_specs=[pl.BlockSpec((tm, d), lambda i, j: (i, 0)),
                  pl.BlockSpec((None, d, PROJ_TILE), w_map),
                  pl.BlockSpec((tm, HEAD_DIM), lambda i, j: (i % pos_blocks, 0)),
                  pl.BlockSpec((tm, HEAD_DIM), lambda i, j: (i % pos_blocks, 0))],
        out_specs=out_spec,
        out_shape=out_shape,
        compiler_params=_params("parallel", "arbitrary"),
        name="project_" + layout,
    )(h, w, cos_t, sin_t)


DIL_UNROLL = 8
DIL_RES = VIEW_RES


def _dilated_kernel(q_ref, k_ref, v_ref, o_ref, acc_ref, m_ref, l_ref, bias_ref, *, seq):
    for ci, (window, dil) in enumerate(DILATED_CONFIGS):
        band = window // dil
        n_pieces = DIL_RES // dil
        pr = BLK // n_pieces
        kr = 2 * pr
        tiles_per_res = seq // (BLK * dil)
        assert band <= BLK and pr % 8 == 0 and pr & (pr - 1) == 0

        q_row = lax.broadcasted_iota(jnp.int32, (BLK, 1), 0)
        k_row = lax.broadcasted_iota(jnp.int32, (1, 2 * BLK), 1)
        q_sub = n_pieces * (q_row & (pr - 1)) + (q_row >> (pr.bit_length() - 1))
        k_sub = n_pieces * (k_row & (kr - 1)) + (k_row >> (kr.bit_length() - 1))
        for lead in range(2):
            dist = (q_sub + lead * BLK) - k_sub
            bias_ref[2 * ci + lead] = jnp.where((dist >= 0) & (dist <= band), 0.0, NEG)

        def body(step, carry, ci=ci, dil=dil, band=band, n_pieces=n_pieces, pr=pr, kr=kr,
                 tiles_per_res=tiles_per_res):
            def pieces(ref, row0, nrows, r):
                return [ref[pl.ds(row0, nrows), pl.ds(pl.multiple_of((r + dil * a) * HEAD_DIM, HEAD_DIM), HEAD_DIM)]
                        for a in range(n_pieces)]

            tiles = []
            for u in range(DIL_UNROLL):
                idx = step * DIL_UNROLL + u
                r = idx // tiles_per_res
                i = idx - r * tiles_per_res
                kb = jnp.maximum(i - 1, 0)
                q_row0 = pl.multiple_of(i * pr, pr)
                k_row0 = pl.multiple_of(kb * pr, pr)
                m_old = None if ci == 0 else jnp.concatenate(pieces(m_ref, q_row0, pr, r), axis=0)
                tiles.append((r, i, kb, q_row0, k_row0, m_old))

            def update(r, q_row0, m_old, m_new, l_tile, pv):
                if ci == 0:
                    l_new = l_tile
                    acc_new = pv
                else:
                    alpha = jnp.exp2(m_old - m_new)
                    l_new = alpha * jnp.concatenate(pieces(l_ref, q_row0, pr, r), axis=0) + l_tile
                    acc_new = alpha * jnp.concatenate(pieces(acc_ref, q_row0, pr, r), axis=0) + pv
                for a in range(n_pieces):
                    col = pl.ds(pl.multiple_of((r + dil * a) * HEAD_DIM, HEAD_DIM), HEAD_DIM)
                    rows = slice(a * pr, (a + 1) * pr)
                    m_ref[pl.ds(q_row0, pr), col] = m_new[rows]
                    l_ref[pl.ds(q_row0, pr), col] = l_new[rows]
                    acc_ref[pl.ds(q_row0, pr), col] = acc_new[rows]

            scores, maxes, probs, results = {}, {}, {}, []
            for t in range(DIL_UNROLL + 4):
                if t < DIL_UNROLL:
                    (r, i, kb, q_row0, k_row0, m_old) = tiles[t]
                    q = jnp.concatenate(pieces(q_ref, q_row0, pr, r), axis=0).astype(BF16)
                    k = jnp.concatenate(pieces(k_ref, k_row0, kr, r), axis=0).astype(BF16)
                    scores[t] = _dot_nt(q, k) + bias_ref[2 * ci + i - kb]
                if 0 <= t - 1 < DIL_UNROLL:
                    m_tile = jnp.max(scores[t - 1], axis=1, keepdims=True)
                    m_old = tiles[t - 1][5]
                    maxes[t - 1] = jnp.broadcast_to(m_tile, (BLK, LANES)) if ci == 0 else jnp.maximum(m_old, m_tile)
                if 0 <= t - 2 < DIL_UNROLL:
                    m_new = maxes[t - 2]
                    probs[t - 2] = jnp.exp2(scores.pop(t - 2) - jnp.concatenate([m_new, m_new], axis=1))
                if 0 <= t - 3 < DIL_UNROLL:
                    (r, i, kb, q_row0, k_row0, m_old) = tiles[t - 3]
                    p = probs.pop(t - 3)
                    v = jnp.concatenate(pieces(v_ref, k_row0, kr, r), axis=0).astype(BF16)
                    v_sum = jnp.concatenate([v, jnp.ones((2 * BLK, LANES), BF16)], axis=1)
                    pv = _dot(p.astype(BF16), v_sum)
                    results.append((r, q_row0, m_old, maxes.pop(t - 3), pv[:, HEAD_DIM:], pv[:, :HEAD_DIM]))
                if 0 <= t - 4 < DIL_UNROLL:
                    update(*results[t - 4])
            return carry

        lax.fori_loop(0, seq // (BLK * DIL_UNROLL), body, 0)
    o_ref[...] = acc_ref[...] / l_ref[...]


def dilated_attention(pf, batch, seq):
    assert all(DIL_RES % d == 0 for _, d in DILATED_CONFIGS)
    assert seq % (BLK * DIL_RES) == 0 and seq // DIL_RES >= 2 * BLK and (seq // BLK) % DIL_UNROLL == 0
    rows = seq // DIL_RES
    width = DIL_RES * HEAD_DIM
    view = pf

    def spec(off):
        return pl.BlockSpec((None, None, rows, width), lambda b, h: (off + h, b, 0, 0))

    out = pl.pallas_call(
        functools.partial(_dilated_kernel, seq=seq),
        grid=(batch, HEADS_A),
        in_specs=[spec(PF_QA), spec(PF_KA), spec(PF_VA)],
        out_specs=spec(0),
        out_shape=jax.ShapeDtypeStruct((HEADS_A, batch, rows, width), F32),
        scratch_shapes=[pltpu.VMEM((rows, width), F32)] * 3
        + [pltpu.VMEM((2 * len(DILATED_CONFIGS), BLK, 2 * BLK), F32)],
        compiler_params=_params("parallel", "parallel"),
        name="dilated_attention",
    )(view, view, view)
    return out


def _compress_kernel(t_ref, pe_ref, w1_ref, w2_ref, o_ref, ot_ref, *, seq):
    n_chunks = seq // CMP_STRIDE
    first = jnp.zeros((n_chunks, HEAD_DIM), F32)
    second = jnp.zeros((n_chunks, HEAD_DIM), F32)
    for i in range(CMP_STRIDE):
        ti = t_ref[:, i * HEAD_DIM:(i + 1) * HEAD_DIM]
        first += _dot((ti + pe_ref[pl.ds(i, 1), :]).astype(BF16), w1_ref[i])
        second += _dot((ti + pe_ref[pl.ds(CMP_STRIDE + i, 1), :]).astype(BF16), w1_ref[CMP_STRIDE + i])
    pre = first + pltpu.roll(second, n_chunks - 1, axis=0)
    out = _dot(jax.nn.gelu(pre).astype(BF16), w2_ref[...])
    o_ref[...] = out.astype(o_ref.dtype)
    ot_ref[...] = out.T.astype(ot_ref.dtype)


def compress(pf, pe, w1, w2, batch, seq):
    assert CMP_LEN == 2 * CMP_STRIDE and CMP_STRIDE == VIEW_RES
    n_chunks = seq // CMP_STRIDE
    pf4 = pf
    bg = batch * KV_GROUPS_B
    return pl.pallas_call(
        functools.partial(_compress_kernel, seq=seq),
        grid=(2, batch, KV_GROUPS_B),
        in_specs=[pl.BlockSpec((None, None, n_chunks, VIEW_RES * HEAD_DIM),
                               lambda kv, b, g: (PF_KC + kv * KV_GROUPS_B + g, b, 0, 0)),
                  pl.BlockSpec((None, CMP_LEN, HEAD_DIM), lambda kv, b, g: (kv, 0, 0)),
                  pl.BlockSpec((None, CMP_LEN, HEAD_DIM, HEAD_DIM), lambda kv, b, g: (kv, 0, 0, 0)),
                  pl.BlockSpec((None, HEAD_DIM, HEAD_DIM), lambda kv, b, g: (kv, 0, 0))],
        out_specs=[pl.BlockSpec((None, None, n_chunks, HEAD_DIM), lambda kv, b, g: (kv, b * KV_GROUPS_B + g, 0, 0)),
                   pl.BlockSpec((None, None, HEAD_DIM, n_chunks), lambda kv, b, g: (kv, b * KV_GROUPS_B + g, 0, 0))],
        out_shape=[jax.ShapeDtypeStruct((2, bg, n_chunks, HEAD_DIM), BF16),
                   jax.ShapeDtypeStruct((2, bg, HEAD_DIM, n_chunks), BF16)],
        compiler_params=_params("parallel", "parallel", "parallel"),
        name="compress",
    )(pf4, pe, w1, w2)


NSA_TQ = 512
NSA_TK = 512


def _select_blocks(score):
    n_s, tq = score.shape
    groups = n_s // 8
    rows8 = [score[8 * v:8 * v + 8, :] for v in range(groups)]
    rank8 = [jnp.zeros((8, tq), F32) for _ in range(groups)]
    sub = lax.broadcasted_iota(jnp.int32, (8, 1), 0)
    for jp in range(n_s):
        vp, sp = divmod(jp, 8)
        row = jnp.broadcast_to(rows8[vp][sp:sp + 1, :], (8, tq))
        for v in range(groups):
            if v > vp:
                beats = jnp.where(row >= rows8[v], 1.0, 0.0)
            elif v < vp:
                beats = jnp.where(row > rows8[v], 1.0, 0.0)
            else:
                beats = jnp.where(sub > sp, jnp.where(row >= rows8[v], 1.0, 0.0),
                                  jnp.where(row > rows8[v], 1.0, 0.0))
            rank8[v] = rank8[v] + beats
    rank = jnp.concatenate(rank8, axis=0)
    return (rank < N_SELECT) & (score > -jnp.inf)


SUM_ROWS = 16


def _with_sum_rows(vt):
    return jnp.concatenate([vt, jnp.ones((SUM_ROWS, vt.shape[1]), vt.dtype)], axis=0)


def _flash_block(s, vt_sum, carry):
    m_i, acc = carry
    m_new = jnp.maximum(m_i, jnp.max(s, axis=0, keepdims=True))
    alpha = jnp.exp2(m_i - m_new)
    p = jnp.exp2(s - m_new)
    return m_new, alpha * acc + _dot(vt_sum, p.astype(BF16))


def _normalized(acc):
    return acc[:HEAD_DIM] * (1.0 / acc[HEAD_DIM:HEAD_DIM + 1])


def _nsa_kernel(q_ref, kc_ref, vct_ref, ks_ref, vst_ref, kw_ref, vwt_ref, e_ref, wb_ref, gl_ref, o_ref, gates_ref,
                *, seq):
    tq, tk = NSA_TQ, NSA_TK
    cols = REP_B * tq
    n_cp = seq // CMP_STRIDE
    n_s = seq // SLC_BLOCK
    qi = pl.program_id(2)
    t0 = qi * tq
    q_t = jnp.concatenate([q_ref[r] for r in range(REP_B)], axis=1)
    lane = lax.broadcasted_iota(jnp.int32, (1, tq), 1)
    tpos = t0 + lane

    def per_head(x):
        return jnp.concatenate([x] * REP_B, axis=1)

    csub = lax.broadcasted_iota(jnp.int32, (n_cp, 1), 0)
    c_ok = (csub * CMP_STRIDE + (CMP_LEN - 1) <= tpos) & (csub < n_cp - 1)
    sc = _dot(kc_ref[...], q_t) + per_head(jnp.where(c_ok, 0.0, NEG))
    e = jnp.exp2(sc - jnp.max(sc, axis=0, keepdims=True))
    has_block = per_head(jnp.where(tpos >= CMP_LEN - 1, 1.0, 0.0))
    p_cmp = e * (has_block / jnp.maximum(jnp.sum(e, axis=0, keepdims=True), 1e-30))
    o_cmp = _dot(vct_ref[...], p_cmp.astype(BF16))

    ws = WIN + tq
    w0 = pl.multiple_of(jnp.maximum(t0 - WIN, 0), tq)
    s_w = _dot(kw_ref[pl.ds(w0, ws), :], q_t) + per_head(wb_ref[...])
    p_w = jnp.exp2(s_w - jnp.max(s_w, axis=0, keepdims=True))
    o_win = _normalized(_dot(_with_sum_rows(vwt_ref[:, pl.ds(w0, ws)]), p_w.astype(BF16)))

    p_sum = p_cmp[:, 0:tq]
    for r in range(1, REP_B):
        p_sum = p_sum + p_cmp[:, r * tq:(r + 1) * tq]
    jj = lax.broadcasted_iota(jnp.int32, (n_s, 1), 0)
    cidx = lax.broadcasted_iota(jnp.int32, (1, n_cp), 1)
    ratio = SLC_BLOCK // CMP_STRIDE
    c_first = ratio * jj - (CMP_LEN // CMP_STRIDE - 1)
    hits = jnp.where((cidx >= c_first) & (cidx < ratio * (jj + 1)), 1.0, 0.0).astype(BF16)
    p_hi = p_sum.astype(BF16)
    p_lo = (p_sum - p_hi.astype(F32)).astype(BF16)
    imp = _dot(hits, p_hi) + _dot(hits, p_lo)
    qblk = tpos >> (SLC_BLOCK.bit_length() - 1)
    forced = (jj == 0) | (jj == qblk) | (jj == qblk - 1)
    valid = jj * SLC_BLOCK <= tpos
    score = jnp.where(forced, FORCE_SCORE, jnp.where(valid, imp, -jnp.inf))
    sel = _select_blocks(score)
    bias = jnp.where(sel, 0.0, SEL_BIAS)
    if n_s < LANES:
        bias = jnp.concatenate([bias, jnp.full((LANES - n_s, tq), SEL_BIAS, F32)], axis=0)
    q_aug = jnp.concatenate([q_t, per_head(bias.astype(BF16))], axis=0)

    def slc_scores(kt):
        k0 = pl.multiple_of(kt * tk, tk)
        k_aug = jnp.concatenate([ks_ref[pl.ds(k0, tk), :], e_ref[pl.ds(k0, tk), :]], axis=1)
        return _dot(k_aug, q_aug)

    def slc_values(kt):
        return _with_sum_rows(vst_ref[:, pl.ds(pl.multiple_of(kt * tk, tk), tk)])

    def slc_step(kt, stats):
        return _flash_block(slc_scores(kt), slc_values(kt), stats)

    last = (t0 + tq - 1) // tk
    stats = (jnp.full((1, cols), NEG, F32), jnp.zeros((HEAD_DIM + SUM_ROWS, cols), F32))
    stats = lax.fori_loop(0, last, slc_step, stats)
    tok = last * tk + lax.broadcasted_iota(jnp.int32, (tk, 1), 0)
    s_last = slc_scores(last) + per_head(jnp.where(tok <= tpos, 0.0, NEG))
    o_slc = _normalized(_flash_block(s_last, slc_values(last), stats)[1])

    gates_ref[...] = jax.nn.sigmoid(gl_ref[...].T)
    first = pl.program_id(1) * (REP_B * 3)

    def gate(k):
        return gates_ref[pl.ds(first + k, 1), :]

    for r in range(REP_B):
        sl = slice(r * tq, (r + 1) * tq)
        merged = (gate(3 * r) * o_cmp[:, sl] + gate(3 * r + 1) * o_slc[:, sl]
                  + gate(3 * r + 2) * o_win[:, sl])
        o_ref[:, r * HEAD_DIM:(r + 1) * HEAD_DIM] = merged.T


def nsa_attention(pr, pc, kvc, kvct, gl, batch, seq):
    tq = NSA_TQ
    n_s = seq // SLC_BLOCK
    assert seq % NSA_TK == 0 and NSA_TK % tq == 0 and seq >= WIN + tq and n_s <= LANES and n_s % 8 == 0
    assert WIN % tq == 0
    assert SLC_BLOCK & (SLC_BLOCK - 1) == 0
    n_cp = seq // CMP_STRIDE
    nq = seq // tq
    pr4 = pr.reshape(pr.shape[0], batch, seq, HEAD_DIM)
    block_onehot = (jnp.arange(seq)[:, None] // SLC_BLOCK == jnp.arange(LANES)[None, :]).astype(BF16)
    lead = jnp.arange(WIN // tq + 1)[:, None, None] * tq
    dist = lead + jnp.arange(tq)[None, None, :] - jnp.arange(WIN + tq)[None, :, None]
    win_bias = jnp.where((dist >= 0) & (dist <= WIN - 1), 0.0, NEG).astype(F32)

    def k_spec(off):
        return pl.BlockSpec((None, None, seq, HEAD_DIM), lambda b, g, i: (off + g, b, 0, 0))

    def vt_spec(off):
        return pl.BlockSpec((None, None, HEAD_DIM, seq), lambda b, g, i: (off + g, b, 0, 0))

    return pl.pallas_call(
        functools.partial(_nsa_kernel, seq=seq),
        grid=(batch, KV_GROUPS_B, nq),
        in_specs=[pl.BlockSpec((REP_B, None, HEAD_DIM, tq), lambda b, g, i: (PC_QB // REP_B + g, b, 0, i)),
                  pl.BlockSpec((None, None, n_cp, HEAD_DIM), lambda b, g, i: (0, b * KV_GROUPS_B + g, 0, 0)),
                  pl.BlockSpec((None, None, HEAD_DIM, n_cp), lambda b, g, i: (1, b * KV_GROUPS_B + g, 0, 0)),
                  k_spec(PR_KS), vt_spec(PC_VS), k_spec(PR_KW), vt_spec(PC_VW),
                  pl.BlockSpec((seq, LANES), lambda b, g, i: (0, 0)),
                  pl.BlockSpec((None, WIN + tq, tq), lambda b, g, i: (jnp.minimum(i, WIN // tq), 0, 0)),
                  pl.BlockSpec((tq, LANES), lambda b, g, i: (b * nq + i, 0))],
        out_specs=pl.BlockSpec((tq, REP_B * HEAD_DIM), lambda b, g, i: (b * nq + i, g)),
        out_shape=jax.ShapeDtypeStruct((batch * seq, WIDTH_B), F32),
        scratch_shapes=[pltpu.VMEM((LANES, tq), F32)],
        compiler_params=_params("parallel", "parallel", "arbitrary"),
        name="nsa_attention",
    )(pc, kvc, kvct, pr4, pc, pr4, pc, block_onehot, win_bias, gl)


def _mm_res_kernel(a_ref, w_ref, r_ref, o_ref):
    o_ref[...] = _dot(a_ref[...], w_ref[...]) + r_ref[...]


def matmul_residual(a, w, res, tm, tn, weights_resident=False):
    m, k = a.shape
    n = w.shape[1]
    if weights_resident:
        grid = (n // tn, m // tm)
        row, col = (lambda j, i: i), (lambda j, i: j)
    else:
        grid = (m // tm, n // tn)
        row, col = (lambda i, j: i), (lambda i, j: j)
    return pl.pallas_call(
        _mm_res_kernel,
        grid=grid,
        in_specs=[pl.BlockSpec((tm, k), lambda p, q: (row(p, q), 0)),
                  pl.BlockSpec((k, tn), lambda p, q: (0, col(p, q))),
                  pl.BlockSpec((tm, tn), lambda p, q: (row(p, q), col(p, q)))],
        out_specs=pl.BlockSpec((tm, tn), lambda p, q: (row(p, q), col(p, q))),
        out_shape=jax.ShapeDtypeStruct((m, n), F32),
        compiler_params=_params("parallel", "arbitrary"),
        name="matmul_residual",
    )(a, w, res)


def _ffn_up_kernel(h_ref, wg_ref, wu_ref, o_ref):
    h = h_ref[...]
    g = _dot(h, wg_ref[...].astype(BF16))
    u = _dot(h, wu_ref[...].astype(BF16))
    o_ref[...] = (g * jax.nn.sigmoid(g) * u).astype(o_ref.dtype)


def ffn_up(h, wg, wu, layer, tm, tn):
    m, k = h.shape
    n = wg.shape[2]
    return pl.pallas_call(
        _ffn_up_kernel,
        grid=(m // tm, n // tn),
        in_specs=[pl.BlockSpec((tm, k), lambda i, j: (i, 0)),
                  pl.BlockSpec((None, k, tn), lambda i, j: (layer, 0, j)),
                  pl.BlockSpec((None, k, tn), lambda i, j: (layer, 0, j))],
        out_specs=pl.BlockSpec((tm, tn), lambda i, j: (i, j)),
        out_shape=jax.ShapeDtypeStruct((m, n), BF16),
        compiler_params=_params("parallel", "arbitrary"),
        name="ffn_up",
    )(h, wg, wu)


def _cast_kernel(w_ref, o_ref):
    o_ref[...] = w_ref[...].astype(o_ref.dtype)


def cast_columns(w, layer, n_cols, tk=512, tn=2816):
    _, k, _ = w.shape
    return pl.pallas_call(
        _cast_kernel,
        grid=(k // tk, n_cols // tn),
        in_specs=[pl.BlockSpec((None, tk, tn), lambda i, j: (layer, i, j))],
        out_specs=pl.BlockSpec((tk, tn), lambda i, j: (i, j)),
        out_shape=jax.ShapeDtypeStruct((k, n_cols), BF16),
        compiler_params=_params("parallel", "parallel"),
        name="cast_columns",
    )(w)


def _layer(x, norm_attn, w_in, layer, ck_pe, ck_w1, ck_w2, cv_pe, cv_w1, cv_w2,
           out_norm_a, out_norm_b, w_out, norm_ffn, w_gate, w_up, w_down, cos_t, sin_t, batch, seq):
    w_main = w_in.astype(BF16)
    w_gl = lax.slice(w_in, (layer, 0, D_MAIN), (layer + 1, D_MODEL, D_MAIN + N_GATES))
    w_gl = jnp.pad(w_gl.reshape(D_MODEL, N_GATES), ((0, 0), (0, LANES - N_GATES))).astype(BF16)
    pe = jnp.stack([ck_pe, cv_pe])
    w1 = jnp.stack([ck_w1, cv_w1]).reshape(2, CMP_LEN, HEAD_DIM, HEAD_DIM).astype(BF16)
    w2 = jnp.stack([ck_w2, cv_w2]).astype(BF16)

    h, gl = rms_gate(x, norm_attn, w_gl)
    pf = project(h, w_main, layer, cos_t, sin_t, F32_TILES, F32, batch, seq, layout="view")
    pr = project(h, w_main, layer, cos_t, sin_t, ROW_TILES, BF16, batch, seq)
    pc = project(h, w_main, layer, cos_t, sin_t, COL_TILES, BF16, batch, seq, layout="cols")
    o_a = dilated_attention(pf, batch, seq)
    kvc, kvct = compress(pf, pe, w1, w2, batch, seq)
    o_b = nsa_attention(pr, pc, kvc, kvct, gl, batch, seq)
    mixed = rms_pair(o_a, o_b, out_norm_a, out_norm_b)
    x1 = matmul_residual(mixed, w_out.astype(BF16), x, *OUT_PROJ_TILE, weights_resident=True)
    h2 = rms(x1, norm_ffn, BF16)
    act = ffn_up(h2, w_gate, w_up, layer, *FFN_UP_TILE)
    x2 = matmul_residual(act, w_down.astype(BF16), x1, *FFN_DOWN_TILE, weights_resident=True)
    return x2


def kernel(x, norm_attn, w_in, ck_pe, ck_w1, ck_w2, cv_pe, cv_w1, cv_w2, out_norm_a, out_norm_b, w_out,
           norm_ffn, w_gate, w_up, w_down, norm_final):
    batch, seq, d = x.shape
    depth = w_in.shape[0]
    cos_t, sin_t = _rope_tables(seq)
    xf = x.reshape(batch * seq, d)
    for l in range(depth):
        xf = _layer(xf, norm_attn[l], w_in, l, ck_pe[l], ck_w1[l], ck_w2[l], cv_pe[l], cv_w1[l], cv_w2[l],
                    out_norm_a[l], out_norm_b[l], w_out[l], norm_ffn[l], w_gate, w_up, w_down[l],
                    cos_t, sin_t, batch, seq)
    return rms(xf, norm_final, F32).reshape(batch, seq, d)
```

```python
import functools
import math

import jax
import jax.numpy as jnp
from jax import lax
from jax.experimental import pallas as pl
from jax.experimental.pallas import tpu as pltpu

F32 = jnp.float32
BF16 = jnp.bfloat16

D_MODEL = 4096
HEAD_DIM = 128
HEADS_A = 16
HEADS_B = 16
KV_GROUPS_B = 4
REP_B = HEADS_B // KV_GROUPS_B
WIDTH_A = HEADS_A * HEAD_DIM
WIDTH_B = HEADS_B * HEAD_DIM
DILATED_CONFIGS = ((128, 1), (512, 4), (2048, 16))
BLK = 128
CMP_LEN = 32
CMP_STRIDE = 16
SLC_BLOCK = 64
N_SELECT = 16
WIN = 512
FORCE_SCORE = 1e6
ROPE_THETA = 500000.0
ROPE_DIM = HEAD_DIM // 4
EPS = 1e-5
Q_SCALE = HEAD_DIM ** -0.5 * math.log2(math.e)
N_GATES = 3 * HEADS_B
D_MAIN = 3 * WIDTH_A + WIDTH_B + 6 * KV_GROUPS_B * HEAD_DIM

VMEM_LIMIT_BYTES = 56 * 1024 * 1024
LANES = 128
NEG = -1e30
SEL_BIAS = -32768.0

PROJ_TILE = 4 * HEAD_DIM
F32_TILES = tuple(range(12)) + (16, 17)
ROW_TILES = (18, 20)
COL_TILES = (12, 13, 14, 15, 19, 21)
Q_TILES = (0, 1, 2, 3, 12, 13, 14, 15)
V_TILES = (8, 9, 10, 11, 17, 19, 21)
PF_QA, PF_KA, PF_VA, PF_KC, PF_VC = 0, 16, 32, 48, 52
PR_KS, PR_KW = 0, 4
PC_QB, PC_VS, PC_VW = 0, 16, 20


def _params(*sem):
    return pltpu.CompilerParams(dimension_semantics=sem, vmem_limit_bytes=VMEM_LIMIT_BYTES)


def _dot(a, b):
    return jnp.dot(a, b, preferred_element_type=F32)


def _dot_nt(a, b):
    return lax.dot_general(a, b, (((1,), (1,)), ((), ())), preferred_element_type=F32)


def _rms(x, g):
    return x * lax.rsqrt(jnp.mean(x * x, axis=-1, keepdims=True) + EPS) * g


def _rms_gate_kernel(x_ref, g_ref, wgl_ref, h_ref, gl_ref):
    hb = _rms(x_ref[...], g_ref[...]).astype(BF16)
    h_ref[...] = hb
    gl_ref[...] = _dot(hb, wgl_ref[...])


def rms_gate(x, gain, w_gl, tm=512):
    m, d = x.shape
    ng = w_gl.shape[1]
    return pl.pallas_call(
        _rms_gate_kernel,
        grid=(m // tm,),
        in_specs=[pl.BlockSpec((tm, d), lambda i: (i, 0)),
                  pl.BlockSpec((1, d), lambda i: (0, 0)),
                  pl.BlockSpec((d, ng), lambda i: (0, 0))],
        out_specs=[pl.BlockSpec((tm, d), lambda i: (i, 0)),
                   pl.BlockSpec((tm, ng), lambda i: (i, 0))],
        out_shape=[jax.ShapeDtypeStruct((m, d), BF16), jax.ShapeDtypeStruct((m, ng), F32)],
        compiler_params=_params("parallel"),
        name="rms_gate",
    )(x, gain.reshape(1, d), w_gl)


def _rms_kernel(x_ref, g_ref, o_ref):
    o_ref[...] = _rms(x_ref[...], g_ref[...]).astype(o_ref.dtype)


def rms(x, gain, out_dtype, tm=512):
    m, d = x.shape
    return pl.pallas_call(
        _rms_kernel,
        grid=(m // tm,),
        in_specs=[pl.BlockSpec((tm, d), lambda i: (i, 0)),
                  pl.BlockSpec((1, d), lambda i: (0, 0))],
        out_specs=pl.BlockSpec((tm, d), lambda i: (i, 0)),
        out_shape=jax.ShapeDtypeStruct((m, d), out_dtype),
        compiler_params=_params("parallel"),
        name="rms",
    )(x, gain.reshape(1, d))


def _rms_pair_kernel(a_ref, b_ref, ga_ref, gb_ref, o_ref):
    n_heads, nv, _ = a_ref.shape
    wa = n_heads * HEAD_DIM
    lanes = [slice(r * HEAD_DIM, (r + 1) * HEAD_DIM) for r in range(VIEW_RES)]
    inv = []
    for r in range(VIEW_RES):
        sq = a_ref[0, :, lanes[r]] * a_ref[0, :, lanes[r]]
        for hh in range(1, n_heads):
            sq = sq + a_ref[hh, :, lanes[r]] * a_ref[hh, :, lanes[r]]
        inv.append(lax.rsqrt(jnp.sum(sq, axis=-1, keepdims=True) * (1.0 / wa) + EPS))
    for hh in range(n_heads):
        by_res = jnp.stack([a_ref[hh, :, lanes[r]] * inv[r] for r in range(VIEW_RES)], axis=0)
        tokens = jnp.swapaxes(by_res, 0, 1).reshape(nv * VIEW_RES, HEAD_DIM)
        cols = slice(hh * HEAD_DIM, (hh + 1) * HEAD_DIM)
        o_ref[:, cols] = (tokens * ga_ref[:, cols]).astype(o_ref.dtype)
    o_ref[:, wa:] = _rms(b_ref[...], gb_ref[...]).astype(o_ref.dtype)


def rms_pair(a, b, ga, gb, tm=512):
    n_heads, batch, rows, width = a.shape
    assert width == VIEW_RES * HEAD_DIM and (rows * VIEW_RES) % tm == 0
    wa = n_heads * HEAD_DIM
    m, wb = b.shape
    tiles_per_seq = rows * VIEW_RES // tm
    return pl.pallas_call(
        _rms_pair_kernel,
        grid=(m // tm,),
        in_specs=[pl.BlockSpec((n_heads, None, tm // VIEW_RES, width),
                               lambda i: (0, i // tiles_per_seq, i % tiles_per_seq, 0)),
                  pl.BlockSpec((tm, wb), lambda i: (i, 0)),
                  pl.BlockSpec((1, wa), lambda i: (0, 0)),
                  pl.BlockSpec((1, wb), lambda i: (0, 0))],
        out_specs=pl.BlockSpec((tm, wa + wb), lambda i: (i, 0)),
        out_shape=jax.ShapeDtypeStruct((m, wa + wb), BF16),
        compiler_params=_params("parallel"),
        name="rms_pair",
    )(a, b, ga.reshape(1, wa), gb.reshape(1, wb))


def _rope_tables(seq):
    inv = ROPE_THETA ** (-jnp.arange(0, ROPE_DIM, 2, dtype=F32) / ROPE_DIM)
    ang = jnp.arange(seq, dtype=F32)[:, None] * inv[None, :]
    cos, sin = jnp.cos(ang), jnp.sin(ang)
    ones = jnp.ones((seq, HEAD_DIM - ROPE_DIM), F32)
    cos_t = jnp.concatenate([cos, cos, ones], axis=1)
    sin_t = jnp.concatenate([-sin, sin, 0.0 * ones], axis=1)
    return cos_t, sin_t


PROJ_ROWS = 256
VIEW_RES = DILATED_CONFIGS[-1][1]


def _proj_kernel(h_ref, w_ref, cos_ref, sin_ref, o_ref, *, src_tiles, layout):
    jt = pl.program_id(1)
    is_v = functools.reduce(jnp.logical_or, [jt == n for n, t in enumerate(src_tiles) if t in V_TILES], False)
    is_q = functools.reduce(jnp.logical_or, [jt == n for n, t in enumerate(src_tiles) if t in Q_TILES], False)
    heads_per_tile = PROJ_TILE // HEAD_DIM
    tm = h_ref.shape[0]
    scale = jnp.where(is_q, Q_SCALE, 1.0).astype(F32)
    lane = lax.broadcasted_iota(jnp.int32, (1, HEAD_DIM), 1)
    first_half = lane < ROPE_DIM // 2

    for c in range(tm // PROJ_ROWS):
        rows = slice(c * PROJ_ROWS, (c + 1) * PROJ_ROWS)
        y = _dot(h_ref[rows, :], w_ref[...])
        cos_t = jnp.where(is_v, 1.0, cos_ref[rows, :]) * scale
        sin_t = jnp.where(is_v, 0.0, sin_ref[rows, :]) * scale
        for hh in range(heads_per_tile):
            yh = y[:, hh * HEAD_DIM:(hh + 1) * HEAD_DIM]
            partner = jnp.where(first_half,
                                pltpu.roll(yh, HEAD_DIM - ROPE_DIM // 2, axis=1),
                                pltpu.roll(yh, ROPE_DIM // 2, axis=1))
            val = yh * cos_t + partner * sin_t
            if layout == "cols":
                for cc in range(PROJ_ROWS // LANES):
                    col0 = c * PROJ_ROWS + cc * LANES
                    o_ref[hh, :, col0:col0 + LANES] = val[cc * LANES:(cc + 1) * LANES, :].T.astype(o_ref.dtype)
            elif layout == "view":
                nv = PROJ_ROWS // VIEW_RES
                by_res = jnp.swapaxes(val.reshape(nv, VIEW_RES, HEAD_DIM), 0, 1)
                for r in range(VIEW_RES):
                    o_ref[hh, c * nv:(c + 1) * nv, r * HEAD_DIM:(r + 1) * HEAD_DIM] = by_res[r].astype(o_ref.dtype)
            else:
                o_ref[hh, rows, :] = val.astype(o_ref.dtype)


def project(h, w, layer, cos_t, sin_t, src_tiles, out_dtype, batch, seq, layout="rows", tm=2048):
    m, d = h.shape
    n_tiles = len(src_tiles)
    heads_per_tile = PROJ_TILE // HEAD_DIM
    tm = min(tm, seq)
    pos_blocks = seq // tm

    def w_map(i, j):
        col = functools.reduce(lambda acc, nt: jnp.where(j == nt[0], nt[1], acc),
                               list(enumerate(src_tiles)), 0)
        return (layer, 0, col)

    if layout == "cols":
        out_spec = pl.BlockSpec((heads_per_tile, None, HEAD_DIM, tm),
                                lambda i, j: (j, i // pos_blocks, 0, i % pos_blocks))
        out_shape = jax.ShapeDtypeStruct((n_tiles * heads_per_tile, batch, HEAD_DIM, seq), out_dtype)
    elif layout == "view":
        out_spec = pl.BlockSpec((heads_per_tile, None, tm // VIEW_RES, VIEW_RES * HEAD_DIM),
                                lambda i, j: (j, i // pos_blocks, i % pos_blocks, 0))
        out_shape = jax.ShapeDtypeStruct((n_tiles * heads_per_tile, batch, seq // VIEW_RES, VIEW_RES * HEAD_DIM),
                                         out_dtype)
    else:
        out_spec = pl.BlockSpec((heads_per_tile, tm, HEAD_DIM), lambda i, j: (j, i, 0))
        out_shape = jax.ShapeDtypeStruct((n_tiles * heads_per_tile, m, HEAD_DIM), out_dtype)

    return pl.pallas_call(
        functools.partial(_proj_kernel, src_tiles=src_tiles, layout=layout),
        grid=(m // tm, n_tiles),
        in_specs=[pl.BlockSpec((tm, d), lambda i, j: (i, 0)),
                  pl.BlockSpec((None, d, PROJ_TILE), w_map),
                  pl.BlockSpec((tm, HEAD_DIM), lambda i, j: (i % pos_blocks, 0)),
                  pl.BlockSpec((tm, HEAD_DIM), lambda i, j: (i % pos_blocks, 0))],
        out_specs=out_spec,
        out_shape=out_shape,
        compiler_params=_params("parallel", "arbitrary"),
        name="project_" + layout,
    )(h, w, cos_t, sin_t)


def _aligned(x, multiple):
    return x if isinstance(x, int) else pl.multiple_of(x, multiple)


DIL_UNROLL = 8
DIL_RES = VIEW_RES


def _dilated_kernel(q_ref, k_ref, v_ref, o_ref, acc_ref, m_ref, l_ref, bias_ref, *, seq):
    for ci, (window, dil) in enumerate(DILATED_CONFIGS):
        band = window // dil
        n_pieces = DIL_RES // dil
        pr = BLK // n_pieces
        kr = 2 * pr
        tiles_per_res = seq // (BLK * dil)
        assert band <= BLK and pr % 8 == 0 and pr & (pr - 1) == 0

        q_row = lax.broadcasted_iota(jnp.int32, (BLK, 1), 0)
        k_row = lax.broadcasted_iota(jnp.int32, (1, 2 * BLK), 1)
        q_sub = n_pieces * (q_row & (pr - 1)) + (q_row >> (pr.bit_length() - 1))
        k_sub = n_pieces * (k_row & (kr - 1)) + (k_row >> (kr.bit_length() - 1))
        for lead in range(2):
            dist = (q_sub + lead * BLK) - k_sub
            bias_ref[2 * ci + lead] = jnp.where((dist >= 0) & (dist <= band), 0.0, NEG)

        def body(step, carry, ci=ci, dil=dil, band=band, n_pieces=n_pieces, pr=pr, kr=kr,
                 tiles_per_res=tiles_per_res):
            def pieces(ref, row0, nrows, r):
                return [ref[pl.ds(row0, nrows), pl.ds(_aligned((r + dil * a) * HEAD_DIM, HEAD_DIM), HEAD_DIM)]
                        for a in range(n_pieces)]

            tiles = []
            for u in range(DIL_UNROLL):
                idx = step * DIL_UNROLL + u
                r = idx // tiles_per_res
                i = idx - r * tiles_per_res
                kb = max(i - 1, 0) if isinstance(i, int) else jnp.maximum(i - 1, 0)
                q_row0 = _aligned(i * pr, pr)
                k_row0 = _aligned(kb * pr, pr)
                m_old = None if ci == 0 else jnp.concatenate(pieces(m_ref, q_row0, pr, r), axis=0)
                tiles.append((r, i, kb, q_row0, k_row0, m_old))

            def update(r, q_row0, m_old, m_new, l_tile, pv):
                if ci == 0:
                    l_new = l_tile
                    acc_new = pv
                else:
                    alpha = jnp.exp2(m_old - m_new)
                    l_new = alpha * jnp.concatenate(pieces(l_ref, q_row0, pr, r), axis=0) + l_tile
                    acc_new = alpha * jnp.concatenate(pieces(acc_ref, q_row0, pr, r), axis=0) + pv
                for a in range(n_pieces):
                    col = pl.ds(_aligned((r + dil * a) * HEAD_DIM, HEAD_DIM), HEAD_DIM)
                    rows = slice(a * pr, (a + 1) * pr)
                    m_ref[pl.ds(q_row0, pr), col] = m_new[rows]
                    l_ref[pl.ds(q_row0, pr), col] = l_new[rows]
                    acc_ref[pl.ds(q_row0, pr), col] = acc_new[rows]

            scores, maxes, probs, results = {}, {}, {}, []
            for t in range(DIL_UNROLL + 4):
                if t < DIL_UNROLL:
                    (r, i, kb, q_row0, k_row0, m_old) = tiles[t]
                    q = jnp.concatenate(pieces(q_ref, q_row0, pr, r), axis=0).astype(BF16)
                    k = jnp.concatenate(pieces(k_ref, k_row0, kr, r), axis=0).astype(BF16)
                    scores[t] = _dot_nt(q, k) + bias_ref[2 * ci + i - kb]
                if 0 <= t - 1 < DIL_UNROLL:
                    m_tile = jnp.max(scores[t - 1], axis=1, keepdims=True)
                    m_old = tiles[t - 1][5]
                    maxes[t - 1] = jnp.broadcast_to(m_tile, (BLK, LANES)) if ci == 0 else jnp.maximum(m_old, m_tile)
                if 0 <= t - 2 < DIL_UNROLL:
                    m_new = maxes[t - 2]
                    probs[t - 2] = jnp.exp2(scores.pop(t - 2) - jnp.concatenate([m_new, m_new], axis=1))
                if 0 <= t - 3 < DIL_UNROLL:
                    (r, i, kb, q_row0, k_row0, m_old) = tiles[t - 3]
                    p = probs.pop(t - 3)
                    v = jnp.concatenate(pieces(v_ref, k_row0, kr, r), axis=0).astype(BF16)
                    v_sum = jnp.concatenate([v, jnp.ones((2 * BLK, LANES), BF16)], axis=1)
                    pv = _dot(p.astype(BF16), v_sum)
                    results.append((r, q_row0, m_old, maxes.pop(t - 3), pv[:, HEAD_DIM:], pv[:, :HEAD_DIM]))
                if 0 <= t - 4 < DIL_UNROLL:
                    update(*results[t - 4])
            return carry

        for step in range(seq // (BLK * DIL_UNROLL)):
            body(step, 0)
    o_ref[...] = acc_ref[...] / l_ref[...]


def dilated_attention(pf, batch, seq):
    assert all(DIL_RES % d == 0 for _, d in DILATED_CONFIGS)
    assert seq % (BLK * DIL_RES) == 0 and seq // DIL_RES >= 2 * BLK and (seq // BLK) % DIL_UNROLL == 0
    rows = seq // DIL_RES
    width = DIL_RES * HEAD_DIM
    view = pf

    def spec(off):
        return pl.BlockSpec((None, None, rows, width), lambda b, h: (off + h, b, 0, 0))

    out = pl.pallas_call(
        functools.partial(_dilated_kernel, seq=seq),
        grid=(batch, HEADS_A),
        in_specs=[spec(PF_QA), spec(PF_KA), spec(PF_VA)],
        out_specs=spec(0),
        out_shape=jax.ShapeDtypeStruct((HEADS_A, batch, rows, width), F32),
        scratch_shapes=[pltpu.VMEM((rows, width), F32)] * 3
        + [pltpu.VMEM((2 * len(DILATED_CONFIGS), BLK, 2 * BLK), F32)],
        compiler_params=_params("parallel", "parallel"),
        name="dilated_attention",
    )(view, view, view)
    return out


def _compress_kernel(t_ref, pe_ref, w1_ref, w2_ref, o_ref, ot_ref, *, seq):
    n_chunks = seq // CMP_STRIDE
    first = jnp.zeros((n_chunks, HEAD_DIM), F32)
    second = jnp.zeros((n_chunks, HEAD_DIM), F32)
    for i in range(CMP_STRIDE):
        ti = t_ref[:, i * HEAD_DIM:(i + 1) * HEAD_DIM]
        first += _dot((ti + pe_ref[pl.ds(i, 1), :]).astype(BF16), w1_ref[i])
        second += _dot((ti + pe_ref[pl.ds(CMP_STRIDE + i, 1), :]).astype(BF16), w1_ref[CMP_STRIDE + i])
    pre = first + pltpu.roll(second, n_chunks - 1, axis=0)
    out = _dot(jax.nn.gelu(pre).astype(BF16), w2_ref[...])
    o_ref[...] = out.astype(o_ref.dtype)
    ot_ref[...] = out.T.astype(ot_ref.dtype)


def compress(pf, pe, w1, w2, batch, seq):
    assert CMP_LEN == 2 * CMP_STRIDE and CMP_STRIDE == VIEW_RES
    n_chunks = seq // CMP_STRIDE
    pf4 = pf
    bg = batch * KV_GROUPS_B
    return pl.pallas_call(
        functools.partial(_compress_kernel, seq=seq),
        grid=(2, batch, KV_GROUPS_B),
        in_specs=[pl.BlockSpec((None, None, n_chunks, VIEW_RES * HEAD_DIM),
                               lambda kv, b, g: (PF_KC + kv * KV_GROUPS_B + g, b, 0, 0)),
                  pl.BlockSpec((None, CMP_LEN, HEAD_DIM), lambda kv, b, g: (kv, 0, 0)),
                  pl.BlockSpec((None, CMP_LEN, HEAD_DIM, HEAD_DIM), lambda kv, b, g: (kv, 0, 0, 0)),
                  pl.BlockSpec((None, HEAD_DIM, HEAD_DIM), lambda kv, b, g: (kv, 0, 0))],
        out_specs=[pl.BlockSpec((None, None, n_chunks, HEAD_DIM), lambda kv, b, g: (kv, b * KV_GROUPS_B + g, 0, 0)),
                   pl.BlockSpec((None, None, HEAD_DIM, n_chunks), lambda kv, b, g: (kv, b * KV_GROUPS_B + g, 0, 0))],
        out_shape=[jax.ShapeDtypeStruct((2, bg, n_chunks, HEAD_DIM), BF16),
                   jax.ShapeDtypeStruct((2, bg, HEAD_DIM, n_chunks), BF16)],
        compiler_params=_params("parallel", "parallel", "parallel"),
        name="compress",
    )(pf4, pe, w1, w2)


NSA_TQ = 512
NSA_TK = 512


def _select_blocks(score):
    n_s, tq = score.shape
    groups = n_s // 8
    rows8 = [score[8 * v:8 * v + 8, :] for v in range(groups)]
    rank8 = [jnp.zeros((8, tq), F32) for _ in range(groups)]
    sub = lax.broadcasted_iota(jnp.int32, (8, 1), 0)
    for jp in range(n_s):
        vp, sp = divmod(jp, 8)
        row = jnp.broadcast_to(rows8[vp][sp:sp + 1, :], (8, tq))
        for v in range(groups):
            if v > vp:
                beats = jnp.where(row >= rows8[v], 1.0, 0.0)
            elif v < vp:
                beats = jnp.where(row > rows8[v], 1.0, 0.0)
            else:
                beats = jnp.where(sub > sp, jnp.where(row >= rows8[v], 1.0, 0.0),
                                  jnp.where(row > rows8[v], 1.0, 0.0))
            rank8[v] = rank8[v] + beats
    rank = jnp.concatenate(rank8, axis=0)
    return (rank < N_SELECT) & (score > -jnp.inf)


SUM_ROWS = 16


def _with_sum_rows(vt):
    return jnp.concatenate([vt, jnp.ones((SUM_ROWS, vt.shape[1]), vt.dtype)], axis=0)


def _flash_block(s, vt_sum, carry):
    m_i, acc = carry
    m_new = jnp.maximum(m_i, jnp.max(s, axis=0, keepdims=True))
    alpha = jnp.exp2(m_i - m_new)
    p = jnp.exp2(s - m_new)
    return m_new, alpha * acc + _dot(vt_sum, p.astype(BF16))


def _normalized(acc):
    return acc[:HEAD_DIM] * (1.0 / acc[HEAD_DIM:HEAD_DIM + 1])


def _nsa_kernel(q_ref, kc_ref, vct_ref, ks_ref, vst_ref, kw_ref, vwt_ref, e_ref, wb_ref, gl_ref, o_ref, gates_ref,
                *, seq):
    tq, tk = NSA_TQ, NSA_TK
    cols = REP_B * tq
    n_cp = seq // CMP_STRIDE
    n_s = seq // SLC_BLOCK
    qi = pl.program_id(2)
    t0 = qi * tq
    q_t = jnp.concatenate([q_ref[r] for r in range(REP_B)], axis=1)
    lane = lax.broadcasted_iota(jnp.int32, (1, tq), 1)
    tpos = t0 + lane

    def per_head(x):
        return jnp.concatenate([x] * REP_B, axis=1)

    csub = lax.broadcasted_iota(jnp.int32, (n_cp, 1), 0)
    c_ok = (csub * CMP_STRIDE + (CMP_LEN - 1) <= tpos) & (csub < n_cp - 1)
    sc = _dot(kc_ref[...], q_t) + per_head(jnp.where(c_ok, 0.0, NEG))
    e = jnp.exp2(sc - jnp.max(sc, axis=0, keepdims=True))
    has_block = per_head(jnp.where(tpos >= CMP_LEN - 1, 1.0, 0.0))
    p_cmp = e * (has_block / jnp.maximum(jnp.sum(e, axis=0, keepdims=True), 1e-30))
    o_cmp = _dot(vct_ref[...], p_cmp.astype(BF16))

    ws = WIN + tq
    w0 = pl.multiple_of(jnp.maximum(t0 - WIN, 0), tq)
    s_w = _dot(kw_ref[pl.ds(w0, ws), :], q_t) + per_head(wb_ref[...])
    p_w = jnp.exp2(s_w - jnp.max(s_w, axis=0, keepdims=True))
    o_win = _normalized(_dot(_with_sum_rows(vwt_ref[:, pl.ds(w0, ws)]), p_w.astype(BF16)))

    p_sum = p_cmp[:, 0:tq]
    for r in range(1, REP_B):
        p_sum = p_sum + p_cmp[:, r * tq:(r + 1) * tq]
    jj = lax.broadcasted_iota(jnp.int32, (n_s, 1), 0)
    cidx = lax.broadcasted_iota(jnp.int32, (1, n_cp), 1)
    ratio = SLC_BLOCK // CMP_STRIDE
    c_first = ratio * jj - (CMP_LEN // CMP_STRIDE - 1)
    hits = jnp.where((cidx >= c_first) & (cidx < ratio * (jj + 1)), 1.0, 0.0).astype(BF16)
    p_hi = p_sum.astype(BF16)
    p_lo = (p_sum - p_hi.astype(F32)).astype(BF16)
    imp = _dot(hits, p_hi) + _dot(hits, p_lo)
    qblk = tpos >> (SLC_BLOCK.bit_length() - 1)
    forced = (jj == 0) | (jj == qblk) | (jj == qblk - 1)
    valid = jj * SLC_BLOCK <= tpos
    score = jnp.where(forced, FORCE_SCORE, jnp.where(valid, imp, -jnp.inf))
    sel = _select_blocks(score)
    bias = jnp.where(sel, 0.0, SEL_BIAS)
    if n_s < LANES:
        bias = jnp.concatenate([bias, jnp.full((LANES - n_s, tq), SEL_BIAS, F32)], axis=0)
    q_aug = jnp.concatenate([q_t, per_head(bias.astype(BF16))], axis=0)

    def slc_scores(kt):
        k0 = pl.multiple_of(kt * tk, tk)
        k_aug = jnp.concatenate([ks_ref[pl.ds(k0, tk), :], e_ref[pl.ds(k0, tk), :]], axis=1)
        return _dot(k_aug, q_aug)

    def slc_values(kt):
        return _with_sum_rows(vst_ref[:, pl.ds(pl.multiple_of(kt * tk, tk), tk)])

    def slc_step(kt, stats):
        return _flash_block(slc_scores(kt), slc_values(kt), stats)

    last = (t0 + tq - 1) // tk
    stats = (jnp.full((1, cols), NEG, F32), jnp.zeros((HEAD_DIM + SUM_ROWS, cols), F32))
    stats = lax.fori_loop(0, last, slc_step, stats)
    tok = last * tk + lax.broadcasted_iota(jnp.int32, (tk, 1), 0)
    s_last = slc_scores(last) + per_head(jnp.where(tok <= tpos, 0.0, NEG))
    o_slc = _normalized(_flash_block(s_last, slc_values(last), stats)[1])

    gates_ref[...] = jax.nn.sigmoid(gl_ref[...].T)
    first = pl.program_id(1) * (REP_B * 3)

    def gate(k):
        return gates_ref[pl.ds(first + k, 1), :]

    for r in range(REP_B):
        sl = slice(r * tq, (r + 1) * tq)
        merged = (gate(3 * r) * o_cmp[:, sl] + gate(3 * r + 1) * o_slc[:, sl]
                  + gate(3 * r + 2) * o_win[:, sl])
        o_ref[:, r * HEAD_DIM:(r + 1) * HEAD_DIM] = merged.T


def nsa_attention(pr, pc, kvc, kvct, gl, batch, seq):
    tq = NSA_TQ
    n_s = seq // SLC_BLOCK
    assert seq % NSA_TK == 0 and NSA_TK % tq == 0 and seq >= WIN + tq and n_s <= LANES and n_s % 8 == 0
    assert WIN % tq == 0
    assert SLC_BLOCK & (SLC_BLOCK - 1) == 0
    n_cp = seq // CMP_STRIDE
    nq = seq // tq
    pr4 = pr.reshape(pr.shape[0], batch, seq, HEAD_DIM)
    block_onehot = (jnp.arange(seq)[:, None] // SLC_BLOCK == jnp.arange(LANES)[None, :]).astype(BF16)
    lead = jnp.arange(WIN // tq + 1)[:, None, None] * tq
    dist = lead + jnp.arange(tq)[None, None, :] - jnp.arange(WIN + tq)[None, :, None]
    win_bias = jnp.where((dist >= 0) & (dist <= WIN - 1), 0.0, NEG).astype(F32)

    def k_spec(off):
        return pl.BlockSpec((None, None, seq, HEAD_DIM), lambda b, g, i: (off + g, b, 0, 0))

    def vt_spec(off):
        return pl.BlockSpec((None, None, HEAD_DIM, seq), lambda b, g, i: (off + g, b, 0, 0))

    return pl.pallas_call(
        functools.partial(_nsa_kernel, seq=seq),
        grid=(batch, KV_GROUPS_B, nq),
        in_specs=[pl.BlockSpec((REP_B, None, HEAD_DIM, tq), lambda b, g, i: (PC_QB // REP_B + g, b, 0, i)),
                  pl.BlockSpec((None, None, n_cp, HEAD_DIM), lambda b, g, i: (0, b * KV_GROUPS_B + g, 0, 0)),
                  pl.BlockSpec((None, None, HEAD_DIM, n_cp), lambda b, g, i: (1, b * KV_GROUPS_B + g, 0, 0)),
                  k_spec(PR_KS), vt_spec(PC_VS), k_spec(PR_KW), vt_spec(PC_VW),
                  pl.BlockSpec((seq, LANES), lambda b, g, i: (0, 0)),
                  pl.BlockSpec((None, WIN + tq, tq), lambda b, g, i: (jnp.minimum(i, WIN // tq), 0, 0)),
                  pl.BlockSpec((tq, LANES), lambda b, g, i: (b * nq + i, 0))],
        out_specs=pl.BlockSpec((tq, REP_B * HEAD_DIM), lambda b, g, i: (b * nq + i, g)),
        out_shape=jax.ShapeDtypeStruct((batch * seq, WIDTH_B), F32),
        scratch_shapes=[pltpu.VMEM((LANES, tq), F32)],
        compiler_params=_params("parallel", "parallel", "arbitrary"),
        name="nsa_attention",
    )(pc, kvc, kvct, pr4, pc, pr4, pc, block_onehot, win_bias, gl)


def _mm_res_kernel(a_ref, w_ref, r_ref, o_ref):
    o_ref[...] = _dot(a_ref[...], w_ref[...]) + r_ref[...]


def matmul_residual(a, w, res, tm, tn, weights_resident=False):
    m, k = a.shape
    n = w.shape[1]
    if weights_resident:
        grid = (n // tn, m // tm)
        row, col = (lambda j, i: i), (lambda j, i: j)
    else:
        grid = (m // tm, n // tn)
        row, col = (lambda i, j: i), (lambda i, j: j)
    return pl.pallas_call(
        _mm_res_kernel,
        grid=grid,
        in_specs=[pl.BlockSpec((tm, k), lambda p, q: (row(p, q), 0)),
                  pl.BlockSpec((k, tn), lambda p, q: (0, col(p, q))),
                  pl.BlockSpec((tm, tn), lambda p, q: (row(p, q), col(p, q)))],
        out_specs=pl.BlockSpec((tm, tn), lambda p, q: (row(p, q), col(p, q))),
        out_shape=jax.ShapeDtypeStruct((m, n), F32),
        compiler_params=_params("parallel", "arbitrary"),
        name="matmul_residual",
    )(a, w, res)


def _ffn_up_kernel(h_ref, wg_ref, wu_ref, o_ref):
    h = h_ref[...]
    g = _dot(h, wg_ref[...].astype(BF16))
    u = _dot(h, wu_ref[...].astype(BF16))
    o_ref[...] = (g * jax.nn.sigmoid(g) * u).astype(o_ref.dtype)


def ffn_up(h, wg, wu, layer, tm, tn):
    m, k = h.shape
    n = wg.shape[2]
    return pl.pallas_call(
        _ffn_up_kernel,
        grid=(m // tm, n // tn),
        in_specs=[pl.BlockSpec((tm, k), lambda i, j: (i, 0)),
                  pl.BlockSpec((None, k, tn), lambda i, j: (layer, 0, j)),
                  pl.BlockSpec((None, k, tn), lambda i, j: (layer, 0, j))],
        out_specs=pl.BlockSpec((tm, tn), lambda i, j: (i, j)),
        out_shape=jax.ShapeDtypeStruct((m, n), BF16),
        compiler_params=_params("parallel", "arbitrary"),
        name="ffn_up",
    )(h, wg, wu)


def _cast_kernel(w_ref, o_ref):
    o_ref[...] = w_ref[...].astype(o_ref.dtype)


def cast_columns(w, layer, n_cols, tk=512, tn=2816):
    _, k, _ = w.shape
    return pl.pallas_call(
        _cast_kernel,
        grid=(k // tk, n_cols // tn),
        in_specs=[pl.BlockSpec((None, tk, tn), lambda i, j: (layer, i, j))],
        out_specs=pl.BlockSpec((tk, tn), lambda i, j: (i, j)),
        out_shape=jax.ShapeDtypeStruct((k, n_cols), BF16),
        compiler_params=_params("parallel", "parallel"),
        name="cast_columns",
    )(w)


def _layer(x, norm_attn, w_in, layer, ck_pe, ck_w1, ck_w2, cv_pe, cv_w1, cv_w2,
           out_norm_a, out_norm_b, w_out, norm_ffn, w_gate, w_up, w_down, cos_t, sin_t, batch, seq):
    w_main = w_in.astype(BF16)
    w_gl = lax.slice(w_in, (layer, 0, D_MAIN), (layer + 1, D_MODEL, D_MAIN + N_GATES))
    w_gl = jnp.pad(w_gl.reshape(D_MODEL, N_GATES), ((0, 0), (0, LANES - N_GATES))).astype(BF16)
    pe = jnp.stack([ck_pe, cv_pe])
    w1 = jnp.stack([ck_w1, cv_w1]).reshape(2, CMP_LEN, HEAD_DIM, HEAD_DIM).astype(BF16)
    w2 = jnp.stack([ck_w2, cv_w2]).astype(BF16)

    h, gl = rms_gate(x, norm_attn, w_gl)
    pf = project(h, w_main, layer, cos_t, sin_t, F32_TILES, F32, batch, seq, layout="view")
    pr = project(h, w_main, layer, cos_t, sin_t, ROW_TILES, BF16, batch, seq)
    pc = project(h, w_main, layer, cos_t, sin_t, COL_TILES, BF16, batch, seq, layout="cols")
    o_a = dilated_attention(pf, batch, seq)
    kvc, kvct = compress(pf, pe, w1, w2, batch, seq)
    o_b = nsa_attention(pr, pc, kvc, kvct, gl, batch, seq)
    mixed = rms_pair(o_a, o_b, out_norm_a, out_norm_b)
    x1 = matmul_residual(mixed, w_out.astype(BF16), x, tm=1024, tn=1024)
    h2 = rms(x1, norm_ffn, BF16)
    act = ffn_up(h2, w_gate, w_up, layer, tm=2048, tn=256)
    x2 = matmul_residual(act, w_down.astype(BF16), x1, tm=512, tn=512, weights_resident=True)
    return x2


def kernel(x, norm_attn, w_in, ck_pe, ck_w1, ck_w2, cv_pe, cv_w1, cv_w2, out_norm_a, out_norm_b, w_out,
           norm_ffn, w_gate, w_up, w_down, norm_final):
    batch, seq, d = x.shape
    depth = w_in.shape[0]
    cos_t, sin_t = _rope_tables(seq)
    xf = x.reshape(batch * seq, d)
    for l in range(depth):
        xf = _layer(xf, norm_attn[l], w_in, l, ck_pe[l], ck_w1[l], ck_w2[l], cv_pe[l], cv_w1[l], cv_w2[l],
                    out_norm_a[l], out_norm_b[l], w_out[l], norm_ffn[l], w_gate, w_up, w_down[l],
                    cos_t, sin_t, batch, seq)
    return rms(xf, norm_final, F32).reshape(batch, seq, d)
```

```python
import functools
import math

import jax
import jax.numpy as jnp
from jax import lax
from jax.experimental import pallas as pl
from jax.experimental.pallas import tpu as pltpu

F32 = jnp.float32
BF16 = jnp.bfloat16

D_MODEL = 4096
HEAD_DIM = 128
HEADS_A = 16
HEADS_B = 16
KV_GROUPS_B = 4
REP_B = HEADS_B // KV_GROUPS_B
WIDTH_A = HEADS_A * HEAD_DIM
WIDTH_B = HEADS_B * HEAD_DIM
DILATED_CONFIGS = ((128, 1), (512, 4), (2048, 16))
BLK = 128
CMP_LEN = 32
CMP_STRIDE = 16
SLC_BLOCK = 64
N_SELECT = 16
WIN = 512
FORCE_SCORE = 1e6
ROPE_THETA = 500000.0
ROPE_DIM = HEAD_DIM // 4
EPS = 1e-5
Q_SCALE = HEAD_DIM ** -0.5 * math.log2(math.e)
N_GATES = 3 * HEADS_B
D_MAIN = 3 * WIDTH_A + WIDTH_B + 6 * KV_GROUPS_B * HEAD_DIM

VMEM_LIMIT_BYTES = 56 * 1024 * 1024
LANES = 128
NEG = -1e30
SEL_BIAS = -32768.0

PROJ_TILE = 4 * HEAD_DIM
F32_TILES = tuple(range(12)) + (16, 17)
ROW_TILES = (18, 20)
COL_TILES = (12, 13, 14, 15, 19, 21)
Q_TILES = (0, 1, 2, 3, 12, 13, 14, 15)
V_TILES = (8, 9, 10, 11, 17, 19, 21)
PF_QA, PF_KA, PF_VA, PF_KC, PF_VC = 0, 16, 32, 48, 52
PR_KS, PR_KW = 0, 4
PC_QB, PC_VS, PC_VW = 0, 16, 20


def _params(*sem):
    return pltpu.CompilerParams(dimension_semantics=sem, vmem_limit_bytes=VMEM_LIMIT_BYTES)


def _dot(a, b):
    return jnp.dot(a, b, preferred_element_type=F32)


def _dot_nt(a, b):
    return lax.dot_general(a, b, (((1,), (1,)), ((), ())), preferred_element_type=F32)


def _rms(x, g):
    return x * lax.rsqrt(jnp.mean(x * x, axis=-1, keepdims=True) + EPS) * g


def _rms_gate_kernel(x_ref, g_ref, wgl_ref, h_ref, gl_ref):
    hb = _rms(x_ref[...], g_ref[...]).astype(BF16)
    h_ref[...] = hb
    gl_ref[...] = _dot(hb, wgl_ref[...])


def rms_gate(x, gain, w_gl, tm=512):
    m, d = x.shape
    ng = w_gl.shape[1]
    return pl.pallas_call(
        _rms_gate_kernel,
        grid=(m // tm,),
        in_specs=[pl.BlockSpec((tm, d), lambda i: (i, 0)),
                  pl.BlockSpec((1, d), lambda i: (0, 0)),
                  pl.BlockSpec((d, ng), lambda i: (0, 0))],
        out_specs=[pl.BlockSpec((tm, d), lambda i: (i, 0)),
                   pl.BlockSpec((tm, ng), lambda i: (i, 0))],
        out_shape=[jax.ShapeDtypeStruct((m, d), BF16), jax.ShapeDtypeStruct((m, ng), F32)],
        compiler_params=_params("parallel"),
        name="rms_gate",
    )(x, gain.reshape(1, d), w_gl)


def _rms_kernel(x_ref, g_ref, o_ref):
    o_ref[...] = _rms(x_ref[...], g_ref[...]).astype(o_ref.dtype)


def rms(x, gain, out_dtype, tm=512):
    m, d = x.shape
    return pl.pallas_call(
        _rms_kernel,
        grid=(m // tm,),
        in_specs=[pl.BlockSpec((tm, d), lambda i: (i, 0)),
                  pl.BlockSpec((1, d), lambda i: (0, 0))],
        out_specs=pl.BlockSpec((tm, d), lambda i: (i, 0)),
        out_shape=jax.ShapeDtypeStruct((m, d), out_dtype),
        compiler_params=_params("parallel"),
        name="rms",
    )(x, gain.reshape(1, d))


def _rms_pair_kernel(a_ref, b_ref, ga_ref, gb_ref, o_ref):
    n_heads, nv, _ = a_ref.shape
    wa = n_heads * HEAD_DIM
    lanes = [slice(r * HEAD_DIM, (r + 1) * HEAD_DIM) for r in range(VIEW_RES)]
    inv = []
    for r in range(VIEW_RES):
        sq = a_ref[0, :, lanes[r]] * a_ref[0, :, lanes[r]]
        for hh in range(1, n_heads):
            sq = sq + a_ref[hh, :, lanes[r]] * a_ref[hh, :, lanes[r]]
        inv.append(lax.rsqrt(jnp.sum(sq, axis=-1, keepdims=True) * (1.0 / wa) + EPS))
    for hh in range(n_heads):
        by_res = jnp.stack([a_ref[hh, :, lanes[r]] * inv[r] for r in range(VIEW_RES)], axis=0)
        tokens = jnp.swapaxes(by_res, 0, 1).reshape(nv * VIEW_RES, HEAD_DIM)
        cols = slice(hh * HEAD_DIM, (hh + 1) * HEAD_DIM)
        o_ref[:, cols] = (tokens * ga_ref[:, cols]).astype(o_ref.dtype)
    o_ref[:, wa:] = _rms(b_ref[...], gb_ref[...]).astype(o_ref.dtype)


def rms_pair(a, b, ga, gb, tm=512):
    n_heads, batch, rows, width = a.shape
    assert width == VIEW_RES * HEAD_DIM and (rows * VIEW_RES) % tm == 0
    wa = n_heads * HEAD_DIM
    m, wb = b.shape
    tiles_per_seq = rows * VIEW_RES // tm
    return pl.pallas_call(
        _rms_pair_kernel,
        grid=(m // tm,),
        in_specs=[pl.BlockSpec((n_heads, None, tm // VIEW_RES, width),
                               lambda i: (0, i // tiles_per_seq, i % tiles_per_seq, 0)),
                  pl.BlockSpec((tm, wb), lambda i: (i, 0)),
                  pl.BlockSpec((1, wa), lambda i: (0, 0)),
                  pl.BlockSpec((1, wb), lambda i: (0, 0))],
        out_specs=pl.BlockSpec((tm, wa + wb), lambda i: (i, 0)),
        out_shape=jax.ShapeDtypeStruct((m, wa + wb), BF16),
        compiler_params=_params("parallel"),
        name="rms_pair",
    )(a, b, ga.reshape(1, wa), gb.reshape(1, wb))


def _rope_tables(seq):
    inv = ROPE_THETA ** (-jnp.arange(0, ROPE_DIM, 2, dtype=F32) / ROPE_DIM)
    ang = jnp.arange(seq, dtype=F32)[:, None] * inv[None, :]
    cos, sin = jnp.cos(ang), jnp.sin(ang)
    ones = jnp.ones((seq, HEAD_DIM - ROPE_DIM), F32)
    cos_t = jnp.concatenate([cos, cos, ones], axis=1)
    sin_t = jnp.concatenate([-sin, sin, 0.0 * ones], axis=1)
    return cos_t, sin_t


PROJ_ROWS = 256
VIEW_RES = DILATED_CONFIGS[-1][1]


def _proj_kernel(h_ref, w_ref, cos_ref, sin_ref, o_ref, *, src_tiles, layout):
    jt = pl.program_id(1)
    is_v = functools.reduce(jnp.logical_or, [jt == n for n, t in enumerate(src_tiles) if t in V_TILES], False)
    is_q = functools.reduce(jnp.logical_or, [jt == n for n, t in enumerate(src_tiles) if t in Q_TILES], False)
    heads_per_tile = PROJ_TILE // HEAD_DIM
    tm = h_ref.shape[0]
    scale = jnp.where(is_q, Q_SCALE, 1.0).astype(F32)
    lane = lax.broadcasted_iota(jnp.int32, (1, HEAD_DIM), 1)
    first_half = lane < ROPE_DIM // 2

    for c in range(tm // PROJ_ROWS):
        rows = slice(c * PROJ_ROWS, (c + 1) * PROJ_ROWS)
        y = _dot(h_ref[rows, :], w_ref[...])
        cos_t = jnp.where(is_v, 1.0, cos_ref[rows, :]) * scale
        sin_t = jnp.where(is_v, 0.0, sin_ref[rows, :]) * scale
        for hh in range(heads_per_tile):
            yh = y[:, hh * HEAD_DIM:(hh + 1) * HEAD_DIM]
            partner = jnp.where(first_half,
                                pltpu.roll(yh, HEAD_DIM - ROPE_DIM // 2, axis=1),
                                pltpu.roll(yh, ROPE_DIM // 2, axis=1))
            val = yh * cos_t + partner * sin_t
            if layout == "cols":
                for cc in range(PROJ_ROWS // LANES):
                    col0 = c * PROJ_ROWS + cc * LANES
                    o_ref[hh, :, col0:col0 + LANES] = val[cc * LANES:(cc + 1) * LANES, :].T.astype(o_ref.dtype)
            elif layout == "view":
                nv = PROJ_ROWS // VIEW_RES
                by_res = jnp.swapaxes(val.reshape(nv, VIEW_RES, HEAD_DIM), 0, 1)
                for r in range(VIEW_RES):
                    o_ref[hh, c * nv:(c + 1) * nv, r * HEAD_DIM:(r + 1) * HEAD_DIM] = by_res[r].astype(o_ref.dtype)
            else:
                o_ref[hh, rows, :] = val.astype(o_ref.dtype)


def project(h, w, layer, cos_t, sin_t, src_tiles, out_dtype, batch, seq, layout="rows", tm=2048):
    m, d = h.shape
    n_tiles = len(src_tiles)
    heads_per_tile = PROJ_TILE // HEAD_DIM
    tm = min(tm, seq)
    pos_blocks = seq // tm

    def w_map(i, j):
        col = functools.reduce(lambda acc, nt: jnp.where(j == nt[0], nt[1], acc),
                               list(enumerate(src_tiles)), 0)
        return (layer, 0, col)

    if layout == "cols":
        out_spec = pl.BlockSpec((heads_per_tile, None, HEAD_DIM, tm),
                                lambda i, j: (j, i // pos_blocks, 0, i % pos_blocks))
        out_shape = jax.ShapeDtypeStruct((n_tiles * heads_per_tile, batch, HEAD_DIM, seq), out_dtype)
    elif layout == "view":
        out_spec = pl.BlockSpec((heads_per_tile, None, tm // VIEW_RES, VIEW_RES * HEAD_DIM),
                                lambda i, j: (j, i // pos_blocks, i % pos_blocks, 0))
        out_shape = jax.ShapeDtypeStruct((n_tiles * heads_per_tile, batch, seq // VIEW_RES, VIEW_RES * HEAD_DIM),
                                         out_dtype)
    else:
        out_spec = pl.BlockSpec((heads_per_tile, tm, HEAD_DIM), lambda i, j: (j, i, 0))
        out_shape = jax.ShapeDtypeStruct((n_tiles * heads_per_tile, m, HEAD_DIM), out_dtype)

    return pl.pallas_call(
        functools.partial(_proj_kernel, src_tiles=src_tiles, layout=layout),
        grid=(m // tm, n_tiles),
        in_specs=[pl.BlockSpec((tm, d), lambda i, j: (i, 0)),
                  pl.BlockSpec((None, d, PROJ_TILE), w_map),
                  pl.BlockSpec((tm, HEAD_DIM), lambda i, j: (i % pos_blocks, 0)),
                  pl.BlockSpec((tm, HEAD_DIM), lambda i, j: (i % pos_blocks, 0))],
        out_specs=out_spec,
        out_shape=out_shape,
        compiler_params=_params("parallel", "arbitrary"),
        name="project_" + layout,
    )(h, w, cos_t, sin_t)


def _aligned(x, multiple):
    return x if isinstance(x, int) else pl.multiple_of(x, multiple)


DIL_UNROLL = 8
DIL_RES = VIEW_RES


def _dilated_kernel(q_ref, k_ref, v_ref, o_ref, acc_ref, m_ref, l_ref, bias_ref, *, seq):
    for ci, (window, dil) in enumerate(DILATED_CONFIGS):
        band = window // dil
        n_pieces = DIL_RES // dil
        pr = BLK // n_pieces
        kr = 2 * pr
        tiles_per_res = seq // (BLK * dil)
        assert band <= BLK and pr % 8 == 0 and pr & (pr - 1) == 0

        q_row = lax.broadcasted_iota(jnp.int32, (BLK, 1), 0)
        k_row = lax.broadcasted_iota(jnp.int32, (1, 2 * BLK), 1)
        q_sub = n_pieces * (q_row & (pr - 1)) + (q_row >> (pr.bit_length() - 1))
        k_sub = n_pieces * (k_row & (kr - 1)) + (k_row >> (kr.bit_length() - 1))
        for lead in range(2):
            dist = (q_sub + lead * BLK) - k_sub
            bias_ref[2 * ci + lead] = jnp.where((dist >= 0) & (dist <= band), 0.0, NEG)

        def body(step, carry, ci=ci, dil=dil, band=band, n_pieces=n_pieces, pr=pr, kr=kr,
                 tiles_per_res=tiles_per_res):
            def pieces(ref, row0, nrows, r):
                return [ref[pl.ds(row0, nrows), pl.ds(_aligned((r + dil * a) * HEAD_DIM, HEAD_DIM), HEAD_DIM)]
                        for a in range(n_pieces)]

            tiles = []
            for u in range(DIL_UNROLL):
                idx = step * DIL_UNROLL + u
                r = idx // tiles_per_res
                i = idx - r * tiles_per_res
                kb = max(i - 1, 0) if isinstance(i, int) else jnp.maximum(i - 1, 0)
                q_row0 = _aligned(i * pr, pr)
                k_row0 = _aligned(kb * pr, pr)
                m_old = None if ci == 0 else jnp.concatenate(pieces(m_ref, q_row0, pr, r), axis=0)
                tiles.append((r, i, kb, q_row0, k_row0, m_old))

            def update(r, q_row0, m_old, m_new, l_tile, pv):
                if ci == 0:
                    l_new = l_tile
                    acc_new = pv
                else:
                    alpha = jnp.exp2(m_old - m_new)
                    l_new = alpha * jnp.concatenate(pieces(l_ref, q_row0, pr, r), axis=0) + l_tile
                    acc_new = alpha * jnp.concatenate(pieces(acc_ref, q_row0, pr, r), axis=0) + pv
                for a in range(n_pieces):
                    col = pl.ds(_aligned((r + dil * a) * HEAD_DIM, HEAD_DIM), HEAD_DIM)
                    rows = slice(a * pr, (a + 1) * pr)
                    m_ref[pl.ds(q_row0, pr), col] = m_new[rows]
                    l_ref[pl.ds(q_row0, pr), col] = l_new[rows]
                    acc_ref[pl.ds(q_row0, pr), col] = acc_new[rows]

            scores, maxes, probs, results = {}, {}, {}, []
            for t in range(DIL_UNROLL + 4):
                if t < DIL_UNROLL:
                    (r, i, kb, q_row0, k_row0, m_old) = tiles[t]
                    q = jnp.concatenate(pieces(q_ref, q_row0, pr, r), axis=0).astype(BF16)
                    k = jnp.concatenate(pieces(k_ref, k_row0, kr, r), axis=0).astype(BF16)
                    scores[t] = _dot_nt(q, k) + bias_ref[2 * ci + i - kb]
                if 0 <= t - 1 < DIL_UNROLL:
                    m_tile = jnp.max(scores[t - 1], axis=1, keepdims=True)
                    m_old = tiles[t - 1][5]
                    maxes[t - 1] = jnp.broadcast_to(m_tile, (BLK, LANES)) if ci == 0 else jnp.maximum(m_old, m_tile)
                if 0 <= t - 2 < DIL_UNROLL:
                    m_new = maxes[t - 2]
                    probs[t - 2] = jnp.exp2(scores.pop(t - 2) - jnp.concatenate([m_new, m_new], axis=1))
                if 0 <= t - 3 < DIL_UNROLL:
                    (r, i, kb, q_row0, k_row0, m_old) = tiles[t - 3]
                    p = probs.pop(t - 3)
                    v = jnp.concatenate(pieces(v_ref, k_row0, kr, r), axis=0).astype(BF16)
                    v_sum = jnp.concatenate([v, jnp.ones((2 * BLK, LANES), BF16)], axis=1)
                    pv = _dot(p.astype(BF16), v_sum)
                    results.append((r, q_row0, m_old, maxes.pop(t - 3), pv[:, HEAD_DIM:], pv[:, :HEAD_DIM]))
                if 0 <= t - 4 < DIL_UNROLL:
                    update(*results[t - 4])
            return carry

        for step in range(seq // (BLK * DIL_UNROLL)):
            body(step, 0)
    o_ref[...] = acc_ref[...] / l_ref[...]


def dilated_attention(pf, batch, seq):
    assert all(DIL_RES % d == 0 for _, d in DILATED_CONFIGS)
    assert seq % (BLK * DIL_RES) == 0 and seq // DIL_RES >= 2 * BLK and (seq // BLK) % DIL_UNROLL == 0
    rows = seq // DIL_RES
    width = DIL_RES * HEAD_DIM
    view = pf

    def spec(off):
        return pl.BlockSpec((None, None, rows, width), lambda b, h: (off + h, b, 0, 0))

    out = pl.pallas_call(
        functools.partial(_dilated_kernel, seq=seq),
        grid=(batch, HEADS_A),
        in_specs=[spec(PF_QA), spec(PF_KA), spec(PF_VA)],
        out_specs=spec(0),
        out_shape=jax.ShapeDtypeStruct((HEADS_A, batch, rows, width), F32),
        scratch_shapes=[pltpu.VMEM((rows, width), F32)] * 3
        + [pltpu.VMEM((2 * len(DILATED_CONFIGS), BLK, 2 * BLK), F32)],
        compiler_params=_params("parallel", "parallel"),
        name="dilated_attention",
    )(view, view, view)
    return out


def _compress_kernel(t_ref, pe_ref, w1_ref, w2_ref, o_ref, ot_ref, *, seq):
    n_chunks = seq // CMP_STRIDE
    first = jnp.zeros((n_chunks, HEAD_DIM), F32)
    second = jnp.zeros((n_chunks, HEAD_DIM), F32)
    for i in range(CMP_STRIDE):
        ti = t_ref[:, i * HEAD_DIM:(i + 1) * HEAD_DIM]
        first += _dot((ti + pe_ref[pl.ds(i, 1), :]).astype(BF16), w1_ref[i])
        second += _dot((ti + pe_ref[pl.ds(CMP_STRIDE + i, 1), :]).astype(BF16), w1_ref[CMP_STRIDE + i])
    pre = first + pltpu.roll(second, n_chunks - 1, axis=0)
    out = _dot(jax.nn.gelu(pre).astype(BF16), w2_ref[...])
    o_ref[...] = out.astype(o_ref.dtype)
    ot_ref[...] = out.T.astype(ot_ref.dtype)


def compress(pf, pe, w1, w2, batch, seq):
    assert CMP_LEN == 2 * CMP_STRIDE and CMP_STRIDE == VIEW_RES
    n_chunks = seq // CMP_STRIDE
    pf4 = pf
    bg = batch * KV_GROUPS_B
    return pl.pallas_call(
        functools.partial(_compress_kernel, seq=seq),
        grid=(2, batch, KV_GROUPS_B),
        in_specs=[pl.BlockSpec((None, None, n_chunks, VIEW_RES * HEAD_DIM),
                               lambda kv, b, g: (PF_KC + kv * KV_GROUPS_B + g, b, 0, 0)),
                  pl.BlockSpec((None, CMP_LEN, HEAD_DIM), lambda kv, b, g: (kv, 0, 0)),
                  pl.BlockSpec((None, CMP_LEN, HEAD_DIM, HEAD_DIM), lambda kv, b, g: (kv, 0, 0, 0)),
                  pl.BlockSpec((None, HEAD_DIM, HEAD_DIM), lambda kv, b, g: (kv, 0, 0))],
        out_specs=[pl.BlockSpec((None, None, n_chunks, HEAD_DIM), lambda kv, b, g: (kv, b * KV_GROUPS_B + g, 0, 0)),
                   pl.BlockSpec((None, None, HEAD_DIM, n_chunks), lambda kv, b, g: (kv, b * KV_GROUPS_B + g, 0, 0))],
        out_shape=[jax.ShapeDtypeStruct((2, bg, n_chunks, HEAD_DIM), BF16),
                   jax.ShapeDtypeStruct((2, bg, HEAD_DIM, n_chunks), BF16)],
        compiler_params=_params("parallel", "parallel", "parallel"),
        name="compress",
    )(pf4, pe, w1, w2)


NSA_TQ = 512
NSA_TK = 512


def _select_blocks(score):
    n_s, tq = score.shape
    groups = n_s // 8
    rows8 = [score[8 * v:8 * v + 8, :] for v in range(groups)]
    rank8 = [jnp.zeros((8, tq), F32) for _ in range(groups)]
    sub = lax.broadcasted_iota(jnp.int32, (8, 1), 0)
    for jp in range(n_s):
        vp, sp = divmod(jp, 8)
        row = jnp.broadcast_to(rows8[vp][sp:sp + 1, :], (8, tq))
        for v in range(groups):
            if v > vp:
                beats = jnp.where(row >= rows8[v], 1.0, 0.0)
            elif v < vp:
                beats = jnp.where(row > rows8[v], 1.0, 0.0)
            else:
                beats = jnp.where(sub > sp, jnp.where(row >= rows8[v], 1.0, 0.0),
                                  jnp.where(row > rows8[v], 1.0, 0.0))
            rank8[v] = rank8[v] + beats
    rank = jnp.concatenate(rank8, axis=0)
    return (rank < N_SELECT) & (score > -jnp.inf)


SUM_ROWS = 16


def _with_sum_rows(vt):
    return jnp.concatenate([vt, jnp.ones((SUM_ROWS, vt.shape[1]), vt.dtype)], axis=0)


def _flash_block(s, vt_sum, carry):
    m_i, acc = carry
    m_new = jnp.maximum(m_i, jnp.max(s, axis=0, keepdims=True))
    alpha = jnp.exp2(m_i - m_new)
    p = jnp.exp2(s - m_new)
    return m_new, alpha * acc + _dot(vt_sum, p.astype(BF16))


def _normalized(acc):
    return acc[:HEAD_DIM] * (1.0 / acc[HEAD_DIM:HEAD_DIM + 1])


def _nsa_kernel(q_ref, kc_ref, vct_ref, ks_ref, vst_ref, kw_ref, vwt_ref, e_ref, wb_ref, gl_ref, o_ref, gates_ref,
                *, seq):
    tq, tk = NSA_TQ, NSA_TK
    cols = REP_B * tq
    n_cp = seq // CMP_STRIDE
    n_s = seq // SLC_BLOCK
    qi = pl.program_id(2)
    t0 = qi * tq
    q_t = jnp.concatenate([q_ref[r] for r in range(REP_B)], axis=1)
    lane = lax.broadcasted_iota(jnp.int32, (1, tq), 1)
    tpos = t0 + lane

    def per_head(x):
        return jnp.concatenate([x] * REP_B, axis=1)

    csub = lax.broadcasted_iota(jnp.int32, (n_cp, 1), 0)
    c_ok = (csub * CMP_STRIDE + (CMP_LEN - 1) <= tpos) & (csub < n_cp - 1)
    sc = _dot(kc_ref[...], q_t) + per_head(jnp.where(c_ok, 0.0, NEG))
    e = jnp.exp2(sc - jnp.max(sc, axis=0, keepdims=True))
    has_block = per_head(jnp.where(tpos >= CMP_LEN - 1, 1.0, 0.0))
    p_cmp = e * (has_block / jnp.maximum(jnp.sum(e, axis=0, keepdims=True), 1e-30))
    o_cmp = _dot(vct_ref[...], p_cmp.astype(BF16))

    ws = WIN + tq
    w0 = pl.multiple_of(jnp.maximum(t0 - WIN, 0), tq)
    s_w = _dot(kw_ref[pl.ds(w0, ws), :], q_t) + per_head(wb_ref[...])
    p_w = jnp.exp2(s_w - jnp.max(s_w, axis=0, keepdims=True))
    o_win = _normalized(_dot(_with_sum_rows(vwt_ref[:, pl.ds(w0, ws)]), p_w.astype(BF16)))

    p_sum = p_cmp[:, 0:tq]
    for r in range(1, REP_B):
        p_sum = p_sum + p_cmp[:, r * tq:(r + 1) * tq]
    jj = lax.broadcasted_iota(jnp.int32, (n_s, 1), 0)
    cidx = lax.broadcasted_iota(jnp.int32, (1, n_cp), 1)
    ratio = SLC_BLOCK // CMP_STRIDE
    c_first = ratio * jj - (CMP_LEN // CMP_STRIDE - 1)
    hits = jnp.where((cidx >= c_first) & (cidx < ratio * (jj + 1)), 1.0, 0.0).astype(BF16)
    p_hi = p_sum.astype(BF16)
    p_lo = (p_sum - p_hi.astype(F32)).astype(BF16)
    imp = _dot(hits, p_hi) + _dot(hits, p_lo)
    qblk = tpos >> (SLC_BLOCK.bit_length() - 1)
    forced = (jj == 0) | (jj == qblk) | (jj == qblk - 1)
    valid = jj * SLC_BLOCK <= tpos
    score = jnp.where(forced, FORCE_SCORE, jnp.where(valid, imp, -jnp.inf))
    sel = _select_blocks(score)
    bias = jnp.where(sel, 0.0, SEL_BIAS)
    if n_s < LANES:
        bias = jnp.concatenate([bias, jnp.full((LANES - n_s, tq), SEL_BIAS, F32)], axis=0)
    q_aug = jnp.concatenate([q_t, per_head(bias.astype(BF16))], axis=0)

    def slc_scores(kt):
        k0 = pl.multiple_of(kt * tk, tk)
        k_aug = jnp.concatenate([ks_ref[pl.ds(k0, tk), :], e_ref[pl.ds(k0, tk), :]], axis=1)
        return _dot(k_aug, q_aug)

    def slc_values(kt):
        return _with_sum_rows(vst_ref[:, pl.ds(pl.multiple_of(kt * tk, tk), tk)])

    def slc_step(kt, stats):
        return _flash_block(slc_scores(kt), slc_values(kt), stats)

    last = (t0 + tq - 1) // tk
    stats = (jnp.full((1, cols), NEG, F32), jnp.zeros((HEAD_DIM + SUM_ROWS, cols), F32))
    stats = lax.fori_loop(0, last // 2, lambda pi, st: slc_step(2 * pi + 1, slc_step(2 * pi, st)), stats)
    stats = lax.cond(last % 2 == 1, lambda st: slc_step(last - 1, st), lambda st: st, stats)
    tok = last * tk + lax.broadcasted_iota(jnp.int32, (tk, 1), 0)
    s_last = slc_scores(last) + per_head(jnp.where(tok <= tpos, 0.0, NEG))
    o_slc = _normalized(_flash_block(s_last, slc_values(last), stats)[1])

    gates_ref[...] = jax.nn.sigmoid(gl_ref[...].T)
    first = pl.program_id(1) * (REP_B * 3)

    def gate(k):
        return gates_ref[pl.ds(first + k, 1), :]

    for r in range(REP_B):
        sl = slice(r * tq, (r + 1) * tq)
        merged = (gate(3 * r) * o_cmp[:, sl] + gate(3 * r + 1) * o_slc[:, sl]
                  + gate(3 * r + 2) * o_win[:, sl])
        o_ref[:, r * HEAD_DIM:(r + 1) * HEAD_DIM] = merged.T


def nsa_attention(pr, pc, kvc, kvct, gl, batch, seq):
    tq = NSA_TQ
    n_s = seq // SLC_BLOCK
    assert seq % NSA_TK == 0 and NSA_TK % tq == 0 and seq >= WIN + tq and n_s <= LANES and n_s % 8 == 0
    assert WIN % tq == 0
    assert SLC_BLOCK & (SLC_BLOCK - 1) == 0
    n_cp = seq // CMP_STRIDE
    nq = seq // tq
    pr4 = pr.reshape(pr.shape[0], batch, seq, HEAD_DIM)
    block_onehot = (jnp.arange(seq)[:, None] // SLC_BLOCK == jnp.arange(LANES)[None, :]).astype(BF16)
    lead = jnp.arange(WIN // tq + 1)[:, None, None] * tq
    dist = lead + jnp.arange(tq)[None, None, :] - jnp.arange(WIN + tq)[None, :, None]
    win_bias = jnp.where((dist >= 0) & (dist <= WIN - 1), 0.0, NEG).astype(F32)

    def k_spec(off):
        return pl.BlockSpec((None, None, seq, HEAD_DIM), lambda b, g, i: (off + g, b, 0, 0))

    def vt_spec(off):
        return pl.BlockSpec((None, None, HEAD_DIM, seq), lambda b, g, i: (off + g, b, 0, 0))

    return pl.pallas_call(
        functools.partial(_nsa_kernel, seq=seq),
        grid=(batch, KV_GROUPS_B, nq),
        in_specs=[pl.BlockSpec((REP_B, None, HEAD_DIM, tq), lambda b, g, i: (PC_QB // REP_B + g, b, 0, i)),
                  pl.BlockSpec((None, None, n_cp, HEAD_DIM), lambda b, g, i: (0, b * KV_GROUPS_B + g, 0, 0)),
                  pl.BlockSpec((None, None, HEAD_DIM, n_cp), lambda b, g, i: (1, b * KV_GROUPS_B + g, 0, 0)),
                  k_spec(PR_KS), vt_spec(PC_VS), k_spec(PR_KW), vt_spec(PC_VW),
                  pl.BlockSpec((seq, LANES), lambda b, g, i: (0, 0)),
                  pl.BlockSpec((None, WIN + tq, tq), lambda b, g, i: (jnp.minimum(i, WIN // tq), 0, 0)),
                  pl.BlockSpec((tq, LANES), lambda b, g, i: (b * nq + i, 0))],
        out_specs=pl.BlockSpec((tq, REP_B * HEAD_DIM), lambda b, g, i: (b * nq + i, g)),
        out_shape=jax.ShapeDtypeStruct((batch * seq, WIDTH_B), F32),
        scratch_shapes=[pltpu.VMEM((LANES, tq), F32)],
        compiler_params=_params("parallel", "parallel", "arbitrary"),
        name="nsa_attention",
    )(pc, kvc, kvct, pr4, pc, pr4, pc, block_onehot, win_bias, gl)


def _mm_res_kernel(a_ref, w_ref, r_ref, o_ref):
    o_ref[...] = _dot(a_ref[...], w_ref[...]) + r_ref[...]


def matmul_residual(a, w, res, tm, tn, weights_resident=False):
    m, k = a.shape
    n = w.shape[1]
    if weights_resident:
        grid = (n // tn, m // tm)
        row, col = (lambda j, i: i), (lambda j, i: j)
    else:
        grid = (m // tm, n // tn)
        row, col = (lambda i, j: i), (lambda i, j: j)
    return pl.pallas_call(
        _mm_res_kernel,
        grid=grid,
        in_specs=[pl.BlockSpec((tm, k), lambda p, q: (row(p, q), 0)),
                  pl.BlockSpec((k, tn), lambda p, q: (0, col(p, q))),
                  pl.BlockSpec((tm, tn), lambda p, q: (row(p, q), col(p, q)))],
        out_specs=pl.BlockSpec((tm, tn), lambda p, q: (row(p, q), col(p, q))),
        out_shape=jax.ShapeDtypeStruct((m, n), F32),
        compiler_params=_params("parallel", "arbitrary"),
        name="matmul_residual",
    )(a, w, res)


def _ffn_up_kernel(h_ref, wg_ref, wu_ref, o_ref):
    h = h_ref[...]
    g = _dot(h, wg_ref[...].astype(BF16))
    u = _dot(h, wu_ref[...].astype(BF16))
    o_ref[...] = (g * jax.nn.sigmoid(g) * u).astype(o_ref.dtype)


def ffn_up(h, wg, wu, layer, tm, tn):
    m, k = h.shape
    n = wg.shape[2]
    return pl.pallas_call(
        _ffn_up_kernel,
        grid=(m // tm, n // tn),
        in_specs=[pl.BlockSpec((tm, k), lambda i, j: (i, 0)),
                  pl.BlockSpec((None, k, tn), lambda i, j: (layer, 0, j)),
                  pl.BlockSpec((None, k, tn), lambda i, j: (layer, 0, j))],
        out_specs=pl.BlockSpec((tm, tn), lambda i, j: (i, j)),
        out_shape=jax.ShapeDtypeStruct((m, n), BF16),
        compiler_params=_params("parallel", "arbitrary"),
        name="ffn_up",
    )(h, wg, wu)


def _cast_kernel(w_ref, o_ref):
    o_ref[...] = w_ref[...].astype(o_ref.dtype)


def cast_columns(w, layer, n_cols, tk=512, tn=2816):
    _, k, _ = w.shape
    return pl.pallas_call(
        _cast_kernel,
        grid=(k // tk, n_cols // tn),
        in_specs=[pl.BlockSpec((None, tk, tn), lambda i, j: (layer, i, j))],
        out_specs=pl.BlockSpec((tk, tn), lambda i, j: (i, j)),
        out_shape=jax.ShapeDtypeStruct((k, n_cols), BF16),
        compiler_params=_params("parallel", "parallel"),
        name="cast_columns",
    )(w)


def _layer(x, norm_attn, w_in, layer, ck_pe, ck_w1, ck_w2, cv_pe, cv_w1, cv_w2,
           out_norm_a, out_norm_b, w_out, norm_ffn, w_gate, w_up, w_down, cos_t, sin_t, batch, seq):
    w_main = w_in.astype(BF16)
    w_gl = lax.slice(w_in, (layer, 0, D_MAIN), (layer + 1, D_MODEL, D_MAIN + N_GATES))
    w_gl = jnp.pad(w_gl.reshape(D_MODEL, N_GATES), ((0, 0), (0, LANES - N_GATES))).astype(BF16)
    pe = jnp.stack([ck_pe, cv_pe])
    w1 = jnp.stack([ck_w1, cv_w1]).reshape(2, CMP_LEN, HEAD_DIM, HEAD_DIM).astype(BF16)
    w2 = jnp.stack([ck_w2, cv_w2]).astype(BF16)

    h, gl = rms_gate(x, norm_attn, w_gl)
    pf = project(h, w_main, layer, cos_t, sin_t, F32_TILES, F32, batch, seq, layout="view")
    pr = project(h, w_main, layer, cos_t, sin_t, ROW_TILES, BF16, batch, seq)
    pc = project(h, w_main, layer, cos_t, sin_t, COL_TILES, BF16, batch, seq, layout="cols")
    o_a = dilated_attention(pf, batch, seq)
    kvc, kvct = compress(pf, pe, w1, w2, batch, seq)
    o_b = nsa_attention(pr, pc, kvc, kvct, gl, batch, seq)
    mixed = rms_pair(o_a, o_b, out_norm_a, out_norm_b)
    x1 = matmul_residual(mixed, w_out.astype(BF16), x, tm=1024, tn=1024)
    h2 = rms(x1, norm_ffn, BF16)
    act = ffn_up(h2, w_gate, w_up, layer, tm=2048, tn=256)
    x2 = matmul_residual(act, w_down.astype(BF16), x1, tm=512, tn=512, weights_resident=True)
    return x2


def kernel(x, norm_attn, w_in, ck_pe, ck_w1, ck_w2, cv_pe, cv_w1, cv_w2, out_norm_a, out_norm_b, w_out,
           norm_ffn, w_gate, w_up, w_down, norm_final):
    batch, seq, d = x.shape
    depth = w_in.shape[0]
    cos_t, sin_t = _rope_tables(seq)
    xf = x.reshape(batch * seq, d)
    for l in range(depth):
        xf = _layer(xf, norm_attn[l], w_in, l, ck_pe[l], ck_w1[l], ck_w2[l], cv_pe[l], cv_w1[l], cv_w2[l],
                    out_norm_a[l], out_norm_b[l], w_out[l], norm_ffn[l], w_gate, w_up, w_down[l],
                    cos_t, sin_t, batch, seq)
    return rms(xf, norm_final, F32).reshape(batch, seq, d)
```

```python
import functools
import math

import jax
import jax.numpy as jnp
from jax import lax
from jax.experimental import pallas as pl
from jax.experimental.pallas import tpu as pltpu

F32 = jnp.float32
BF16 = jnp.bfloat16

D_MODEL = 4096
HEAD_DIM = 128
HEADS_A = 16
HEADS_B = 16
KV_GROUPS_B = 4
REP_B = HEADS_B // KV_GROUPS_B
WIDTH_A = HEADS_A * HEAD_DIM
WIDTH_B = HEADS_B * HEAD_DIM
DILATED_CONFIGS = ((128, 1), (512, 4), (2048, 16))
BLK = 128
CMP_LEN = 32
CMP_STRIDE = 16
SLC_BLOCK = 64
N_SELECT = 16
WIN = 512
FORCE_SCORE = 1e6
ROPE_THETA = 500000.0
ROPE_DIM = HEAD_DIM // 4
EPS = 1e-5
Q_SCALE = HEAD_DIM ** -0.5 * math.log2(math.e)
N_GATES = 3 * HEADS_B
D_MAIN = 3 * WIDTH_A + WIDTH_B + 6 * KV_GROUPS_B * HEAD_DIM

VMEM_LIMIT_BYTES = 56 * 1024 * 1024
LANES = 128
NEG = -1e30
SEL_BIAS = -32768.0

PROJ_TILE = 4 * HEAD_DIM
F32_TILES = tuple(range(12)) + (16, 17)
ROW_TILES = (18, 20)
COL_TILES = (12, 13, 14, 15, 19, 21)
Q_TILES = (0, 1, 2, 3, 12, 13, 14, 15)
V_TILES = (8, 9, 10, 11, 17, 19, 21)
PF_QA, PF_KA, PF_VA, PF_KC, PF_VC = 0, 16, 32, 48, 52
PR_KS, PR_KW = 0, 4
PC_QB, PC_VS, PC_VW = 0, 16, 20


def _params(*sem):
    return pltpu.CompilerParams(dimension_semantics=sem, vmem_limit_bytes=VMEM_LIMIT_BYTES)


def _dot(a, b):
    return jnp.dot(a, b, preferred_element_type=F32)


def _dot_nt(a, b):
    return lax.dot_general(a, b, (((1,), (1,)), ((), ())), preferred_element_type=F32)


def _rms(x, g):
    return x * lax.rsqrt(jnp.mean(x * x, axis=-1, keepdims=True) + EPS) * g


def _rms_gate_kernel(x_ref, g_ref, wgl_ref, h_ref, gl_ref):
    hb = _rms(x_ref[...], g_ref[...]).astype(BF16)
    h_ref[...] = hb
    gl_ref[...] = _dot(hb, wgl_ref[...])


def rms_gate(x, gain, w_gl, tm=512):
    m, d = x.shape
    ng = w_gl.shape[1]
    return pl.pallas_call(
        _rms_gate_kernel,
        grid=(m // tm,),
        in_specs=[pl.BlockSpec((tm, d), lambda i: (i, 0)),
                  pl.BlockSpec((1, d), lambda i: (0, 0)),
                  pl.BlockSpec((d, ng), lambda i: (0, 0))],
        out_specs=[pl.BlockSpec((tm, d), lambda i: (i, 0)),
                   pl.BlockSpec((tm, ng), lambda i: (i, 0))],
        out_shape=[jax.ShapeDtypeStruct((m, d), BF16), jax.ShapeDtypeStruct((m, ng), F32)],
        compiler_params=_params("parallel"),
        name="rms_gate",
    )(x, gain.reshape(1, d), w_gl)


def _rms_kernel(x_ref, g_ref, o_ref):
    o_ref[...] = _rms(x_ref[...], g_ref[...]).astype(o_ref.dtype)


def rms(x, gain, out_dtype, tm=512):
    m, d = x.shape
    return pl.pallas_call(
        _rms_kernel,
        grid=(m // tm,),
        in_specs=[pl.BlockSpec((tm, d), lambda i: (i, 0)),
                  pl.BlockSpec((1, d), lambda i: (0, 0))],
        out_specs=pl.BlockSpec((tm, d), lambda i: (i, 0)),
        out_shape=jax.ShapeDtypeStruct((m, d), out_dtype),
        compiler_params=_params("parallel"),
        name="rms",
    )(x, gain.reshape(1, d))


def _rms_pair_kernel(a_ref, b_ref, ga_ref, gb_ref, o_ref):
    n_heads, nv, _ = a_ref.shape
    wa = n_heads * HEAD_DIM
    lanes = [slice(r * HEAD_DIM, (r + 1) * HEAD_DIM) for r in range(VIEW_RES)]
    inv = []
    for r in range(VIEW_RES):
        sq = a_ref[0, :, lanes[r]] * a_ref[0, :, lanes[r]]
        for hh in range(1, n_heads):
            sq = sq + a_ref[hh, :, lanes[r]] * a_ref[hh, :, lanes[r]]
        inv.append(lax.rsqrt(jnp.sum(sq, axis=-1, keepdims=True) * (1.0 / wa) + EPS))
    for hh in range(n_heads):
        by_res = jnp.stack([a_ref[hh, :, lanes[r]] * inv[r] for r in range(VIEW_RES)], axis=0)
        tokens = jnp.swapaxes(by_res, 0, 1).reshape(nv * VIEW_RES, HEAD_DIM)
        cols = slice(hh * HEAD_DIM, (hh + 1) * HEAD_DIM)
        o_ref[:, cols] = (tokens * ga_ref[:, cols]).astype(o_ref.dtype)
    o_ref[:, wa:] = _rms(b_ref[...], gb_ref[...]).astype(o_ref.dtype)


def rms_pair(a, b, ga, gb, tm=512):
    n_heads, batch, rows, width = a.shape
    assert width == VIEW_RES * HEAD_DIM and (rows * VIEW_RES) % tm == 0
    wa = n_heads * HEAD_DIM
    m, wb = b.shape
    tiles_per_seq = rows * VIEW_RES // tm
    return pl.pallas_call(
        _rms_pair_kernel,
        grid=(m // tm,),
        in_specs=[pl.BlockSpec((n_heads, None, tm // VIEW_RES, width),
                               lambda i: (0, i // tiles_per_seq, i % tiles_per_seq, 0)),
                  pl.BlockSpec((tm, wb), lambda i: (i, 0)),
                  pl.BlockSpec((1, wa), lambda i: (0, 0)),
                  pl.BlockSpec((1, wb), lambda i: (0, 0))],
        out_specs=pl.BlockSpec((tm, wa + wb), lambda i: (i, 0)),
        out_shape=jax.ShapeDtypeStruct((m, wa + wb), BF16),
        compiler_params=_params("parallel"),
        name="rms_pair",
    )(a, b, ga.reshape(1, wa), gb.reshape(1, wb))


def _rope_tables(seq):
    inv = ROPE_THETA ** (-jnp.arange(0, ROPE_DIM, 2, dtype=F32) / ROPE_DIM)
    ang = jnp.arange(seq, dtype=F32)[:, None] * inv[None, :]
    cos, sin = jnp.cos(ang), jnp.sin(ang)
    ones = jnp.ones((seq, HEAD_DIM - ROPE_DIM), F32)
    cos_t = jnp.concatenate([cos, cos, ones], axis=1)
    sin_t = jnp.concatenate([-sin, sin, 0.0 * ones], axis=1)
    return cos_t, sin_t


PROJ_ROWS = 256
VIEW_RES = DILATED_CONFIGS[-1][1]


def _proj_kernel(h_ref, w_ref, cos_ref, sin_ref, o_ref, *, src_tiles, layout):
    jt = pl.program_id(1)
    is_v = functools.reduce(jnp.logical_or, [jt == n for n, t in enumerate(src_tiles) if t in V_TILES], False)
    is_q = functools.reduce(jnp.logical_or, [jt == n for n, t in enumerate(src_tiles) if t in Q_TILES], False)
    heads_per_tile = PROJ_TILE // HEAD_DIM
    tm = h_ref.shape[0]
    scale = jnp.where(is_q, Q_SCALE, 1.0).astype(F32)
    lane = lax.broadcasted_iota(jnp.int32, (1, HEAD_DIM), 1)
    first_half = lane < ROPE_DIM // 2

    for c in range(tm // PROJ_ROWS):
        rows = slice(c * PROJ_ROWS, (c + 1) * PROJ_ROWS)
        y = _dot(h_ref[rows, :], w_ref[...])
        cos_t = jnp.where(is_v, 1.0, cos_ref[rows, :]) * scale
        sin_t = jnp.where(is_v, 0.0, sin_ref[rows, :]) * scale
        for hh in range(heads_per_tile):
            yh = y[:, hh * HEAD_DIM:(hh + 1) * HEAD_DIM]
            partner = jnp.where(first_half,
                                pltpu.roll(yh, HEAD_DIM - ROPE_DIM // 2, axis=1),
                                pltpu.roll(yh, ROPE_DIM // 2, axis=1))
            val = yh * cos_t + partner * sin_t
            if layout == "cols":
                for cc in range(PROJ_ROWS // LANES):
                    col0 = c * PROJ_ROWS + cc * LANES
                    o_ref[hh, :, col0:col0 + LANES] = val[cc * LANES:(cc + 1) * LANES, :].T.astype(o_ref.dtype)
            elif layout == "view":
                nv = PROJ_ROWS // VIEW_RES
                by_res = jnp.swapaxes(val.reshape(nv, VIEW_RES, HEAD_DIM), 0, 1)
                for r in range(VIEW_RES):
                    o_ref[hh, c * nv:(c + 1) * nv, r * HEAD_DIM:(r + 1) * HEAD_DIM] = by_res[r].astype(o_ref.dtype)
            else:
                o_ref[hh, rows, :] = val.astype(o_ref.dtype)


def project(h, w, layer, cos_t, sin_t, src_tiles, out_dtype, batch, seq, layout="rows", tm=2048):
    m, d = h.shape
    n_tiles = len(src_tiles)
    heads_per_tile = PROJ_TILE // HEAD_DIM
    tm = min(tm, seq)
    pos_blocks = seq // tm

    def w_map(i, j):
        col = functools.reduce(lambda acc, nt: jnp.where(j == nt[0], nt[1], acc),
                               list(enumerate(src_tiles)), 0)
        return (layer, 0, col)

    if layout == "cols":
        out_spec = pl.BlockSpec((heads_per_tile, None, HEAD_DIM, tm),
                                lambda i, j: (j, i // pos_blocks, 0, i % pos_blocks))
        out_shape = jax.ShapeDtypeStruct((n_tiles * heads_per_tile, batch, HEAD_DIM, seq), out_dtype)
    elif layout == "view":
        out_spec = pl.BlockSpec((heads_per_tile, None, tm // VIEW_RES, VIEW_RES * HEAD_DIM),
                                lambda i, j: (j, i // pos_blocks, i % pos_blocks, 0))
        out_shape = jax.ShapeDtypeStruct((n_tiles * heads_per_tile, batch, seq // VIEW_RES, VIEW_RES * HEAD_DIM),
                                         out_dtype)
    else:
        out_spec = pl.BlockSpec((heads_per_tile, tm, HEAD_DIM), lambda i, j: (j, i, 0))
        out_shape = jax.ShapeDtypeStruct((n_tiles * heads_per_tile, m, HEAD_DIM), out_dtype)

    return pl.pallas_call(
        functools.partial(_proj_kernel, src_tiles=src_tiles, layout=layout),
        grid=(m // tm, n_tiles),
        in_specs=[pl.BlockSpec((tm, d), lambda i, j: (i, 0)),
                  pl.BlockSpec((None, d, PROJ_TILE), w_map),
                  pl.BlockSpec((tm, HEAD_DIM), lambda i, j: (i % pos_blocks, 0)),
                  pl.BlockSpec((tm, HEAD_DIM), lambda i, j: (i % pos_blocks, 0))],
        out_specs=out_spec,
        out_shape=out_shape,
        compiler_params=_params("parallel", "arbitrary"),
        name="project_" + layout,
    )(h, w, cos_t, sin_t)


def _aligned(x, multiple):
    return x if isinstance(x, int) else pl.multiple_of(x, multiple)


DIL_UNROLL = 32
DIL_RES = VIEW_RES


def _dilated_kernel(q_ref, k_ref, v_ref, o_ref, acc_ref, m_ref, l_ref, bias_ref, *, seq):
    for ci, (window, dil) in enumerate(DILATED_CONFIGS):
        band = window // dil
        n_pieces = DIL_RES // dil
        pr = BLK // n_pieces
        kr = 2 * pr
        tiles_per_res = seq // (BLK * dil)
        assert band <= BLK and pr % 8 == 0 and pr & (pr - 1) == 0

        q_row = lax.broadcasted_iota(jnp.int32, (BLK, 1), 0)
        k_row = lax.broadcasted_iota(jnp.int32, (1, 2 * BLK), 1)
        q_sub = n_pieces * (q_row & (pr - 1)) + (q_row >> (pr.bit_length() - 1))
        k_sub = n_pieces * (k_row & (kr - 1)) + (k_row >> (kr.bit_length() - 1))
        for lead in range(2):
            dist = (q_sub + lead * BLK) - k_sub
            bias_ref[2 * ci + lead] = jnp.where((dist >= 0) & (dist <= band), 0.0, NEG)

        def body(step, carry, ci=ci, dil=dil, band=band, n_pieces=n_pieces, pr=pr, kr=kr,
                 tiles_per_res=tiles_per_res):
            def pieces(ref, row0, nrows, r):
                return [ref[pl.ds(row0, nrows), pl.ds(_aligned((r + dil * a) * HEAD_DIM, HEAD_DIM), HEAD_DIM)]
                        for a in range(n_pieces)]

            tiles = []
            for u in range(DIL_UNROLL):
                idx = step * DIL_UNROLL + u
                r = idx // tiles_per_res
                i = idx - r * tiles_per_res
                kb = max(i - 1, 0) if isinstance(i, int) else jnp.maximum(i - 1, 0)
                q_row0 = _aligned(i * pr, pr)
                k_row0 = _aligned(kb * pr, pr)
                tiles.append((r, i, kb, q_row0, k_row0))

            def update(r, q_row0, m_old, m_new, l_tile, pv):
                if ci == 0:
                    l_new = l_tile
                    acc_new = pv
                else:
                    alpha = jnp.exp2(m_old - m_new)
                    l_new = alpha * jnp.concatenate(pieces(l_ref, q_row0, pr, r), axis=0) + l_tile
                    acc_new = alpha * jnp.concatenate(pieces(acc_ref, q_row0, pr, r), axis=0) + pv
                for a in range(n_pieces):
                    col = pl.ds(_aligned((r + dil * a) * HEAD_DIM, HEAD_DIM), HEAD_DIM)
                    rows = slice(a * pr, (a + 1) * pr)
                    m_ref[pl.ds(q_row0, pr), col] = m_new[rows]
                    l_ref[pl.ds(q_row0, pr), col] = l_new[rows]
                    acc_ref[pl.ds(q_row0, pr), col] = acc_new[rows]

            scores, m_olds, maxes, probs, results = {}, {}, {}, {}, []
            for t in range(DIL_UNROLL + 4):
                if t < DIL_UNROLL:
                    (r, i, kb, q_row0, k_row0) = tiles[t]
                    q = jnp.concatenate(pieces(q_ref, q_row0, pr, r), axis=0).astype(BF16)
                    k = jnp.concatenate(pieces(k_ref, k_row0, kr, r), axis=0).astype(BF16)
                    scores[t] = _dot_nt(q, k) + bias_ref[2 * ci + i - kb]
                if 0 <= t - 1 < DIL_UNROLL:
                    m_tile = jnp.max(scores[t - 1], axis=1, keepdims=True)
                    (r, i, kb, q_row0, k_row0) = tiles[t - 1]
                    m_old = None if ci == 0 else jnp.concatenate(pieces(m_ref, q_row0, pr, r), axis=0)
                    m_olds[t - 1] = m_old
                    maxes[t - 1] = jnp.broadcast_to(m_tile, (BLK, LANES)) if ci == 0 else jnp.maximum(m_old, m_tile)
                if 0 <= t - 2 < DIL_UNROLL:
                    m_new = maxes[t - 2]
                    probs[t - 2] = jnp.exp2(scores.pop(t - 2) - jnp.concatenate([m_new, m_new], axis=1))
                if 0 <= t - 3 < DIL_UNROLL:
                    (r, i, kb, q_row0, k_row0) = tiles[t - 3]
                    m_old = m_olds.pop(t - 3)
                    p = probs.pop(t - 3)
                    v = jnp.concatenate(pieces(v_ref, k_row0, kr, r), axis=0).astype(BF16)
                    v_sum = jnp.concatenate([v, jnp.ones((2 * BLK, LANES), BF16)], axis=1)
                    pv = _dot(p.astype(BF16), v_sum)
                    results.append((r, q_row0, m_old, maxes.pop(t - 3), pv[:, HEAD_DIM:], pv[:, :HEAD_DIM]))
                if 0 <= t - 4 < DIL_UNROLL:
                    update(*results[t - 4])
            return carry

        for step in range(seq // (BLK * DIL_UNROLL)):
            body(step, 0)
    o_ref[...] = acc_ref[...] / l_ref[...]


def dilated_attention(pf, batch, seq):
    assert all(DIL_RES % d == 0 for _, d in DILATED_CONFIGS)
    assert seq % (BLK * DIL_RES) == 0 and seq // DIL_RES >= 2 * BLK and (seq // BLK) % DIL_UNROLL == 0
    rows = seq // DIL_RES
    width = DIL_RES * HEAD_DIM
    view = pf

    def spec(off):
        return pl.BlockSpec((None, None, rows, width), lambda b, h: (off + h, b, 0, 0))

    out = pl.pallas_call(
        functools.partial(_dilated_kernel, seq=seq),
        grid=(batch, HEADS_A),
        in_specs=[spec(PF_QA), spec(PF_KA), spec(PF_VA)],
        out_specs=spec(0),
        out_shape=jax.ShapeDtypeStruct((HEADS_A, batch, rows, width), F32),
        scratch_shapes=[pltpu.VMEM((rows, width), F32)] * 3
        + [pltpu.VMEM((2 * len(DILATED_CONFIGS), BLK, 2 * BLK), F32)],
        compiler_params=_params("parallel", "parallel"),
        name="dilated_attention",
    )(view, view, view)
    return out


def _compress_kernel(t_ref, pe_ref, w1_ref, w2_ref, o_ref, ot_ref, *, seq):
    n_chunks = seq // CMP_STRIDE
    first = jnp.zeros((n_chunks, HEAD_DIM), F32)
    second = jnp.zeros((n_chunks, HEAD_DIM), F32)
    for i in range(CMP_STRIDE):
        ti = t_ref[:, i * HEAD_DIM:(i + 1) * HEAD_DIM]
        first += _dot((ti + pe_ref[pl.ds(i, 1), :]).astype(BF16), w1_ref[i])
        second += _dot((ti + pe_ref[pl.ds(CMP_STRIDE + i, 1), :]).astype(BF16), w1_ref[CMP_STRIDE + i])
    pre = first + pltpu.roll(second, n_chunks - 1, axis=0)
    out = _dot(jax.nn.gelu(pre).astype(BF16), w2_ref[...])
    o_ref[...] = out.astype(o_ref.dtype)
    ot_ref[...] = out.T.astype(ot_ref.dtype)


def compress(pf, pe, w1, w2, batch, seq):
    assert CMP_LEN == 2 * CMP_STRIDE and CMP_STRIDE == VIEW_RES
    n_chunks = seq // CMP_STRIDE
    pf4 = pf
    bg = batch * KV_GROUPS_B
    return pl.pallas_call(
        functools.partial(_compress_kernel, seq=seq),
        grid=(2, batch, KV_GROUPS_B),
        in_specs=[pl.BlockSpec((None, None, n_chunks, VIEW_RES * HEAD_DIM),
                               lambda kv, b, g: (PF_KC + kv * KV_GROUPS_B + g, b, 0, 0)),
                  pl.BlockSpec((None, CMP_LEN, HEAD_DIM), lambda kv, b, g: (kv, 0, 0)),
                  pl.BlockSpec((None, CMP_LEN, HEAD_DIM, HEAD_DIM), lambda kv, b, g: (kv, 0, 0, 0)),
                  pl.BlockSpec((None, HEAD_DIM, HEAD_DIM), lambda kv, b, g: (kv, 0, 0))],
        out_specs=[pl.BlockSpec((None, None, n_chunks, HEAD_DIM), lambda kv, b, g: (kv, b * KV_GROUPS_B + g, 0, 0)),
                   pl.BlockSpec((None, None, HEAD_DIM, n_chunks), lambda kv, b, g: (kv, b * KV_GROUPS_B + g, 0, 0))],
        out_shape=[jax.ShapeDtypeStruct((2, bg, n_chunks, HEAD_DIM), BF16),
                   jax.ShapeDtypeStruct((2, bg, HEAD_DIM, n_chunks), BF16)],
        compiler_params=_params("parallel", "parallel", "parallel"),
        name="compress",
    )(pf4, pe, w1, w2)


NSA_TQ = 512
NSA_TK = 512


def _select_blocks(score):
    n_s, tq = score.shape
    groups = n_s // 8
    rows8 = [score[8 * v:8 * v + 8, :] for v in range(groups)]
    rank8 = [jnp.zeros((8, tq), F32) for _ in range(groups)]
    sub = lax.broadcasted_iota(jnp.int32, (8, 1), 0)
    for jp in range(n_s):
        vp, sp = divmod(jp, 8)
        row = jnp.broadcast_to(rows8[vp][sp:sp + 1, :], (8, tq))
        for v in range(groups):
            if v > vp:
                beats = jnp.where(row >= rows8[v], 1.0, 0.0)
            elif v < vp:
                beats = jnp.where(row > rows8[v], 1.0, 0.0)
            else:
                beats = jnp.where(sub > sp, jnp.where(row >= rows8[v], 1.0, 0.0),
                                  jnp.where(row > rows8[v], 1.0, 0.0))
            rank8[v] = rank8[v] + beats
    rank = jnp.concatenate(rank8, axis=0)
    return (rank < N_SELECT) & (score > -jnp.inf)


SUM_ROWS = 16


def _with_sum_rows(vt):
    return jnp.concatenate([vt, jnp.ones((SUM_ROWS, vt.shape[1]), vt.dtype)], axis=0)


def _flash_block(s, vt_sum, carry):
    m_i, acc = carry
    m_new = jnp.maximum(m_i, jnp.max(s, axis=0, keepdims=True))
    alpha = jnp.exp2(m_i - m_new)
    p = jnp.exp2(s - m_new)
    return m_new, alpha * acc + _dot(vt_sum, p.astype(BF16))


def _normalized(acc):
    return acc[:HEAD_DIM] * (1.0 / acc[HEAD_DIM:HEAD_DIM + 1])


def _nsa_kernel(q_ref, kc_ref, vct_ref, ks_ref, vst_ref, kw_ref, vwt_ref, e_ref, wb_ref, gl_ref, o_ref, gates_ref,
                *, seq):
    tq, tk = NSA_TQ, NSA_TK
    cols = REP_B * tq
    n_cp = seq // CMP_STRIDE
    n_s = seq // SLC_BLOCK
    qi = pl.program_id(2)
    t0 = qi * tq
    q_t = jnp.concatenate([q_ref[r] for r in range(REP_B)], axis=1)
    lane = lax.broadcasted_iota(jnp.int32, (1, tq), 1)
    tpos = t0 + lane

    def per_head(x):
        return jnp.concatenate([x] * REP_B, axis=1)

    csub = lax.broadcasted_iota(jnp.int32, (n_cp, 1), 0)
    c_ok = (csub * CMP_STRIDE + (CMP_LEN - 1) <= tpos) & (csub < n_cp - 1)
    sc = _dot(kc_ref[...], q_t) + per_head(jnp.where(c_ok, 0.0, NEG))
    e = jnp.exp2(sc - jnp.max(sc, axis=0, keepdims=True))
    has_block = per_head(jnp.where(tpos >= CMP_LEN - 1, 1.0, 0.0))
    p_cmp = e * (has_block / jnp.maximum(jnp.sum(e, axis=0, keepdims=True), 1e-30))
    o_cmp = _dot(vct_ref[...], p_cmp.astype(BF16))

    ws = WIN + tq
    w0 = pl.multiple_of(jnp.maximum(t0 - WIN, 0), tq)
    s_w = _dot(kw_ref[pl.ds(w0, ws), :], q_t) + per_head(wb_ref[...])
    p_w = jnp.exp2(s_w - jnp.max(s_w, axis=0, keepdims=True))
    o_win = _normalized(_dot(_with_sum_rows(vwt_ref[:, pl.ds(w0, ws)]), p_w.astype(BF16)))

    p_sum = p_cmp[:, 0:tq]
    for r in range(1, REP_B):
        p_sum = p_sum + p_cmp[:, r * tq:(r + 1) * tq]
    jj = lax.broadcasted_iota(jnp.int32, (n_s, 1), 0)
    cidx = lax.broadcasted_iota(jnp.int32, (1, n_cp), 1)
    ratio = SLC_BLOCK // CMP_STRIDE
    c_first = ratio * jj - (CMP_LEN // CMP_STRIDE - 1)
    hits = jnp.where((cidx >= c_first) & (cidx < ratio * (jj + 1)), 1.0, 0.0).astype(BF16)
    p_hi = p_sum.astype(BF16)
    p_lo = (p_sum - p_hi.astype(F32)).astype(BF16)
    imp = _dot(hits, p_hi) + _dot(hits, p_lo)
    qblk = tpos >> (SLC_BLOCK.bit_length() - 1)
    forced = (jj == 0) | (jj == qblk) | (jj == qblk - 1)
    valid = jj * SLC_BLOCK <= tpos
    score = jnp.where(forced, FORCE_SCORE, jnp.where(valid, imp, -jnp.inf))
    sel = _select_blocks(score)
    bias = jnp.where(sel, 0.0, SEL_BIAS)
    if n_s < LANES:
        bias = jnp.concatenate([bias, jnp.full((LANES - n_s, tq), SEL_BIAS, F32)], axis=0)
    q_aug = jnp.concatenate([q_t, per_head(bias.astype(BF16))], axis=0)

    def slc_scores(kt):
        k0 = pl.multiple_of(kt * tk, tk)
        k_aug = jnp.concatenate([ks_ref[pl.ds(k0, tk), :], e_ref[pl.ds(k0, tk), :]], axis=1)
        return _dot(k_aug, q_aug)

    def slc_values(kt):
        return _with_sum_rows(vst_ref[:, pl.ds(pl.multiple_of(kt * tk, tk), tk)])

    def slc_step(kt, stats):
        return _flash_block(slc_scores(kt), slc_values(kt), stats)

    last = (t0 + tq - 1) // tk
    stats = (jnp.full((1, cols), NEG, F32), jnp.zeros((HEAD_DIM + SUM_ROWS, cols), F32))
    stats = lax.fori_loop(0, last // 2, lambda pi, st: slc_step(2 * pi + 1, slc_step(2 * pi, st)), stats)
    stats = lax.cond(last % 2 == 1, lambda st: slc_step(last - 1, st), lambda st: st, stats)
    tok = last * tk + lax.broadcasted_iota(jnp.int32, (tk, 1), 0)
    s_last = slc_scores(last) + per_head(jnp.where(tok <= tpos, 0.0, NEG))
    o_slc = _normalized(_flash_block(s_last, slc_values(last), stats)[1])

    gates_ref[...] = jax.nn.sigmoid(gl_ref[...].T)
    first = pl.program_id(1) * (REP_B * 3)

    def gate(k):
        return gates_ref[pl.ds(first + k, 1), :]

    for r in range(REP_B):
        sl = slice(r * tq, (r + 1) * tq)
        merged = (gate(3 * r) * o_cmp[:, sl] + gate(3 * r + 1) * o_slc[:, sl]
                  + gate(3 * r + 2) * o_win[:, sl])
        o_ref[:, r * HEAD_DIM:(r + 1) * HEAD_DIM] = merged.T


def nsa_attention(pr, pc, kvc, kvct, gl, batch, seq):
    tq = NSA_TQ
    n_s = seq // SLC_BLOCK
    assert seq % NSA_TK == 0 and NSA_TK % tq == 0 and seq >= WIN + tq and n_s <= LANES and n_s % 8 == 0
    assert WIN % tq == 0
    assert SLC_BLOCK & (SLC_BLOCK - 1) == 0
    n_cp = seq // CMP_STRIDE
    nq = seq // tq
    pr4 = pr.reshape(pr.shape[0], batch, seq, HEAD_DIM)
    block_onehot = (jnp.arange(seq)[:, None] // SLC_BLOCK == jnp.arange(LANES)[None, :]).astype(BF16)
    lead = jnp.arange(WIN // tq + 1)[:, None, None] * tq
    dist = lead + jnp.arange(tq)[None, None, :] - jnp.arange(WIN + tq)[None, :, None]
    win_bias = jnp.where((dist >= 0) & (dist <= WIN - 1), 0.0, NEG).astype(F32)

    def k_spec(off):
        return pl.BlockSpec((None, None, seq, HEAD_DIM), lambda b, g, i: (off + g, b, 0, 0))

    def vt_spec(off):
        return pl.BlockSpec((None, None, HEAD_DIM, seq), lambda b, g, i: (off + g, b, 0, 0))

    return pl.pallas_call(
        functools.partial(_nsa_kernel, seq=seq),
        grid=(batch, KV_GROUPS_B, nq),
        in_specs=[pl.BlockSpec((REP_B, None, HEAD_DIM, tq), lambda b, g, i: (PC_QB // REP_B + g, b, 0, i)),
                  pl.BlockSpec((None, None, n_cp, HEAD_DIM), lambda b, g, i: (0, b * KV_GROUPS_B + g, 0, 0)),
                  pl.BlockSpec((None, None, HEAD_DIM, n_cp), lambda b, g, i: (1, b * KV_GROUPS_B + g, 0, 0)),
                  k_spec(PR_KS), vt_spec(PC_VS), k_spec(PR_KW), vt_spec(PC_VW),
                  pl.BlockSpec((seq, LANES), lambda b, g, i: (0, 0)),
                  pl.BlockSpec((None, WIN + tq, tq), lambda b, g, i: (jnp.minimum(i, WIN // tq), 0, 0)),
                  pl.BlockSpec((tq, LANES), lambda b, g, i: (b * nq + i, 0))],
        out_specs=pl.BlockSpec((tq, REP_B * HEAD_DIM), lambda b, g, i: (b * nq + i, g)),
        out_shape=jax.ShapeDtypeStruct((batch * seq, WIDTH_B), F32),
        scratch_shapes=[pltpu.VMEM((LANES, tq), F32)],
        compiler_params=_params("parallel", "parallel", "arbitrary"),
        name="nsa_attention",
    )(pc, kvc, kvct, pr4, pc, pr4, pc, block_onehot, win_bias, gl)


def _mm_res_kernel(a_ref, w_ref, r_ref, o_ref):
    o_ref[...] = _dot(a_ref[...], w_ref[...]) + r_ref[...]


def matmul_residual(a, w, res, tm, tn, weights_resident=False):
    m, k = a.shape
    n = w.shape[1]
    if weights_resident:
        grid = (n // tn, m // tm)
        row, col = (lambda j, i: i), (lambda j, i: j)
    else:
        grid = (m // tm, n // tn)
        row, col = (lambda i, j: i), (lambda i, j: j)
    return pl.pallas_call(
        _mm_res_kernel,
        grid=grid,
        in_specs=[pl.BlockSpec((tm, k), lambda p, q: (row(p, q), 0)),
                  pl.BlockSpec((k, tn), lambda p, q: (0, col(p, q))),
                  pl.BlockSpec((tm, tn), lambda p, q: (row(p, q), col(p, q)))],
        out_specs=pl.BlockSpec((tm, tn), lambda p, q: (row(p, q), col(p, q))),
        out_shape=jax.ShapeDtypeStruct((m, n), F32),
        compiler_params=_params("parallel", "arbitrary"),
        name="matmul_residual",
    )(a, w, res)


def _ffn_up_kernel(h_ref, wg_ref, wu_ref, o_ref):
    h = h_ref[...]
    g = _dot(h, wg_ref[...].astype(BF16))
    u = _dot(h, wu_ref[...].astype(BF16))
    o_ref[...] = (g * jax.nn.sigmoid(g) * u).astype(o_ref.dtype)


def ffn_up(h, wg, wu, layer, tm, tn):
    m, k = h.shape
    n = wg.shape[2]
    return pl.pallas_call(
        _ffn_up_kernel,
        grid=(m // tm, n // tn),
        in_specs=[pl.BlockSpec((tm, k), lambda i, j: (i, 0)),
                  pl.BlockSpec((None, k, tn), lambda i, j: (layer, 0, j)),
                  pl.BlockSpec((None, k, tn), lambda i, j: (layer, 0, j))],
        out_specs=pl.BlockSpec((tm, tn), lambda i, j: (i, j)),
        out_shape=jax.ShapeDtypeStruct((m, n), BF16),
        compiler_params=_params("parallel", "arbitrary"),
        name="ffn_up",
    )(h, wg, wu)


def _cast_kernel(w_ref, o_ref):
    o_ref[...] = w_ref[...].astype(o_ref.dtype)


def cast_columns(w, layer, n_cols, tk=512, tn=2816):
    _, k, _ = w.shape
    return pl.pallas_call(
        _cast_kernel,
        grid=(k // tk, n_cols // tn),
        in_specs=[pl.BlockSpec((None, tk, tn), lambda i, j: (layer, i, j))],
        out_specs=pl.BlockSpec((tk, tn), lambda i, j: (i, j)),
        out_shape=jax.ShapeDtypeStruct((k, n_cols), BF16),
        compiler_params=_params("parallel", "parallel"),
        name="cast_columns",
    )(w)


def _layer(x, norm_attn, w_in, layer, ck_pe, ck_w1, ck_w2, cv_pe, cv_w1, cv_w2,
           out_norm_a, out_norm_b, w_out, norm_ffn, w_gate, w_up, w_down, cos_t, sin_t, batch, seq):
    w_main = w_in.astype(BF16)
    w_gl = lax.slice(w_in, (layer, 0, D_MAIN), (layer + 1, D_MODEL, D_MAIN + N_GATES))
    w_gl = jnp.pad(w_gl.reshape(D_MODEL, N_GATES), ((0, 0), (0, LANES - N_GATES))).astype(BF16)
    pe = jnp.stack([ck_pe, cv_pe])
    w1 = jnp.stack([ck_w1, cv_w1]).reshape(2, CMP_LEN, HEAD_DIM, HEAD_DIM).astype(BF16)
    w2 = jnp.stack([ck_w2, cv_w2]).astype(BF16)

    h, gl = rms_gate(x, norm_attn, w_gl)
    pf = project(h, w_main, layer, cos_t, sin_t, F32_TILES, F32, batch, seq, layout="view")
    pr = project(h, w_main, layer, cos_t, sin_t, ROW_TILES, BF16, batch, seq)
    pc = project(h, w_main, layer, cos_t, sin_t, COL_TILES, BF16, batch, seq, layout="cols")
    o_a = dilated_attention(pf, batch, seq)
    kvc, kvct = compress(pf, pe, w1, w2, batch, seq)
    o_b = nsa_attention(pr, pc, kvc, kvct, gl, batch, seq)
    mixed = rms_pair(o_a, o_b, out_norm_a, out_norm_b)
    x1 = matmul_residual(mixed, w_out.astype(BF16), x, tm=1024, tn=1024)
    h2 = rms(x1, norm_ffn, BF16)
    act = ffn_up(h2, w_gate, w_up, layer, tm=2048, tn=256)
    x2 = matmul_residual(act, w_down.astype(BF16), x1, tm=512, tn=512, weights_resident=True)
    return x2


def kernel(x, norm_attn, w_in, ck_pe, ck_w1, ck_w2, cv_pe, cv_w1, cv_w2, out_norm_a, out_norm_b, w_out,
           norm_ffn, w_gate, w_up, w_down, norm_final):
    batch, seq, d = x.shape
    depth = w_in.shape[0]
    cos_t, sin_t = _rope_tables(seq)
    xf = x.reshape(batch * seq, d)
    for l in range(depth):
        xf = _layer(xf, norm_attn[l], w_in, l, ck_pe[l], ck_w1[l], ck_w2[l], cv_pe[l], cv_w1[l], cv_w2[l],
                    out_norm_a[l], out_norm_b[l], w_out[l], norm_ffn[l], w_gate, w_up, w_down[l],
                    cos_t, sin_t, batch, seq)
    return rms(xf, norm_final, F32).reshape(batch, seq, d)
```

```python
import functools
import math

import jax
import jax.numpy as jnp
from jax import lax
from jax.experimental import pallas as pl
from jax.experimental.pallas import tpu as pltpu

F32 = jnp.float32
BF16 = jnp.bfloat16

D_MODEL = 4096
HEAD_DIM = 128
HEADS_A = 16
HEADS_B = 16
KV_GROUPS_B = 4
REP_B = HEADS_B // KV_GROUPS_B
WIDTH_A = HEADS_A * HEAD_DIM
WIDTH_B = HEADS_B * HEAD_DIM
DILATED_CONFIGS = ((128, 1), (512, 4), (2048, 16))
BLK = 128
CMP_LEN = 32
CMP_STRIDE = 16
SLC_BLOCK = 64
N_SELECT = 16
WIN = 512
FORCE_SCORE = 1e6
ROPE_THETA = 500000.0
ROPE_DIM = HEAD_DIM // 4
EPS = 1e-5
Q_SCALE = HEAD_DIM ** -0.5 * math.log2(math.e)
N_GATES = 3 * HEADS_B
D_MAIN = 3 * WIDTH_A + WIDTH_B + 6 * KV_GROUPS_B * HEAD_DIM

VMEM_LIMIT_BYTES = 56 * 1024 * 1024
LANES = 128
NEG = -1e30
SEL_BIAS = -32768.0

PROJ_TILE = 4 * HEAD_DIM
F32_TILES = tuple(range(12)) + (16, 17)
ROW_TILES = (18, 20)
COL_TILES = (12, 13, 14, 15, 19, 21)
Q_TILES = (0, 1, 2, 3, 12, 13, 14, 15)
V_TILES = (8, 9, 10, 11, 17, 19, 21)
PF_QA, PF_KA, PF_VA, PF_KC, PF_VC = 0, 16, 32, 48, 52
PR_KS, PR_KW = 0, 4
PC_QB, PC_VS, PC_VW = 0, 16, 20


def _params(*sem):
    return pltpu.CompilerParams(dimension_semantics=sem, vmem_limit_bytes=VMEM_LIMIT_BYTES)


def _dot(a, b):
    return jnp.dot(a, b, preferred_element_type=F32)


def _dot_nt(a, b):
    return lax.dot_general(a, b, (((1,), (1,)), ((), ())), preferred_element_type=F32)


def _rms(x, g):
    return x * lax.rsqrt(jnp.mean(x * x, axis=-1, keepdims=True) + EPS) * g


def _rms_gate_kernel(x_ref, g_ref, wgl_ref, h_ref, gl_ref):
    hb = _rms(x_ref[...], g_ref[...]).astype(BF16)
    h_ref[...] = hb
    gl_ref[...] = _dot(hb, wgl_ref[...])


def rms_gate(x, gain, w_gl, tm=512):
    m, d = x.shape
    ng = w_gl.shape[1]
    return pl.pallas_call(
        _rms_gate_kernel,
        grid=(m // tm,),
        in_specs=[pl.BlockSpec((tm, d), lambda i: (i, 0)),
                  pl.BlockSpec((1, d), lambda i: (0, 0)),
                  pl.BlockSpec((d, ng), lambda i: (0, 0))],
        out_specs=[pl.BlockSpec((tm, d), lambda i: (i, 0)),
                   pl.BlockSpec((tm, ng), lambda i: (i, 0))],
        out_shape=[jax.ShapeDtypeStruct((m, d), BF16), jax.ShapeDtypeStruct((m, ng), F32)],
        compiler_params=_params("parallel"),
        name="rms_gate",
    )(x, gain.reshape(1, d), w_gl)


def _rms_kernel(x_ref, g_ref, o_ref):
    o_ref[...] = _rms(x_ref[...], g_ref[...]).astype(o_ref.dtype)


def rms(x, gain, out_dtype, tm=512):
    m, d = x.shape
    return pl.pallas_call(
        _rms_kernel,
        grid=(m // tm,),
        in_specs=[pl.BlockSpec((tm, d), lambda i: (i, 0)),
                  pl.BlockSpec((1, d), lambda i: (0, 0))],
        out_specs=pl.BlockSpec((tm, d), lambda i: (i, 0)),
        out_shape=jax.ShapeDtypeStruct((m, d), out_dtype),
        compiler_params=_params("parallel"),
        name="rms",
    )(x, gain.reshape(1, d))


def _rms_pair_kernel(a_ref, b_ref, ga_ref, gb_ref, o_ref):
    n_heads, nv, _ = a_ref.shape
    wa = n_heads * HEAD_DIM
    lanes = [slice(r * HEAD_DIM, (r + 1) * HEAD_DIM) for r in range(VIEW_RES)]
    inv = []
    for r in range(VIEW_RES):
        sq = a_ref[0, :, lanes[r]] * a_ref[0, :, lanes[r]]
        for hh in range(1, n_heads):
            sq = sq + a_ref[hh, :, lanes[r]] * a_ref[hh, :, lanes[r]]
        inv.append(lax.rsqrt(jnp.sum(sq, axis=-1, keepdims=True) * (1.0 / wa) + EPS))
    for hh in range(n_heads):
        by_res = jnp.stack([a_ref[hh, :, lanes[r]] * inv[r] for r in range(VIEW_RES)], axis=0)
        tokens = jnp.swapaxes(by_res, 0, 1).reshape(nv * VIEW_RES, HEAD_DIM)
        cols = slice(hh * HEAD_DIM, (hh + 1) * HEAD_DIM)
        o_ref[:, cols] = (tokens * ga_ref[:, cols]).astype(o_ref.dtype)
    o_ref[:, wa:] = _rms(b_ref[...], gb_ref[...]).astype(o_ref.dtype)


def rms_pair(a, b, ga, gb, tm=512):
    n_heads, batch, rows, width = a.shape
    assert width == VIEW_RES * HEAD_DIM and (rows * VIEW_RES) % tm == 0
    wa = n_heads * HEAD_DIM
    m, wb = b.shape
    tiles_per_seq = rows * VIEW_RES // tm
    return pl.pallas_call(
        _rms_pair_kernel,
        grid=(m // tm,),
        in_specs=[pl.BlockSpec((n_heads, None, tm // VIEW_RES, width),
                               lambda i: (0, i // tiles_per_seq, i % tiles_per_seq, 0)),
                  pl.BlockSpec((tm, wb), lambda i: (i, 0)),
                  pl.BlockSpec((1, wa), lambda i: (0, 0)),
                  pl.BlockSpec((1, wb), lambda i: (0, 0))],
        out_specs=pl.BlockSpec((tm, wa + wb), lambda i: (i, 0)),
        out_shape=jax.ShapeDtypeStruct((m, wa + wb), BF16),
        compiler_params=_params("parallel"),
        name="rms_pair",
    )(a, b, ga.reshape(1, wa), gb.reshape(1, wb))


def _rope_tables(seq):
    inv = ROPE_THETA ** (-jnp.arange(0, ROPE_DIM, 2, dtype=F32) / ROPE_DIM)
    ang = jnp.arange(seq, dtype=F32)[:, None] * inv[None, :]
    cos, sin = jnp.cos(ang), jnp.sin(ang)
    ones = jnp.ones((seq, HEAD_DIM - ROPE_DIM), F32)
    cos_t = jnp.concatenate([cos, cos, ones], axis=1)
    sin_t = jnp.concatenate([-sin, sin, 0.0 * ones], axis=1)
    return cos_t, sin_t


PROJ_ROWS = 256
VIEW_RES = DILATED_CONFIGS[-1][1]


def _proj_kernel(h_ref, w_ref, cos_ref, sin_ref, o_ref, *, src_tiles, layout):
    jt = pl.program_id(1)
    is_v = functools.reduce(jnp.logical_or, [jt == n for n, t in enumerate(src_tiles) if t in V_TILES], False)
    is_q = functools.reduce(jnp.logical_or, [jt == n for n, t in enumerate(src_tiles) if t in Q_TILES], False)
    heads_per_tile = PROJ_TILE // HEAD_DIM
    tm = h_ref.shape[0]
    scale = jnp.where(is_q, Q_SCALE, 1.0).astype(F32)
    lane = lax.broadcasted_iota(jnp.int32, (1, HEAD_DIM), 1)
    first_half = lane < ROPE_DIM // 2

    for c in range(tm // PROJ_ROWS):
        rows = slice(c * PROJ_ROWS, (c + 1) * PROJ_ROWS)
        y = _dot(h_ref[rows, :], w_ref[...])
        cos_t = jnp.where(is_v, 1.0, cos_ref[rows, :]) * scale
        sin_t = jnp.where(is_v, 0.0, sin_ref[rows, :]) * scale
        for hh in range(heads_per_tile):
            yh = y[:, hh * HEAD_DIM:(hh + 1) * HEAD_DIM]
            partner = jnp.where(first_half,
                                pltpu.roll(yh, HEAD_DIM - ROPE_DIM // 2, axis=1),
                                pltpu.roll(yh, ROPE_DIM // 2, axis=1))
            val = yh * cos_t + partner * sin_t
            if layout == "cols":
                for cc in range(PROJ_ROWS // LANES):
                    col0 = c * PROJ_ROWS + cc * LANES
                    o_ref[hh, :, col0:col0 + LANES] = val[cc * LANES:(cc + 1) * LANES, :].T.astype(o_ref.dtype)
            elif layout == "view":
                nv = PROJ_ROWS // VIEW_RES
                by_res = jnp.swapaxes(val.reshape(nv, VIEW_RES, HEAD_DIM), 0, 1)
                for r in range(VIEW_RES):
                    o_ref[hh, c * nv:(c + 1) * nv, r * HEAD_DIM:(r + 1) * HEAD_DIM] = by_res[r].astype(o_ref.dtype)
            else:
                o_ref[hh, rows, :] = val.astype(o_ref.dtype)


def project(h, w, layer, cos_t, sin_t, src_tiles, out_dtype, batch, seq, layout="rows", tm=2048):
    m, d = h.shape
    n_tiles = len(src_tiles)
    heads_per_tile = PROJ_TILE // HEAD_DIM
    tm = min(tm, seq)
    pos_blocks = seq // tm

    def w_map(i, j):
        col = functools.reduce(lambda acc, nt: jnp.where(j == nt[0], nt[1], acc),
                               list(enumerate(src_tiles)), 0)
        return (layer, 0, col)

    if layout == "cols":
        out_spec = pl.BlockSpec((heads_per_tile, None, HEAD_DIM, tm),
                                lambda i, j: (j, i // pos_blocks, 0, i % pos_blocks))
        out_shape = jax.ShapeDtypeStruct((n_tiles * heads_per_tile, batch, HEAD_DIM, seq), out_dtype)
    elif layout == "view":
        out_spec = pl.BlockSpec((heads_per_tile, None, tm // VIEW_RES, VIEW_RES * HEAD_DIM),
                                lambda i, j: (j, i // pos_blocks, i % pos_blocks, 0))
        out_shape = jax.ShapeDtypeStruct((n_tiles * heads_per_tile, batch, seq // VIEW_RES, VIEW_RES * HEAD_DIM),
                                         out_dtype)
    else:
        out_spec = pl.BlockSpec((heads_per_tile, tm, HEAD_DIM), lambda i, j: (j, i, 0))
        out_shape = jax.ShapeDtypeStruct((n_tiles * heads_per_tile, m, HEAD_DIM), out_dtype)

    return pl.pallas_call(
        functools.partial(_proj_kernel, src_tiles=src_tiles, layout=layout),
        grid=(m // tm, n_tiles),
        in_specs=[pl.BlockSpec((tm, d), lambda i, j: (i, 0)),
                  pl.BlockSpec((None, d, PROJ_TILE), w_map),
                  pl.BlockSpec((tm, HEAD_DIM), lambda i, j: (i % pos_blocks, 0)),
                  pl.BlockSpec((tm, HEAD_DIM), lambda i, j: (i % pos_blocks, 0))],
        out_specs=out_spec,
        out_shape=out_shape,
        compiler_params=_params("parallel", "arbitrary"),
        name="project_" + layout,
    )(h, w, cos_t, sin_t)


def _aligned(x, multiple):
    return x if isinstance(x, int) else pl.multiple_of(x, multiple)


DIL_UNROLL = 32
DIL_RES = VIEW_RES


def _dilated_kernel(q_ref, k_ref, v_ref, o_ref, acc_ref, m_ref, l_ref, bias_ref, *, seq):
    for ci, (window, dil) in enumerate(DILATED_CONFIGS):
        band = window // dil
        n_pieces = DIL_RES // dil
        pr = BLK // n_pieces
        kr = 2 * pr
        tiles_per_res = seq // (BLK * dil)
        assert band <= BLK and pr % 8 == 0 and pr & (pr - 1) == 0

        q_row = lax.broadcasted_iota(jnp.int32, (BLK, 1), 0)
        k_row = lax.broadcasted_iota(jnp.int32, (1, 2 * BLK), 1)
        q_sub = n_pieces * (q_row & (pr - 1)) + (q_row >> (pr.bit_length() - 1))
        k_sub = n_pieces * (k_row & (kr - 1)) + (k_row >> (kr.bit_length() - 1))
        for lead in range(2):
            dist = (q_sub + lead * BLK) - k_sub
            bias_ref[2 * ci + lead] = jnp.where((dist >= 0) & (dist <= band), 0.0, NEG)

        def body(step, carry, ci=ci, dil=dil, band=band, n_pieces=n_pieces, pr=pr, kr=kr,
                 tiles_per_res=tiles_per_res):
            def pieces(ref, row0, nrows, r):
                return [ref[pl.ds(row0, nrows), pl.ds(_aligned((r + dil * a) * HEAD_DIM, HEAD_DIM), HEAD_DIM)]
                        for a in range(n_pieces)]

            tiles = []
            for u in range(DIL_UNROLL):
                idx = step * DIL_UNROLL + u
                r = idx // tiles_per_res
                i = idx - r * tiles_per_res
                kb = max(i - 1, 0) if isinstance(i, int) else jnp.maximum(i - 1, 0)
                q_row0 = _aligned(i * pr, pr)
                k_row0 = _aligned(kb * pr, pr)
                tiles.append((r, i, kb, q_row0, k_row0))

            def update(r, q_row0, m_old, m_new, l_tile, pv):
                if ci == 0:
                    l_new = l_tile
                    acc_new = pv
                else:
                    alpha = jnp.exp2(m_old - m_new)
                    l_new = alpha * jnp.concatenate(pieces(l_ref, q_row0, pr, r), axis=0) + l_tile
                    acc_new = alpha * jnp.concatenate(pieces(acc_ref, q_row0, pr, r), axis=0) + pv
                for a in range(n_pieces):
                    col = pl.ds(_aligned((r + dil * a) * HEAD_DIM, HEAD_DIM), HEAD_DIM)
                    rows = slice(a * pr, (a + 1) * pr)
                    m_ref[pl.ds(q_row0, pr), col] = m_new[rows]
                    l_ref[pl.ds(q_row0, pr), col] = l_new[rows]
                    acc_ref[pl.ds(q_row0, pr), col] = acc_new[rows]

            scores, m_olds, maxes, probs, results = {}, {}, {}, {}, []
            for t in range(DIL_UNROLL + 4):
                if t < DIL_UNROLL:
                    (r, i, kb, q_row0, k_row0) = tiles[t]
                    q = jnp.concatenate(pieces(q_ref, q_row0, pr, r), axis=0).astype(BF16)
                    k = jnp.concatenate(pieces(k_ref, k_row0, kr, r), axis=0).astype(BF16)
                    scores[t] = _dot_nt(q, k) + bias_ref[2 * ci + i - kb]
                if 0 <= t - 1 < DIL_UNROLL:
                    m_tile = jnp.max(scores[t - 1], axis=1, keepdims=True)
                    (r, i, kb, q_row0, k_row0) = tiles[t - 1]
                    m_old = None if ci == 0 else jnp.concatenate(pieces(m_ref, q_row0, pr, r), axis=0)
                    m_olds[t - 1] = m_old
                    maxes[t - 1] = jnp.broadcast_to(m_tile, (BLK, LANES)) if ci == 0 else jnp.maximum(m_old, m_tile)
                if 0 <= t - 2 < DIL_UNROLL:
                    m_new = maxes[t - 2]
                    probs[t - 2] = jnp.exp2(scores.pop(t - 2) - jnp.concatenate([m_new, m_new], axis=1))
                if 0 <= t - 3 < DIL_UNROLL:
                    (r, i, kb, q_row0, k_row0) = tiles[t - 3]
                    m_old = m_olds.pop(t - 3)
                    p = probs.pop(t - 3)
                    v = jnp.concatenate(pieces(v_ref, k_row0, kr, r), axis=0).astype(BF16)
                    v_sum = jnp.concatenate([v, jnp.ones((2 * BLK, LANES), BF16)], axis=1)
                    pv = _dot(p.astype(BF16), v_sum)
                    results.append((r, q_row0, m_old, maxes.pop(t - 3), pv[:, HEAD_DIM:], pv[:, :HEAD_DIM]))
                if 0 <= t - 4 < DIL_UNROLL:
                    update(*results[t - 4])
            return carry

        for step in range(seq // (BLK * DIL_UNROLL)):
            body(step, 0)
    o_ref[...] = acc_ref[...] / l_ref[...]


def dilated_attention(pf, batch, seq):
    assert all(DIL_RES % d == 0 for _, d in DILATED_CONFIGS)
    assert seq % (BLK * DIL_RES) == 0 and seq // DIL_RES >= 2 * BLK and (seq // BLK) % DIL_UNROLL == 0
    rows = seq // DIL_RES
    width = DIL_RES * HEAD_DIM
    view = pf

    def spec(off):
        return pl.BlockSpec((None, None, rows, width), lambda b, h: (off + h, b, 0, 0))

    out = pl.pallas_call(
        functools.partial(_dilated_kernel, seq=seq),
        grid=(batch, HEADS_A),
        in_specs=[spec(PF_QA), spec(PF_KA), spec(PF_VA)],
        out_specs=spec(0),
        out_shape=jax.ShapeDtypeStruct((HEADS_A, batch, rows, width), F32),
        scratch_shapes=[pltpu.VMEM((rows, width), F32)] * 3
        + [pltpu.VMEM((2 * len(DILATED_CONFIGS), BLK, 2 * BLK), F32)],
        compiler_params=_params("parallel", "parallel"),
        name="dilated_attention",
    )(view, view, view)
    return out


def _compress_kernel(t_ref, pe_ref, w1_ref, w2_ref, o_ref, ot_ref, *, seq):
    n_chunks = seq // CMP_STRIDE
    first = jnp.zeros((n_chunks, HEAD_DIM), F32)
    second = jnp.zeros((n_chunks, HEAD_DIM), F32)
    for i in range(CMP_STRIDE):
        ti = t_ref[:, i * HEAD_DIM:(i + 1) * HEAD_DIM]
        first += _dot((ti + pe_ref[pl.ds(i, 1), :]).astype(BF16), w1_ref[i])
        second += _dot((ti + pe_ref[pl.ds(CMP_STRIDE + i, 1), :]).astype(BF16), w1_ref[CMP_STRIDE + i])
    pre = first + pltpu.roll(second, n_chunks - 1, axis=0)
    out = _dot(jax.nn.gelu(pre).astype(BF16), w2_ref[...])
    o_ref[...] = out.astype(o_ref.dtype)
    ot_ref[...] = out.T.astype(ot_ref.dtype)


def compress(pf, pe, w1, w2, batch, seq):
    assert CMP_LEN == 2 * CMP_STRIDE and CMP_STRIDE == VIEW_RES
    n_chunks = seq // CMP_STRIDE
    pf4 = pf
    bg = batch * KV_GROUPS_B
    return pl.pallas_call(
        functools.partial(_compress_kernel, seq=seq),
        grid=(2, batch, KV_GROUPS_B),
        in_specs=[pl.BlockSpec((None, None, n_chunks, VIEW_RES * HEAD_DIM),
                               lambda kv, b, g: (PF_KC + kv * KV_GROUPS_B + g, b, 0, 0)),
                  pl.BlockSpec((None, CMP_LEN, HEAD_DIM), lambda kv, b, g: (kv, 0, 0)),
                  pl.BlockSpec((None, CMP_LEN, HEAD_DIM, HEAD_DIM), lambda kv, b, g: (kv, 0, 0, 0)),
                  pl.BlockSpec((None, HEAD_DIM, HEAD_DIM), lambda kv, b, g: (kv, 0, 0))],
        out_specs=[pl.BlockSpec((None, None, n_chunks, HEAD_DIM), lambda kv, b, g: (kv, b * KV_GROUPS_B + g, 0, 0)),
                   pl.BlockSpec((None, None, HEAD_DIM, n_chunks), lambda kv, b, g: (kv, b * KV_GROUPS_B + g, 0, 0))],
        out_shape=[jax.ShapeDtypeStruct((2, bg, n_chunks, HEAD_DIM), BF16),
                   jax.ShapeDtypeStruct((2, bg, HEAD_DIM, n_chunks), BF16)],
        compiler_params=_params("parallel", "parallel", "parallel"),
        name="compress",
    )(pf4, pe, w1, w2)


NSA_TQ = 512
NSA_TK = 512


def _select_blocks(score):
    n_s, tq = score.shape
    groups = n_s // 8
    rows8 = [score[8 * v:8 * v + 8, :] for v in range(groups)]
    rank8 = [jnp.zeros((8, tq), F32) for _ in range(groups)]
    sub = lax.broadcasted_iota(jnp.int32, (8, 1), 0)
    for jp in range(n_s):
        vp, sp = divmod(jp, 8)
        row = jnp.broadcast_to(rows8[vp][sp:sp + 1, :], (8, tq))
        for v in range(groups):
            if v > vp:
                beats = jnp.where(row >= rows8[v], 1.0, 0.0)
            elif v < vp:
                beats = jnp.where(row > rows8[v], 1.0, 0.0)
            else:
                beats = jnp.where(sub > sp, jnp.where(row >= rows8[v], 1.0, 0.0),
                                  jnp.where(row > rows8[v], 1.0, 0.0))
            rank8[v] = rank8[v] + beats
    rank = jnp.concatenate(rank8, axis=0)
    return (rank < N_SELECT) & (score > -jnp.inf)


SUM_ROWS = 16


def _with_sum_rows(vt):
    return jnp.concatenate([vt, jnp.ones((SUM_ROWS, vt.shape[1]), vt.dtype)], axis=0)


def _flash_block(s, vt_sum, carry):
    m_i, acc = carry
    m_new = jnp.maximum(m_i, jnp.max(s, axis=0, keepdims=True))
    alpha = jnp.exp2(m_i - m_new)
    p = jnp.exp2(s - m_new)
    return m_new, alpha * acc + _dot(vt_sum, p.astype(BF16))


def _normalized(acc):
    return acc[:HEAD_DIM] * (1.0 / acc[HEAD_DIM:HEAD_DIM + 1])


def _nsa_kernel(q_ref, kc_ref, vct_ref, ks_ref, vst_ref, kw_ref, vwt_ref, e_ref, wb_ref, gl_ref, o_ref, gates_ref,
                *, seq):
    tq, tk = NSA_TQ, NSA_TK
    cols = REP_B * tq
    n_cp = seq // CMP_STRIDE
    n_s = seq // SLC_BLOCK
    qi = pl.program_id(2)
    t0 = qi * tq
    q_t = jnp.concatenate([q_ref[r] for r in range(REP_B)], axis=1)
    lane = lax.broadcasted_iota(jnp.int32, (1, tq), 1)
    tpos = t0 + lane

    def per_head(x):
        return jnp.concatenate([x] * REP_B, axis=1)

    csub = lax.broadcasted_iota(jnp.int32, (n_cp, 1), 0)
    c_ok = (csub * CMP_STRIDE + (CMP_LEN - 1) <= tpos) & (csub < n_cp - 1)
    sc = _dot(kc_ref[...], q_t) + per_head(jnp.where(c_ok, 0.0, NEG))
    e = jnp.exp2(sc - jnp.max(sc, axis=0, keepdims=True))
    has_block = per_head(jnp.where(tpos >= CMP_LEN - 1, 1.0, 0.0))
    p_cmp = e * (has_block / jnp.maximum(jnp.sum(e, axis=0, keepdims=True), 1e-30))
    o_cmp = _dot(vct_ref[...], p_cmp.astype(BF16))

    ws = WIN + tq
    w0 = pl.multiple_of(jnp.maximum(t0 - WIN, 0), tq)
    s_w = _dot(kw_ref[pl.ds(w0, ws), :], q_t) + per_head(wb_ref[...])
    p_w = jnp.exp2(s_w - jnp.max(s_w, axis=0, keepdims=True))
    o_win = _normalized(_dot(_with_sum_rows(vwt_ref[:, pl.ds(w0, ws)]), p_w.astype(BF16)))

    p_sum = p_cmp[:, 0:tq]
    for r in range(1, REP_B):
        p_sum = p_sum + p_cmp[:, r * tq:(r + 1) * tq]
    jj = lax.broadcasted_iota(jnp.int32, (n_s, 1), 0)
    cidx = lax.broadcasted_iota(jnp.int32, (1, n_cp), 1)
    ratio = SLC_BLOCK // CMP_STRIDE
    c_first = ratio * jj - (CMP_LEN // CMP_STRIDE - 1)
    hits = jnp.where((cidx >= c_first) & (cidx < ratio * (jj + 1)), 1.0, 0.0).astype(BF16)
    p_hi = p_sum.astype(BF16)
    p_lo = (p_sum - p_hi.astype(F32)).astype(BF16)
    imp = _dot(hits, p_hi) + _dot(hits, p_lo)
    qblk = tpos >> (SLC_BLOCK.bit_length() - 1)
    forced = (jj == 0) | (jj == qblk) | (jj == qblk - 1)
    valid = jj * SLC_BLOCK <= tpos
    score = jnp.where(forced, FORCE_SCORE, jnp.where(valid, imp, -jnp.inf))
    sel = _select_blocks(score)
    bias = jnp.where(sel, 0.0, SEL_BIAS)
    if n_s < LANES:
        bias = jnp.concatenate([bias, jnp.full((LANES - n_s, tq), SEL_BIAS, F32)], axis=0)
    q_aug = jnp.concatenate([q_t, per_head(bias.astype(BF16))], axis=0)

    def slc_scores(kt):
        k0 = _aligned(kt * tk, tk)
        k_aug = jnp.concatenate([ks_ref[pl.ds(k0, tk), :], e_ref[pl.ds(k0, tk), :]], axis=1)
        return _dot(k_aug, q_aug)

    def slc_values(kt):
        return _with_sum_rows(vst_ref[:, pl.ds(_aligned(kt * tk, tk), tk)])

    def slc_step(kt, stats):
        return _flash_block(slc_scores(kt), slc_values(kt), stats)

    last = (t0 + tq - 1) // tk
    stats = (jnp.full((1, cols), NEG, F32), jnp.zeros((HEAD_DIM + SUM_ROWS, cols), F32))
    def full_blocks(n):
        def run(st):
            for kt in range(n):
                st = slc_step(kt, st)
            return st
        return run

    stats = lax.switch(last, [full_blocks(n) for n in range(seq // tk)], stats)
    tok = last * tk + lax.broadcasted_iota(jnp.int32, (tk, 1), 0)
    s_last = slc_scores(last) + per_head(jnp.where(tok <= tpos, 0.0, NEG))
    o_slc = _normalized(_flash_block(s_last, slc_values(last), stats)[1])

    gates_ref[...] = jax.nn.sigmoid(gl_ref[...].T)
    first = pl.program_id(1) * (REP_B * 3)

    def gate(k):
        return gates_ref[pl.ds(first + k, 1), :]

    for r in range(REP_B):
        sl = slice(r * tq, (r + 1) * tq)
        merged = (gate(3 * r) * o_cmp[:, sl] + gate(3 * r + 1) * o_slc[:, sl]
                  + gate(3 * r + 2) * o_win[:, sl])
        o_ref[:, r * HEAD_DIM:(r + 1) * HEAD_DIM] = merged.T


def nsa_attention(pr, pc, kvc, kvct, gl, batch, seq):
    tq = NSA_TQ
    n_s = seq // SLC_BLOCK
    assert seq % NSA_TK == 0 and NSA_TK % tq == 0 and seq >= WIN + tq and n_s <= LANES and n_s % 8 == 0
    assert WIN % tq == 0
    assert SLC_BLOCK & (SLC_BLOCK - 1) == 0
    n_cp = seq // CMP_STRIDE
    nq = seq // tq
    pr4 = pr.reshape(pr.shape[0], batch, seq, HEAD_DIM)
    block_onehot = (jnp.arange(seq)[:, None] // SLC_BLOCK == jnp.arange(LANES)[None, :]).astype(BF16)
    lead = jnp.arange(WIN // tq + 1)[:, None, None] * tq
    dist = lead + jnp.arange(tq)[None, None, :] - jnp.arange(WIN + tq)[None, :, None]
    win_bias = jnp.where((dist >= 0) & (dist <= WIN - 1), 0.0, NEG).astype(F32)

    def k_spec(off):
        return pl.BlockSpec((None, None, seq, HEAD_DIM), lambda b, g, i: (off + g, b, 0, 0))

    def vt_spec(off):
        return pl.BlockSpec((None, None, HEAD_DIM, seq), lambda b, g, i: (off + g, b, 0, 0))

    return pl.pallas_call(
        functools.partial(_nsa_kernel, seq=seq),
        grid=(batch, KV_GROUPS_B, nq),
        in_specs=[pl.BlockSpec((REP_B, None, HEAD_DIM, tq), lambda b, g, i: (PC_QB // REP_B + g, b, 0, i)),
                  pl.BlockSpec((None, None, n_cp, HEAD_DIM), lambda b, g, i: (0, b * KV_GROUPS_B + g, 0, 0)),
                  pl.BlockSpec((None, None, HEAD_DIM, n_cp), lambda b, g, i: (1, b * KV_GROUPS_B + g, 0, 0)),
                  k_spec(PR_KS), vt_spec(PC_VS), k_spec(PR_KW), vt_spec(PC_VW),
                  pl.BlockSpec((seq, LANES), lambda b, g, i: (0, 0)),
                  pl.BlockSpec((None, WIN + tq, tq), lambda b, g, i: (jnp.minimum(i, WIN // tq), 0, 0)),
                  pl.BlockSpec((tq, LANES), lambda b, g, i: (b * nq + i, 0))],
        out_specs=pl.BlockSpec((tq, REP_B * HEAD_DIM), lambda b, g, i: (b * nq + i, g)),
        out_shape=jax.ShapeDtypeStruct((batch * seq, WIDTH_B), F32),
        scratch_shapes=[pltpu.VMEM((LANES, tq), F32)],
        compiler_params=_params("parallel", "parallel", "arbitrary"),
        name="nsa_attention",
    )(pc, kvc, kvct, pr4, pc, pr4, pc, block_onehot, win_bias, gl)


def _mm_res_kernel(a_ref, w_ref, r_ref, o_ref):
    o_ref[...] = _dot(a_ref[...], w_ref[...]) + r_ref[...]


def matmul_residual(a, w, res, tm, tn, weights_resident=False):
    m, k = a.shape
    n = w.shape[1]
    if weights_resident:
        grid = (n // tn, m // tm)
        row, col = (lambda j, i: i), (lambda j, i: j)
    else:
        grid = (m // tm, n // tn)
        row, col = (lambda i, j: i), (lambda i, j: j)
    return pl.pallas_call(
        _mm_res_kernel,
        grid=grid,
        in_specs=[pl.BlockSpec((tm, k), lambda p, q: (row(p, q), 0)),
                  pl.BlockSpec((k, tn), lambda p, q: (0, col(p, q))),
                  pl.BlockSpec((tm, tn), lambda p, q: (row(p, q), col(p, q)))],
        out_specs=pl.BlockSpec((tm, tn), lambda p, q: (row(p, q), col(p, q))),
        out_shape=jax.ShapeDtypeStruct((m, n), F32),
        compiler_params=_params("parallel", "arbitrary"),
        name="matmul_residual",
    )(a, w, res)


def _ffn_up_kernel(h_ref, wg_ref, wu_ref, o_ref):
    h = h_ref[...]
    g = _dot(h, wg_ref[...].astype(BF16))
    u = _dot(h, wu_ref[...].astype(BF16))
    o_ref[...] = (g * jax.nn.sigmoid(g) * u).astype(o_ref.dtype)


def ffn_up(h, wg, wu, layer, tm, tn):
    m, k = h.shape
    n = wg.shape[2]
    return pl.pallas_call(
        _ffn_up_kernel,
        grid=(m // tm, n // tn),
        in_specs=[pl.BlockSpec((tm, k), lambda i, j: (i, 0)),
                  pl.BlockSpec((None, k, tn), lambda i, j: (layer, 0, j)),
                  pl.BlockSpec((None, k, tn), lambda i, j: (layer, 0, j))],
        out_specs=pl.BlockSpec((tm, tn), lambda i, j: (i, j)),
        out_shape=jax.ShapeDtypeStruct((m, n), BF16),
        compiler_params=_params("parallel", "arbitrary"),
        name="ffn_up",
    )(h, wg, wu)


def _cast_kernel(w_ref, o_ref):
    o_ref[...] = w_ref[...].astype(o_ref.dtype)


def cast_columns(w, layer, n_cols, tk=512, tn=2816):
    _, k, _ = w.shape
    return pl.pallas_call(
        _cast_kernel,
        grid=(k // tk, n_cols // tn),
        in_specs=[pl.BlockSpec((None, tk, tn), lambda i, j: (layer, i, j))],
        out_specs=pl.BlockSpec((tk, tn), lambda i, j: (i, j)),
        out_shape=jax.ShapeDtypeStruct((k, n_cols), BF16),
        compiler_params=_params("parallel", "parallel"),
        name="cast_columns",
    )(w)


def _layer(x, norm_attn, w_in, layer, ck_pe, ck_w1, ck_w2, cv_pe, cv_w1, cv_w2,
           out_norm_a, out_norm_b, w_out, norm_ffn, w_gate, w_up, w_down, cos_t, sin_t, batch, seq):
    w_main = w_in.astype(BF16)
    w_gl = lax.slice(w_in, (layer, 0, D_MAIN), (layer + 1, D_MODEL, D_MAIN + N_GATES))
    w_gl = jnp.pad(w_gl.reshape(D_MODEL, N_GATES), ((0, 0), (0, LANES - N_GATES))).astype(BF16)
    pe = jnp.stack([ck_pe, cv_pe])
    w1 = jnp.stack([ck_w1, cv_w1]).reshape(2, CMP_LEN, HEAD_DIM, HEAD_DIM).astype(BF16)
    w2 = jnp.stack([ck_w2, cv_w2]).astype(BF16)

    h, gl = rms_gate(x, norm_attn, w_gl)
    pf = project(h, w_main, layer, cos_t, sin_t, F32_TILES, F32, batch, seq, layout="view")
    pr = project(h, w_main, layer, cos_t, sin_t, ROW_TILES, BF16, batch, seq)
    pc = project(h, w_main, layer, cos_t, sin_t, COL_TILES, BF16, batch, seq, layout="cols")
    o_a = dilated_attention(pf, batch, seq)
    kvc, kvct = compress(pf, pe, w1, w2, batch, seq)
    o_b = nsa_attention(pr, pc, kvc, kvct, gl, batch, seq)
    mixed = rms_pair(o_a, o_b, out_norm_a, out_norm_b)
    x1 = matmul_residual(mixed, w_out.astype(BF16), x, tm=1024, tn=1024)
    h2 = rms(x1, norm_ffn, BF16)
    act = ffn_up(h2, w_gate, w_up, layer, tm=2048, tn=256)
    x2 = matmul_residual(act, w_down.astype(BF16), x1, tm=512, tn=512, weights_resident=True)
    return x2


def kernel(x, norm_attn, w_in, ck_pe, ck_w1, ck_w2, cv_pe, cv_w1, cv_w2, out_norm_a, out_norm_b, w_out,
           norm_ffn, w_gate, w_up, w_down, norm_final):
    batch, seq, d = x.shape
    depth = w_in.shape[0]
    cos_t, sin_t = _rope_tables(seq)
    xf = x.reshape(batch * seq, d)
    for l in range(depth):
        xf = _layer(xf, norm_attn[l], w_in, l, ck_pe[l], ck_w1[l], ck_w2[l], cv_pe[l], cv_w1[l], cv_w2[l],
                    out_norm_a[l], out_norm_b[l], w_out[l], norm_ffn[l], w_gate, w_up, w_down[l],
                    cos_t, sin_t, batch, seq)
    return rms(xf, norm_final, F32).reshape(batch, seq, d)
```
